```python
import math
import numpy as np
import jax
import jax.numpy as jnp
from jax import lax

D_MODEL = 1024
BATCH = 8
SEQ = 4096
DEPTH = 1

EPS = 1e-6
DN_HEADS = 8
DN_HEAD_DIM = 128
DN_WIDTH = DN_HEADS * DN_HEAD_DIM
CONV_WIDTH = 4
CHUNK = 64
ATT_GROUPS = ((128, 1), (512, 4), (2048, 16))
ATT_HEADS_PER_GROUP = 4
ATT_HEAD_DIM = 128
ATT_N_HEADS = len(ATT_GROUPS) * ATT_HEADS_PER_GROUP
ATT_WIDTH = ATT_N_HEADS * ATT_HEAD_DIM
ATT_OUT_WIDTH = ATT_HEADS_PER_GROUP * ATT_HEAD_DIM
ATT_BLOCK = 128
ROPE_THETA = 500000.0
ROPE_DIM = ATT_HEAD_DIM // 4
D_FF = -(-8 * D_MODEL // (3 * 256)) * 256
IN_SIZES = (DN_WIDTH, DN_WIDTH, DN_WIDTH, DN_WIDTH, DN_HEADS, DN_HEADS,
            ATT_WIDTH, ATT_WIDTH, ATT_WIDTH, D_MODEL, D_MODEL)
IN_DIM = sum(IN_SIZES)

kernel_name = "hybrid_gated_deltanet_dilated_attn_block"


def rms_norm(x, w):
    x32 = x.astype(jnp.float32)
    y = x32 * lax.rsqrt(jnp.mean(x32 * x32, axis=-1, keepdims=True) + EPS)
    return (y * w.astype(jnp.float32)).astype(x.dtype)


def l2_normalize(x):
    x32 = x.astype(jnp.float32)
    return x32 * lax.rsqrt(jnp.sum(x32 * x32, axis=-1, keepdims=True) + EPS)


def causal_depthwise_conv(x, w):
    k_width = w.shape[0]
    seq = x.shape[1]
    xp = jnp.pad(x, ((0, 0), (k_width - 1, 0), (0, 0)))
    return sum(xp[:, j:j + seq] * w[j] for j in range(k_width))


def _to_chunks(t, n_chunks):
    b, _, h = t.shape[:3]
    t = t.reshape((b, n_chunks, CHUNK, h) + t.shape[3:])
    return jnp.swapaxes(jnp.swapaxes(t, 0, 1), 2, 3)


def gated_delta_rule(q, k, v, g, beta):
    f32 = jnp.float32
    b, seq, h, dk = q.shape
    dv = v.shape[-1]
    n = seq // CHUNK
    q, k, v = (_to_chunks(t.astype(f32), n) for t in (q, k, v))
    g, beta = (_to_chunks(t.astype(f32), n) for t in (g, beta))
    gc = jnp.cumsum(g, axis=-1)
    idx = jnp.arange(CHUNK)
    causal = idx[:, None] >= idx[None, :]
    strict = idx[:, None] > idx[None, :]
    decay = jnp.exp(jnp.where(causal, gc[..., :, None] - gc[..., None, :], -jnp.inf))
    kb = k * beta[..., None]
    vb = v * beta[..., None]
    a = jnp.where(strict, jnp.einsum('nbhcd,nbhsd->nbhcs', kb, k) * decay, 0.0)
    eye = jnp.eye(CHUNK, dtype=f32)
    t_inv = lax.linalg.triangular_solve(a + eye, jnp.broadcast_to(eye, a.shape),
                                        left_side=True, lower=True, unit_diagonal=True)
    u = jnp.einsum('nbhcs,nbhse->nbhce', t_inv, vb)
    w = jnp.einsum('nbhcs,nbhsd->nbhcd', t_inv, kb * jnp.exp(gc)[..., None])
    qk = jnp.einsum('nbhcd,nbhsd->nbhcs', q, k) * decay
    q_dec = q * jnp.exp(gc)[..., None]
    k_dec = k * jnp.exp(gc[..., -1:] - gc)[..., None]
    g_last = jnp.exp(gc[..., -1])

    def step(state, xs):
        u_n, w_n, qk_n, qd_n, kd_n, gl_n = xs
        v_new = u_n - jnp.einsum('bhcd,bhde->bhce', w_n, state)
        o_n = jnp.einsum('bhcd,bhde->bhce', qd_n, state) + jnp.einsum('bhcs,bhse->bhce', qk_n, v_new)
        state = state * gl_n[..., None, None] + jnp.einsum('bhcd,bhce->bhde', kd_n, v_new)
        return state, o_n

    s0 = jnp.zeros((b, h, dk, dv), f32)
    _, o = lax.scan(step, s0, (u, w, qk, q_dec, k_dec, g_last))
    o = jnp.swapaxes(jnp.swapaxes(o, 2, 3), 0, 1)
    return o.reshape(b, seq, h, dv)


def partial_rope(x, positions):
    half = ROPE_DIM // 2
    inv_freq = jnp.power(ROPE_THETA, -jnp.arange(half, dtype=jnp.float32) * (2.0 / ROPE_DIM))
    ang = positions.astype(jnp.float32)[:, None] * inv_freq[None, :]
    cos = jnp.cos(ang)[None, :, None, :]
    sin = jnp.sin(ang)[None, :, None, :]
    x1 = x[..., :half].astype(jnp.float32)
    x2 = x[..., half:ROPE_DIM].astype(jnp.float32)
    rest = x[..., ROPE_DIM:].astype(jnp.float32)
    return jnp.concatenate([x1 * cos - x2 * sin, x2 * cos + x1 * sin, rest], axis=-1).astype(x.dtype)


def dilated_window_attention(q, k, v, window, dilation):
    b, seq, h, d = q.shape
    n_keys = window // dilation
    span = dilation * ATT_BLOCK
    lp = -(-seq // span) * span
    m_len = lp // dilation
    nb = m_len // ATT_BLOCK

    def to_blocks(t):
        t = jnp.pad(t, ((0, 0), (0, lp - seq), (0, 0), (0, 0)))
        t = jnp.swapaxes(t.reshape(b, m_len, dilation, h, d), 1, 2)
        return t.reshape(b, dilation, nb, ATT_BLOCK, h, d)

    def with_prev(t):
        prev = jnp.pad(t, ((0, 0), (0, 0), (1, 0), (0, 0), (0, 0), (0, 0)))[:, :, :-1]
        return jnp.concatenate([prev, t], axis=3)

    def from_blocks(t):
        t = t.reshape((b, dilation, m_len) + t.shape[4:])
        t = jnp.swapaxes(t, 1, 2)
        return t.reshape((b, lp) + t.shape[3:])[:, :seq]

    qb = to_blocks(q)
    kc = with_prev(to_blocks(k))
    vc = with_prev(to_blocks(v))
    s = jnp.einsum('brnqhd,brnkhd->brnhqk', qb, kc, preferred_element_type=jnp.float32) * (d ** -0.5)
    qi = jnp.arange(ATT_BLOCK)[:, None]
    kj = jnp.arange(2 * ATT_BLOCK)[None, :]
    dist = qi + ATT_BLOCK - kj
    blk = jnp.arange(nb)[:, None, None]
    valid = (dist >= 0) & (dist <= n_keys) & ((blk > 0) | (kj >= ATT_BLOCK))
    s = jnp.where(valid[:, None], s, -jnp.inf)
    m = jnp.max(s, axis=-1, keepdims=True)
    p = jnp.exp(s - m)
    den = jnp.sum(p, axis=-1, keepdims=True)
    o = jnp.einsum('brnhqk,brnkhd->brnqhd', p / den, vc.astype(jnp.float32))
    lse = jnp.swapaxes((m + jnp.log(den))[..., 0], 3, 4)
    return from_blocks(o), from_blocks(lse)


def setup_inputs(seed: int = 0) -> dict:
    key = jax.random.key(seed)
    ks = jax.random.split(key, 16)
    f32 = jnp.float32

    def dense(k, shape, fan_in):
        return jax.random.normal(k, shape, f32) * (fan_in ** -0.5)

    def gain(k, shape):
        return 1.0 + 0.02 * jax.random.normal(k, shape, f32)

    x = jax.random.normal(ks[0], (BATCH, SEQ, D_MODEL), f32)
    norm1_w = gain(ks[1], (DEPTH, D_MODEL))
    w_in = dense(ks[2], (DEPTH, D_MODEL, IN_DIM), D_MODEL)
    conv_w = dense(ks[3], (DEPTH, CONV_WIDTH, 3 * DN_WIDTH), CONV_WIDTH)
    a_log = jnp.log(jax.random.uniform(ks[4], (DEPTH, DN_HEADS), f32, minval=1.0, maxval=16.0))
    dt = jnp.exp(jax.random.uniform(ks[5], (DEPTH, DN_HEADS), f32,
                                    minval=math.log(1e-3), maxval=math.log(1e-1)))
    dt_bias = dt + jnp.log(-jnp.expm1(-dt))
    dn_norm_w = gain(ks[6], (DEPTH, DN_HEAD_DIM))
    w_proj_a = dense(ks[7], (DEPTH, DN_WIDTH, D_MODEL), DN_WIDTH)
    w_proj_b = dense(ks[8], (DEPTH, ATT_OUT_WIDTH, D_MODEL), ATT_OUT_WIDTH)
    w_out = dense(ks[9], (DEPTH, D_MODEL, D_MODEL), D_MODEL)
    norm2_w = gain(ks[10], (DEPTH, D_MODEL))
    w_gate_up = dense(ks[11], (DEPTH, D_MODEL, 2 * D_FF), D_MODEL)
    w_down = dense(ks[12], (DEPTH, D_FF, D_MODEL), D_FF)
    final_norm_w = gain(ks[13], (D_MODEL,))
    return {"x": x, "norm1_w": norm1_w, "w_in": w_in, "conv_w": conv_w, "a_log": a_log,
            "dt_bias": dt_bias, "dn_norm_w": dn_norm_w, "w_proj_a": w_proj_a, "w_proj_b": w_proj_b,
            "w_out": w_out, "norm2_w": norm2_w, "w_gate_up": w_gate_up, "w_down": w_down,
            "final_norm_w": final_norm_w}


def reference(x, norm1_w, w_in, conv_w, a_log, dt_bias, dn_norm_w, w_proj_a, w_proj_b,
              w_out, norm2_w, w_gate_up, w_down, final_norm_w):
    f32 = jnp.float32
    b, seq, _ = x.shape
    positions = jnp.arange(seq)
    splits = np.cumsum(IN_SIZES)[:-1].tolist()
    for i in range(DEPTH):
        h = rms_norm(x, norm1_w[i])
        proj = h @ w_in[i]
        dq, dk, dv, dz, db, da, aq, ak, av, ga, gb = jnp.split(proj, splits, axis=-1)

        qkv = jax.nn.silu(causal_depthwise_conv(jnp.concatenate([dq, dk, dv], axis=-1), conv_w[i]))
        cq, ck, cv = jnp.split(qkv, 3, axis=-1)
        q = l2_normalize(cq.reshape(b, seq, DN_HEADS, DN_HEAD_DIM)) * (DN_HEAD_DIM ** -0.5)
        k = l2_normalize(ck.reshape(b, seq, DN_HEADS, DN_HEAD_DIM))
        v = cv.reshape(b, seq, DN_HEADS, DN_HEAD_DIM)
        beta = jax.nn.sigmoid(db.astype(f32))
        g = -jnp.exp(a_log[i].astype(f32)) * jax.nn.softplus(da.astype(f32) + dt_bias[i].astype(f32))
        o_a = gated_delta_rule(q, k, v, g, beta)
        o_a = rms_norm(o_a, dn_norm_w[i]) * jax.nn.silu(dz.reshape(b, seq, DN_HEADS, DN_HEAD_DIM).astype(f32))
        y_a = o_a.reshape(b, seq, DN_WIDTH).astype(x.dtype) @ w_proj_a[i]

        qr = partial_rope(aq.reshape(b, seq, ATT_N_HEADS, ATT_HEAD_DIM), positions)
        kr = partial_rope(ak.reshape(b, seq, ATT_N_HEADS, ATT_HEAD_DIM), positions)
        vv = av.reshape(b, seq, ATT_N_HEADS, ATT_HEAD_DIM)
        outs, lses = [], []
        for gi, (window, dil) in enumerate(ATT_GROUPS):
            sl = slice(gi * ATT_HEADS_PER_GROUP, (gi + 1) * ATT_HEADS_PER_GROUP)
            o_g, lse_g = dilated_window_attention(qr[:, :, sl], kr[:, :, sl], vv[:, :, sl], window, dil)
            outs.append(o_g)
            lses.append(lse_g)
        alpha = jax.nn.softmax(jnp.stack(lses), axis=0)
        o_b = jnp.sum(alpha[..., None] * jnp.stack(outs), axis=0)
        y_b = o_b.reshape(b, seq, ATT_OUT_WIDTH).astype(x.dtype) @ w_proj_b[i]

        merged = jax.nn.sigmoid(ga) * y_a + jax.nn.sigmoid(gb) * y_b
        x = x + merged @ w_out[i]

        h2 = rms_norm(x, norm2_w[i])
        gate, up = jnp.split(h2 @ w_gate_up[i], 2, axis=-1)
        x = x + (jax.nn.silu(gate) * up) @ w_down[i]
    return rms_norm(x, final_norm_w)
```

```python
import functools
import math

import jax
import jax.numpy as jnp
from jax import lax
from jax.experimental import pallas as pl
from jax.experimental.pallas import tpu as pltpu

F32 = jnp.float32
BF16 = jnp.bfloat16

EPS = 1e-6
LANES = 128
HEAD_DIM = 128
DN_HEADS = 8
DN_WIDTH = DN_HEADS * HEAD_DIM
CONV_WIDTH = 4
CHUNK = 64
ATT_GROUPS = ((128, 1), (512, 4), (2048, 16))
ATT_HEADS_PER_GROUP = 4
ATT_N_HEADS = len(ATT_GROUPS) * ATT_HEADS_PER_GROUP
ATT_WIDTH = ATT_N_HEADS * HEAD_DIM
ATT_OUT_WIDTH = ATT_HEADS_PER_GROUP * HEAD_DIM
ATT_BLOCK = 128
ROPE_THETA = 500000.0
ROPE_DIM = HEAD_DIM // 4
NEG_BIG = -1e30

CB_DQ, CB_DK, CB_DV, CB_DZ = 0, 8, 16, 24
CB_GA, CB_GB = 32, 40
CB_AQ, CB_AK, CB_AV = 48, 60, 72
MAIN_WIDTH = 84 * LANES

VMEM_LIMIT = 56 * 1024 * 1024


def _cparams(sem):
    return pltpu.CompilerParams(dimension_semantics=sem, vmem_limit_bytes=VMEM_LIMIT)


def _dot(a, b):
    return jnp.dot(a, b, preferred_element_type=F32)


def _dot_nt(a, b):
    return lax.dot_general(a, b, (((1,), (1,)), ((), ())), preferred_element_type=F32)


def _dot_f32(a, b):
    return jnp.dot(a, b, preferred_element_type=F32, precision=lax.Precision.HIGHEST)


def _sigmoid(x):
    return 1.0 / (1.0 + jnp.exp(-x))


def _inproj_kernel(x_ref, nw_ref, w_ref, ws_ref, o_ref, os_ref, h_ref):
    @pl.when(pl.program_id(1) == 0)
    def _():
        x = x_ref[...]
        h = x * lax.rsqrt(jnp.mean(x * x, axis=-1, keepdims=True) + EPS) * nw_ref[...]
        hb = h.astype(BF16)
        h_ref[...] = hb
        os_ref[...] = _dot(hb, ws_ref[...])

    o_ref[...] = _dot(h_ref[...], w_ref[...])


def _in_projection(x2, norm_w, w_main, w_small, *, tm, tn):
    t, d = x2.shape
    n = w_main.shape[1]
    return pl.pallas_call(
        _inproj_kernel,
        grid=(t // tm, n // tn),
        in_specs=[
            pl.BlockSpec((tm, d), lambda i, j: (i, 0)),
            pl.BlockSpec((1, d), lambda i, j: (0, 0)),
            pl.BlockSpec((d, tn), lambda i, j: (0, j)),
            pl.BlockSpec((d, LANES), lambda i, j: (0, 0)),
        ],
        out_specs=[
            pl.BlockSpec((tm, tn), lambda i, j: (i, j)),
            pl.BlockSpec((tm, LANES), lambda i, j: (i, 0)),
        ],
        out_shape=[
            jax.ShapeDtypeStruct((t, n), F32),
            jax.ShapeDtypeStruct((t, LANES), F32),
        ],
        scratch_shapes=[pltpu.VMEM((tm, d), BF16)],
        compiler_params=_cparams(("parallel", "arbitrary")),
        name="in_projection",
    )(x2, norm_w, w_main, w_small)


def _deltanet_kernel(q_ref, k_ref, v_ref, z_ref, sm_ref, cwq_ref, cwk_ref, cwv_ref,
                     alog_ref, dtb_ref, nw_ref, o_ref,
                     s_scr, xq_scr, xk_scr, xv_scr, gt_scr, gcb_scr, bb_scr,
                     u_scr, w_scr, qk_scr, *, lb):
    head = pl.program_id(1)
    lstep = pl.program_id(2)
    nc = lb // CHUNK
    pad = 8

    @pl.when(lstep == 0)
    def _():
        s_scr[...] = jnp.zeros_like(s_scr)
        for scr in (xq_scr, xk_scr, xv_scr):
            scr[0:pad, :] = jnp.zeros((pad, HEAD_DIM), F32)

    def conv_silu(x_ref, scr, cw_ref):
        scr[pad:pad + lb, :] = x_ref[...]
        acc = None
        for j in range(CONV_WIDTH):
            off = pad - (CONV_WIDTH - 1) + j
            term = scr[off:off + lb, :] * cw_ref[j:j + 1, :]
            acc = term if acc is None else acc + term
        scr[0:pad, :] = scr[lb:lb + pad, :]
        return acc * _sigmoid(acc)

    cq = conv_silu(q_ref, xq_scr, cwq_ref)
    ck = conv_silu(k_ref, xk_scr, cwk_ref)
    cv = conv_silu(v_ref, xv_scr, cwv_ref)
    qn = cq * lax.rsqrt(jnp.sum(cq * cq, axis=-1, keepdims=True) + EPS) * (HEAD_DIM ** -0.5)
    kn = ck * lax.rsqrt(jnp.sum(ck * ck, axis=-1, keepdims=True) + EPS)
    xq_scr[pad:pad + lb, :] = qn
    xk_scr[pad:pad + lb, :] = kn
    xv_scr[pad:pad + lb, :] = cv

    sm = sm_ref[...]
    beta_all = _sigmoid(sm)
    xs = sm + dtb_ref[...]
    softplus = jnp.maximum(xs, 0.0) + jnp.log(1.0 + jnp.exp(-jnp.abs(xs)))
    g_all = -jnp.exp(alog_ref[...]) * softplus
    row = lax.broadcasted_iota(jnp.int32, (lb, LANES), 0)
    rmod = jnp.bitwise_and(row, CHUNK - 1)
    gc_all = g_all
    shift = 1
    while shift < CHUNK:
        gc_all = gc_all + jnp.where(rmod >= shift, pltpu.roll(gc_all, shift, axis=0), 0.0)
        shift *= 2
    lane = lax.broadcasted_iota(jnp.int32, (lb, LANES), 1)
    gcb_scr[...] = jnp.broadcast_to(
        jnp.sum(jnp.where(lane == head + DN_HEADS, gc_all, 0.0), axis=1, keepdims=True), (lb, LANES))
    bb_scr[...] = jnp.broadcast_to(
        jnp.sum(jnp.where(lane == head, beta_all, 0.0), axis=1, keepdims=True), (lb, LANES))
    for c in range(nc):
        gt_scr[c] = gc_all[c * CHUNK:(c + 1) * CHUNK, :].T

    ii = lax.broadcasted_iota(jnp.int32, (CHUNK, CHUNK), 0)
    jj = lax.broadcasted_iota(jnp.int32, (CHUNK, CHUNK), 1)
    eye = (ii == jj).astype(F32)

    def prep(c, carry):
        r0 = pl.multiple_of(c * CHUNK, CHUNK)
        q = xq_scr[pl.ds(pad + r0, CHUNK), :]
        k = xk_scr[pl.ds(pad + r0, CHUNK), :]
        v = xv_scr[pl.ds(pad + r0, CHUNK), :]
        gcb = gcb_scr[pl.ds(r0, CHUNK), :]
        bb = bb_scr[pl.ds(r0, CHUNK), :]
        g_row = gt_scr[c, pl.ds(head + DN_HEADS, 1), :]
        g_last = gcb_scr[pl.ds(r0 + CHUNK - 1, 1), :]
        diff = gcb[:, :CHUNK] - g_row
        decay = jnp.where(ii >= jj, jnp.exp(jnp.minimum(diff, 0.0)), 0.0)
        kb = k * bb
        kbf = k.astype(BF16)
        a = jnp.where(ii > jj, _dot_nt(kb.astype(BF16), kbf) * decay, 0.0)
        m = -a
        t_inv = eye + m
        for _ in range(5):
            m = _dot_f32(m, m)
            t_inv = t_inv + _dot_f32(t_inv, m)
        eg = jnp.exp(gcb)
        tb = t_inv.astype(BF16)
        u_scr[pl.ds(r0, CHUNK), :] = _dot(tb, (v * bb).astype(BF16))
        w_scr[pl.ds(r0, CHUNK), :] = _dot(tb, (kb * eg).astype(BF16))
        qk_scr[pl.ds(r0, CHUNK), :] = _dot_nt(q.astype(BF16), kbf) * decay
        xq_scr[pl.ds(pad + r0, CHUNK), :] = q * eg
        xk_scr[pl.ds(pad + r0, CHUNK), :] = k * jnp.exp(g_last - gcb)
        return carry

    lax.fori_loop(0, nc, prep, 0)

    def step(c, carry):
        r0 = pl.multiple_of(c * CHUNK, CHUNK)
        s = s_scr[...]
        sb = s.astype(BF16)
        g_last = gcb_scr[pl.ds(r0 + CHUNK - 1, 1), :]
        v_new = u_scr[pl.ds(r0, CHUNK), :] - _dot(w_scr[pl.ds(r0, CHUNK), :].astype(BF16), sb)
        vb = v_new.astype(BF16)
        o = (_dot(xq_scr[pl.ds(pad + r0, CHUNK), :].astype(BF16), sb)
             + _dot(qk_scr[pl.ds(r0, CHUNK), :].astype(BF16), vb))
        kd_t = xk_scr[pl.ds(pad + r0, CHUNK), :].T.astype(BF16)
        s_scr[...] = s * jnp.exp(g_last) + _dot(kd_t, vb)
        on = o * lax.rsqrt(jnp.mean(o * o, axis=-1, keepdims=True) + EPS) * nw_ref[...]
        z = z_ref[pl.ds(r0, CHUNK), :]
        o_ref[pl.ds(r0, CHUNK), :] = (on * (z * _sigmoid(z))).astype(o_ref.dtype)
        return carry

    lax.fori_loop(0, nc, step, 0)


def _deltanet(main, small, conv_w, alog_row, dtb_row, dn_norm_w, *, batch, seq, lb):
    t = batch * seq
    nlb = seq // lb

    def col(cb):
        return pl.BlockSpec((lb, HEAD_DIM), lambda b, h, l, cb=cb: (b * nlb + l, cb + h))

    def cw(cb):
        return pl.BlockSpec((CONV_WIDTH, HEAD_DIM), lambda b, h, l, cb=cb: (0, cb + h))

    row = pl.BlockSpec((1, LANES), lambda b, h, l: (0, 0))
    return pl.pallas_call(
        functools.partial(_deltanet_kernel, lb=lb),
        grid=(batch, DN_HEADS, nlb),
        in_specs=[col(CB_DQ), col(CB_DK), col(CB_DV), col(CB_DZ),
                  pl.BlockSpec((lb, LANES), lambda b, h, l: (b * nlb + l, 0)),
                  cw(0), cw(DN_HEADS), cw(2 * DN_HEADS), row, row, row],
        out_specs=pl.BlockSpec((lb, HEAD_DIM), lambda b, h, l: (b * nlb + l, h)),
        out_shape=jax.ShapeDtypeStruct((t, DN_WIDTH), BF16),
        scratch_shapes=[
            pltpu.VMEM((HEAD_DIM, HEAD_DIM), F32),
            pltpu.VMEM((lb + 8, HEAD_DIM), F32),
            pltpu.VMEM((lb + 8, HEAD_DIM), F32),
            pltpu.VMEM((lb + 8, HEAD_DIM), F32),
            pltpu.VMEM((lb // CHUNK, LANES, CHUNK), F32),
            pltpu.VMEM((lb, LANES), F32),
            pltpu.VMEM((lb, LANES), F32),
            pltpu.VMEM((lb, HEAD_DIM), F32),
            pltpu.VMEM((lb, HEAD_DIM), F32),
            pltpu.VMEM((lb, CHUNK), F32),
        ],
        compiler_params=_cparams(("parallel", "parallel", "arbitrary")),
        name="deltanet",
    )(main, main, main, main, small, conv_w, conv_w, conv_w, alog_row, dtb_row, dn_norm_w)


def _attention_kernel(q_ref, k_ref, v_ref, tc_ref, ts_ref, o_ref,
                      qr_scr, kr_scr, og_scr, lse_scr, *, seq):
    group = pl.program_id(2)
    rows = 256
    lane = lax.broadcasted_iota(jnp.int32, (rows, LANES), 1)
    half = ROPE_DIM // 2

    def rope(i, carry):
        r0 = pl.multiple_of(i * rows, rows)
        tc = tc_ref[pl.ds(r0, rows), :]
        ts = ts_ref[pl.ds(r0, rows), :]
        for src, dst, scale in ((q_ref, qr_scr, HEAD_DIM ** -0.5), (k_ref, kr_scr, None)):
            x = src[pl.ds(r0, rows), :]
            partner = jnp.where(lane < half, pltpu.roll(x, LANES - half, axis=1),
                                pltpu.roll(x, half, axis=1))
            y = x * tc + partner * ts
            dst[pl.ds(r0, rows), :] = y if scale is None else y * scale
        return carry

    lax.fori_loop(0, seq // rows, rope, 0)

    ii = lax.broadcasted_iota(jnp.int32, (ATT_BLOCK, ATT_BLOCK), 0)
    jj = lax.broadcasted_iota(jnp.int32, (ATT_BLOCK, ATT_BLOCK), 1)

    def run_group(gi, dil):
        nb = seq // (dil * ATT_BLOCK)
        span = dil * ATT_BLOCK

        def rows_of(start):
            if dil == 1:
                return pl.ds(start, ATT_BLOCK)
            return pl.ds(start, ATT_BLOCK, stride=dil)

        def block(t, carry):
            r = lax.shift_right_logical(t, nb.bit_length() - 1)
            n = jnp.bitwise_and(t, nb - 1)
            start = r + n * span
            prev = jnp.maximum(start - span, r)
            qb = qr_scr[rows_of(start), :].astype(BF16)
            s_cur = _dot_nt(qb, kr_scr[rows_of(start), :].astype(BF16))
            s_prev = _dot_nt(qb, kr_scr[rows_of(prev), :].astype(BF16))
            s_cur = jnp.where(jj <= ii, s_cur, NEG_BIG)
            s_prev = jnp.where(jj - ii >= jnp.where(n > 0, 0, ATT_BLOCK), s_prev, NEG_BIG)
            m = jnp.maximum(jnp.max(s_cur, axis=1, keepdims=True),
                            jnp.max(s_prev, axis=1, keepdims=True))
            p_cur = jnp.exp(s_cur - m)
            p_prev = jnp.exp(s_prev - m)
            den = jnp.sum(p_cur, axis=1, keepdims=True) + jnp.sum(p_prev, axis=1, keepdims=True)
            acc = (_dot(p_cur.astype(BF16), v_ref[rows_of(start), :].astype(BF16))
                   + _dot(p_prev.astype(BF16), v_ref[rows_of(prev), :].astype(BF16)))
            og_scr[gi, rows_of(start), :] = acc / den
            lse_scr[gi, rows_of(start), :] = jnp.broadcast_to(m + jnp.log(den), (ATT_BLOCK, LANES))
            return carry

        lax.fori_loop(0, seq // ATT_BLOCK, block, 0)

    for gi, (_, dil) in enumerate(ATT_GROUPS):
        @pl.when(group == gi)
        def _(gi=gi, dil=dil):
            run_group(gi, dil)

    @pl.when(group == len(ATT_GROUPS) - 1)
    def _():
        def merge(i, carry):
            r0 = pl.multiple_of(i * rows, rows)
            ls = [lse_scr[g, pl.ds(r0, rows), :] for g in range(len(ATT_GROUPS))]
            mx = functools.reduce(jnp.maximum, ls)
            es = [jnp.exp(l - mx) for l in ls]
            num = functools.reduce(
                lambda a, b: a + b, [e * og_scr[g, pl.ds(r0, rows), :] for g, e in enumerate(es)])
            den = functools.reduce(lambda a, b: a + b, es)
            o_ref[pl.ds(r0, rows), :] = (num / den).astype(o_ref.dtype)
            return carry

        lax.fori_loop(0, seq // rows, merge, 0)


def _attention(main, rope_cos, rope_sin, *, batch, seq):
    t = batch * seq
    ng = len(ATT_GROUPS)

    def col(cb):
        return pl.BlockSpec((seq, HEAD_DIM),
                            lambda b, h, g, cb=cb: (b, cb + g * ATT_HEADS_PER_GROUP + h))

    tab = pl.BlockSpec((seq, LANES), lambda b, h, g: (0, 0))
    return pl.pallas_call(
        functools.partial(_attention_kernel, seq=seq),
        grid=(batch, ATT_HEADS_PER_GROUP, ng),
        in_specs=[col(CB_AQ), col(CB_AK), col(CB_AV), tab, tab],
        out_specs=pl.BlockSpec((seq, HEAD_DIM), lambda b, h, g: (b, h)),
        out_shape=jax.ShapeDtypeStruct((t, ATT_OUT_WIDTH), BF16),
        scratch_shapes=[
            pltpu.VMEM((seq, HEAD_DIM), F32),
            pltpu.VMEM((seq, HEAD_DIM), F32),
            pltpu.VMEM((ng, seq, HEAD_DIM), F32),
            pltpu.VMEM((ng, seq, LANES), F32),
        ],
        compiler_params=_cparams(("parallel", "parallel", "arbitrary")),
        name="dilated_attention",
    )(main, main, main, rope_cos, rope_sin)


def _rope_tables(seq):
    half = ROPE_DIM // 2
    inv_freq = jnp.power(ROPE_THETA, -jnp.arange(half, dtype=F32) * (2.0 / ROPE_DIM))
    ang = jnp.arange(seq).astype(F32)[:, None] * inv_freq[None, :]
    cos, sin = jnp.cos(ang), jnp.sin(ang)
    rest = HEAD_DIM - ROPE_DIM
    tc = jnp.concatenate([cos, cos, jnp.ones((seq, rest), F32)], axis=1)
    ts = jnp.concatenate([-sin, sin, jnp.zeros((seq, rest), F32)], axis=1)
    return tc, ts


def _merge_kernel(x_ref, oa_ref, ob_ref, ga_ref, gb_ref, wa_ref, wb_ref, wo_ref, o_ref):
    ya = _dot(oa_ref[...], wa_ref[...])
    yb = _dot(ob_ref[...], wb_ref[...])
    merged = _sigmoid(ga_ref[...]) * ya + _sigmoid(gb_ref[...]) * yb
    o_ref[...] = x_ref[...] + _dot(merged.astype(BF16), wo_ref[...])


def _merge(x2, oa, ob, main, w_a, w_b, w_o, *, tm):
    t, d = x2.shape
    gate_cb = CB_GA * LANES // d
    const = lambda shape: pl.BlockSpec(shape, lambda i: (0, 0))
    return pl.pallas_call(
        _merge_kernel,
        grid=(t // tm,),
        in_specs=[
            pl.BlockSpec((tm, d), lambda i: (i, 0)),
            pl.BlockSpec((tm, DN_WIDTH), lambda i: (i, 0)),
            pl.BlockSpec((tm, ATT_OUT_WIDTH), lambda i: (i, 0)),
            pl.BlockSpec((tm, d), lambda i: (i, gate_cb)),
            pl.BlockSpec((tm, d), lambda i: (i, gate_cb + 1)),
            const(w_a.shape), const(w_b.shape), const(w_o.shape),
        ],
        out_specs=pl.BlockSpec((tm, d), lambda i: (i, 0)),
        out_shape=jax.ShapeDtypeStruct((t, d), F32),
        compiler_params=_cparams(("parallel",)),
        name="branch_merge",
    )(x2, oa, ob, main, main, w_a, w_b, w_o)


def _ffn_kernel(x_ref, nw_ref, wg_ref, wu_ref, wd_ref, fw_ref, o_ref, h_ref, acc_ref, *, final_norm):
    f = pl.program_id(1)

    @pl.when(f == 0)
    def _():
        x = x_ref[...]
        h = x * lax.rsqrt(jnp.mean(x * x, axis=-1, keepdims=True) + EPS) * nw_ref[...]
        h_ref[...] = h.astype(BF16)
        acc_ref[...] = x

    h = h_ref[...]
    gate = _dot(h, wg_ref[...])
    up = _dot(h, wu_ref[...])
    act = gate * _sigmoid(gate) * up
    acc_ref[...] += _dot(act.astype(BF16), wd_ref[...])

    @pl.when(f == pl.num_programs(1) - 1)
    def _():
        y = acc_ref[...]
        if final_norm:
            y = y * lax.rsqrt(jnp.mean(y * y, axis=-1, keepdims=True) + EPS) * fw_ref[...]
        o_ref[...] = y


def _ffn(x1, norm_w, w_gate_up, w_down, final_w, *, tm, tf, final_norm):
    t, d = x1.shape
    d_ff = w_down.shape[0]
    nf = d_ff // tf
    return pl.pallas_call(
        functools.partial(_ffn_kernel, final_norm=final_norm),
        grid=(t // tm, nf),
        in_specs=[
            pl.BlockSpec((tm, d), lambda i, f: (i, 0)),
            pl.BlockSpec((1, d), lambda i, f: (0, 0)),
            pl.BlockSpec((d, tf), lambda i, f: (0, f)),
            pl.BlockSpec((d, tf), lambda i, f: (0, nf + f)),
            pl.BlockSpec((tf, d), lambda i, f: (f, 0)),
            pl.BlockSpec((1, d), lambda i, f: (0, 0)),
        ],
        out_specs=pl.BlockSpec((tm, d), lambda i, f: (i, 0)),
        out_shape=jax.ShapeDtypeStruct((t, d), F32),
        scratch_shapes=[pltpu.VMEM((tm, d), BF16), pltpu.VMEM((tm, d), F32)],
        compiler_params=_cparams(("parallel", "arbitrary")),
        name="swiglu_ffn",
    )(x1, norm_w, w_gate_up, w_gate_up, w_down, final_w)


def _pad_lanes(v, offset):
    out = jnp.zeros((1, LANES), F32)
    return lax.dynamic_update_slice(out, v.reshape(1, -1).astype(F32), (0, offset))


def kernel(x, norm1_w, w_in, conv_w, a_log, dt_bias, dn_norm_w, w_proj_a, w_proj_b, w_out,
           norm2_w, w_gate_up, w_down, final_norm_w):
    batch, seq, d_model = x.shape
    depth = w_in.shape[0]
    t = batch * seq
    assert d_model == 8 * LANES and seq % (ATT_GROUPS[-1][1] * ATT_BLOCK) == 0
    d_ff = w_down.shape[1]

    o_dz_end = 4 * DN_WIDTH
    o_small_end = o_dz_end + 2 * DN_HEADS
    o_att_end = o_small_end + 3 * ATT_WIDTH

    rope_cos, rope_sin = _rope_tables(seq)
    x2 = x.reshape(t, d_model)
    for i in range(depth):
        wi = w_in[i]
        w_main = jnp.concatenate(
            [wi[:, :o_dz_end], wi[:, o_att_end:], wi[:, o_small_end:o_att_end]], axis=1).astype(BF16)
        w_small = jnp.pad(wi[:, o_dz_end:o_small_end],
                          ((0, 0), (0, LANES - 2 * DN_HEADS))).astype(BF16)
        main, small = _in_projection(x2, norm1_w[i].reshape(1, d_model), w_main, w_small,
                                     tm=1024, tn=MAIN_WIDTH // 6)
        oa = _deltanet(main, small, conv_w[i], _pad_lanes(a_log[i], DN_HEADS),
                       _pad_lanes(dt_bias[i], DN_HEADS), dn_norm_w[i].reshape(1, HEAD_DIM),
                       batch=batch, seq=seq, lb=512)
        ob = _attention(main, rope_cos, rope_sin, batch=batch, seq=seq)
        x2 = _merge(x2, oa, ob, main, w_proj_a[i].astype(BF16), w_proj_b[i].astype(BF16),
                    w_out[i].astype(BF16), tm=512)
        x2 = _ffn(x2, norm2_w[i].reshape(1, d_model), w_gate_up[i].astype(BF16),
                  w_down[i].astype(BF16), final_norm_w.reshape(1, d_model), tm=512, tf=d_ff // 2,
                  final_norm=(i == depth - 1))
    return x2.reshape(batch, seq, d_model)
```

```python
import functools
import math

import jax
import jax.numpy as jnp
from jax import lax
from jax.experimental import pallas as pl
from jax.experimental.pallas import tpu as pltpu

F32 = jnp.float32
BF16 = jnp.bfloat16

EPS = 1e-6
LANES = 128
HEAD_DIM = 128
DN_HEADS = 8
DN_WIDTH = DN_HEADS * HEAD_DIM
CONV_WIDTH = 4
CHUNK = 64
ATT_GROUPS = ((128, 1), (512, 4), (2048, 16))
ATT_HEADS_PER_GROUP = 4
ATT_N_HEADS = len(ATT_GROUPS) * ATT_HEADS_PER_GROUP
ATT_WIDTH = ATT_N_HEADS * HEAD_DIM
ATT_OUT_WIDTH = ATT_HEADS_PER_GROUP * HEAD_DIM
ATT_BLOCK = 128
ROPE_THETA = 500000.0
ROPE_DIM = HEAD_DIM // 4
NEG_BIG = -1e30

CB_DQ, CB_DK, CB_DV, CB_DZ = 0, 8, 16, 24
CB_GA, CB_GB = 32, 40
CB_AQ, CB_AK, CB_AV = 48, 60, 72
MAIN_WIDTH = 84 * LANES

VMEM_LIMIT = 56 * 1024 * 1024


def _cparams(sem):
    return pltpu.CompilerParams(dimension_semantics=sem, vmem_limit_bytes=VMEM_LIMIT)


def _dot(a, b):
    return jnp.dot(a, b, preferred_element_type=F32)


def _dot_nt(a, b):
    return lax.dot_general(a, b, (((1,), (1,)), ((), ())), preferred_element_type=F32)


def _dot_f32(a, b):
    return jnp.dot(a, b, preferred_element_type=F32, precision=lax.Precision.HIGHEST)


def _sigmoid(x):
    return 1.0 / (1.0 + jnp.exp(-x))


def _inproj_kernel(x_ref, nw_ref, w_ref, ws_ref, o_ref, os_ref, h_ref):
    @pl.when(pl.program_id(1) == 0)
    def _():
        x = x_ref[...]
        h = x * lax.rsqrt(jnp.mean(x * x, axis=-1, keepdims=True) + EPS) * nw_ref[...]
        hb = h.astype(BF16)
        h_ref[...] = hb
        os_ref[...] = _dot(hb, ws_ref[...])

    o_ref[...] = _dot(h_ref[...], w_ref[...])


def _in_projection(x2, norm_w, w_main, w_small, *, tm, tn):
    t, d = x2.shape
    n = w_main.shape[1]
    return pl.pallas_call(
        _inproj_kernel,
        grid=(t // tm, n // tn),
        in_specs=[
            pl.BlockSpec((tm, d), lambda i, j: (i, 0)),
            pl.BlockSpec((1, d), lambda i, j: (0, 0)),
            pl.BlockSpec((d, tn), lambda i, j: (0, j)),
            pl.BlockSpec((d, LANES), lambda i, j: (0, 0)),
        ],
        out_specs=[
            pl.BlockSpec((tm, tn), lambda i, j: (i, j)),
            pl.BlockSpec((tm, LANES), lambda i, j: (i, 0)),
        ],
        out_shape=[
            jax.ShapeDtypeStruct((t, n), F32),
            jax.ShapeDtypeStruct((t, LANES), F32),
        ],
        scratch_shapes=[pltpu.VMEM((tm, d), BF16)],
        compiler_params=_cparams(("parallel", "arbitrary")),
        name="in_projection",
    )(x2, norm_w, w_main, w_small)


def _deltanet_kernel(q_ref, k_ref, v_ref, z_ref, sm_ref, cw_ref, alog_ref, dtb_ref, nw_ref, o_ref,
                     s_scr, tail_scr, u_scr, wq_scr, qk_scr, kdt_scr, egl_scr, *, lb):
    lstep = pl.program_id(1)
    nc = lb // CHUNK
    width = DN_WIDTH
    look = 8

    @pl.when(lstep == 0)
    def _():
        s_scr[...] = jnp.zeros_like(s_scr)
        tail_scr[...] = jnp.zeros_like(tail_scr)

    ii = lax.broadcasted_iota(jnp.int32, (CHUNK, CHUNK), 0)
    jj = lax.broadcasted_iota(jnp.int32, (CHUNK, CHUNK), 1)
    eye = (ii == jj).astype(F32)
    rowi = lax.broadcasted_iota(jnp.int32, (CHUNK, LANES), 0)

    def conv_silu(x_ref, which, c, r0):
        prev0 = pl.multiple_of(jnp.maximum(r0 - look, 0), look)
        top = lax.select(c == 0, tail_scr[which], x_ref[pl.ds(prev0, look), :])
        win = jnp.concatenate([top, x_ref[pl.ds(r0, CHUNK), :]], axis=0)
        acc = None
        for j in range(CONV_WIDTH):
            off = look - (CONV_WIDTH - 1) + j
            term = win[off:off + CHUNK, :] * cw_ref[j:j + 1, which * width:(which + 1) * width]
            acc = term if acc is None else acc + term
        return acc * _sigmoid(acc)

    def prep(c, carry):
        r0 = pl.multiple_of(c * CHUNK, CHUNK)
        cq = conv_silu(q_ref, 0, c, r0)
        ck = conv_silu(k_ref, 1, c, r0)
        cv = conv_silu(v_ref, 2, c, r0)
        sm = sm_ref[pl.ds(r0, CHUNK), :]
        beta_all = _sigmoid(sm)
        xs = sm + dtb_ref[...]
        softplus = jnp.maximum(xs, 0.0) + jnp.log(1.0 + jnp.exp(-jnp.abs(xs)))
        gc_all = -jnp.exp(alog_ref[...]) * softplus
        shift = 1
        while shift < CHUNK:
            gc_all = gc_all + jnp.where(rowi >= shift, pltpu.roll(gc_all, shift, axis=0), 0.0)
            shift *= 2
        gc_t = gc_all.T
        heads = range(DN_HEADS)
        qs, ks, kbs, egs, rhs, ns, qks = [], [], [], [], [], [], []
        for h in heads:
            cols = slice(h * HEAD_DIM, (h + 1) * HEAD_DIM)
            q, k, v = cq[:, cols], ck[:, cols], cv[:, cols]
            q = q * (lax.rsqrt(jnp.sum(q * q, axis=-1, keepdims=True) + EPS) * (HEAD_DIM ** -0.5))
            k = k * lax.rsqrt(jnp.sum(k * k, axis=-1, keepdims=True) + EPS)
            bb = jnp.broadcast_to(beta_all[:, h:h + 1], (CHUNK, LANES))
            gcb = jnp.broadcast_to(gc_all[:, DN_HEADS + h:DN_HEADS + h + 1], (CHUNK, LANES))
            g_row = gc_t[DN_HEADS + h:DN_HEADS + h + 1, :]
            g_last = gcb[CHUNK - 1:CHUNK, :]
            decay = jnp.where(ii >= jj, jnp.exp(jnp.minimum(gcb[:, :CHUNK] - g_row, 0.0)), 0.0)
            kb = k * bb
            eg = jnp.exp(gcb)
            both = _dot_nt(jnp.concatenate([kb, q], axis=0).astype(BF16), k.astype(BF16))
            ns.append(jnp.where(ii > jj, -both[:CHUNK] * decay, 0.0))
            qk_scr[c, h] = (both[CHUNK:] * decay).astype(BF16)
            kdt_scr[c, h] = (k * jnp.exp(g_last - gcb)).T.astype(BF16)
            egl_scr[c, h] = jnp.broadcast_to(jnp.exp(g_last), (8, LANES))
            rhs.append(jnp.concatenate([v * bb, kb * eg], axis=1).astype(BF16))
            qs.append(q * eg)
        ps = []
        for h in heads:
            nb = ns[h].astype(BF16)
            ps.append(_dot(nb, nb))
        xs_ = [eye + ns[h] for h in heads]
        for _ in range(4):
            for h in heads:
                r = _dot(jnp.concatenate([xs_[h], ps[h]], axis=0).astype(BF16), ps[h].astype(BF16))
                xs_[h] = xs_[h] + r[:CHUNK]
                ps[h] = r[CHUNK:]
        for h in heads:
            xs_[h] = xs_[h] + _dot(xs_[h].astype(BF16), ps[h].astype(BF16))
        for h in heads:
            uw = _dot(xs_[h].astype(BF16), rhs[h])
            u_scr[h, pl.ds(r0, CHUNK), :] = uw[:, :HEAD_DIM]
            wq_scr[c, h] = jnp.concatenate([uw[:, HEAD_DIM:], qs[h]], axis=0).astype(BF16)
        return carry

    lax.fori_loop(0, nc, prep, 0)

    def step(c, carry):
        r0 = pl.multiple_of(c * CHUNK, CHUNK)
        heads = range(DN_HEADS)
        ss = [s_scr[h] for h in heads]
        rs = [_dot(wq_scr[c, h], ss[h].astype(BF16)) for h in heads]
        vbs = [(u_scr[h, pl.ds(r0, CHUNK), :] - rs[h][:CHUNK]).astype(BF16) for h in heads]
        for h in heads:
            s_scr[h] = ss[h] * egl_scr[c, h][0:1, :] + _dot(kdt_scr[c, h], vbs[h])
        for h in heads:
            cols = slice(h * HEAD_DIM, (h + 1) * HEAD_DIM)
            o = rs[h][CHUNK:] + _dot(qk_scr[c, h], vbs[h])
            on = o * lax.rsqrt(jnp.mean(o * o, axis=-1, keepdims=True) + EPS) * nw_ref[...]
            z = z_ref[pl.ds(r0, CHUNK), cols]
            o_ref[pl.ds(r0, CHUNK), cols] = (on * (z * _sigmoid(z))).astype(o_ref.dtype)
        return carry

    lax.fori_loop(0, nc, step, 0)

    for which, x_ref in enumerate((q_ref, k_ref, v_ref)):
        tail_scr[which] = x_ref[lb - look:lb, :]


def _deltanet(main, small, conv_w, alog_row, dtb_row, dn_norm_w, *, batch, seq, lb):
    t = batch * seq
    nlb = seq // lb
    nc = lb // CHUNK
    wblk = DN_WIDTH // LANES

    def col(cb):
        return pl.BlockSpec((lb, DN_WIDTH), lambda b, l, cb=cb: (b * nlb + l, cb // wblk))

    row = pl.BlockSpec((1, LANES), lambda b, l: (0, 0))
    return pl.pallas_call(
        functools.partial(_deltanet_kernel, lb=lb),
        grid=(batch, nlb),
        in_specs=[col(CB_DQ), col(CB_DK), col(CB_DV), col(CB_DZ),
                  pl.BlockSpec((lb, LANES), lambda b, l: (b * nlb + l, 0)),
                  pl.BlockSpec(conv_w.shape, lambda b, l: (0, 0)), row, row, row],
        out_specs=pl.BlockSpec((lb, DN_WIDTH), lambda b, l: (b * nlb + l, 0)),
        out_shape=jax.ShapeDtypeStruct((t, DN_WIDTH), BF16),
        scratch_shapes=[
            pltpu.VMEM((DN_HEADS, HEAD_DIM, HEAD_DIM), F32),
            pltpu.VMEM((3, 8, DN_WIDTH), F32),
            pltpu.VMEM((DN_HEADS, lb, HEAD_DIM), F32),
            pltpu.VMEM((nc, DN_HEADS, 2 * CHUNK, HEAD_DIM), BF16),
            pltpu.VMEM((nc, DN_HEADS, CHUNK, CHUNK), BF16),
            pltpu.VMEM((nc, DN_HEADS, HEAD_DIM, CHUNK), BF16),
            pltpu.VMEM((nc, DN_HEADS, 8, LANES), F32),
        ],
        compiler_params=_cparams(("parallel", "arbitrary")),
        name="deltanet",
    )(main, main, main, main, small, conv_w, alog_row, dtb_row, dn_norm_w)


def _attention_kernel(q_ref, k_ref, v_ref, tc_ref, ts_ref, o_ref,
                      qr_scr, kr_scr, og_scr, lse_scr, *, seq):
    group = pl.program_id(2)
    rows = 256
    lane = lax.broadcasted_iota(jnp.int32, (rows, LANES), 1)
    half = ROPE_DIM // 2

    def rope(i, carry):
        r0 = pl.multiple_of(i * rows, rows)
        tc = tc_ref[pl.ds(r0, rows), :]
        ts = ts_ref[pl.ds(r0, rows), :]
        for src, dst, scale in ((q_ref, qr_scr, HEAD_DIM ** -0.5), (k_ref, kr_scr, None)):
            x = src[pl.ds(r0, rows), :]
            partner = jnp.where(lane < half, pltpu.roll(x, LANES - half, axis=1),
                                pltpu.roll(x, half, axis=1))
            y = x * tc + partner * ts
            dst[pl.ds(r0, rows), :] = y if scale is None else y * scale
        return carry

    lax.fori_loop(0, seq // rows, rope, 0)

    ii = lax.broadcasted_iota(jnp.int32, (ATT_BLOCK, 2 * ATT_BLOCK), 0)
    jj = lax.broadcasted_iota(jnp.int32, (ATT_BLOCK, 2 * ATT_BLOCK), 1)
    dist = ii + ATT_BLOCK - jj
    band_mask = jnp.where(dist >= 0, jnp.where(dist <= ATT_BLOCK, 0.0, NEG_BIG), NEG_BIG)
    prev_half = jnp.where(jj < ATT_BLOCK, 1.0, 0.0)
    per_iter = 4

    def run_group(gi, dil):
        nb = seq // (dil * ATT_BLOCK)
        span = dil * ATT_BLOCK

        def rows_of(start):
            if dil == 1:
                return pl.ds(start, ATT_BLOCK)
            return pl.ds(start, ATT_BLOCK, stride=dil)

        def blocks(it, carry):
            where = []
            for u in range(per_iter):
                t = it * per_iter + u
                r = lax.shift_right_logical(t, nb.bit_length() - 1)
                n = jnp.bitwise_and(t, nb - 1)
                start = r + n * span
                where.append((n, start, jnp.maximum(start - span, r)))
            scores = []
            for n, start, prev in where:
                qb = qr_scr[rows_of(start), :].astype(BF16)
                kcat = jnp.concatenate([kr_scr[rows_of(prev), :], kr_scr[rows_of(start), :]], axis=0)
                s = _dot_nt(qb, kcat.astype(BF16))
                no_prev = jnp.where(n > 0, 0.0, NEG_BIG)
                s = s + (band_mask + prev_half * no_prev)
                scores.append(s)
            probs = []
            for s in scores:
                m = jnp.max(s, axis=1, keepdims=True)
                p = jnp.exp(s - m)
                probs.append((m, p, jnp.sum(p, axis=1, keepdims=True)))
            for (n, start, prev), (m, p, den) in zip(where, probs):
                vcat = jnp.concatenate([v_ref[rows_of(prev), :], v_ref[rows_of(start), :]], axis=0)
                acc = _dot(p.astype(BF16), vcat.astype(BF16))
                og_scr[gi, rows_of(start), :] = acc / den
                lse_scr[gi, rows_of(start), :] = jnp.broadcast_to(m + jnp.log(den),
                                                                  (ATT_BLOCK, LANES))
            return carry

        lax.fori_loop(0, seq // (ATT_BLOCK * per_iter), blocks, 0)

    for gi, (_, dil) in enumerate(ATT_GROUPS):
        @pl.when(group == gi)
        def _(gi=gi, dil=dil):
            run_group(gi, dil)

    @pl.when(group == len(ATT_GROUPS) - 1)
    def _():
        def merge(i, carry):
            r0 = pl.multiple_of(i * rows, rows)
            ls = [lse_scr[g, pl.ds(r0, rows), :] for g in range(len(ATT_GROUPS))]
            mx = functools.reduce(jnp.maximum, ls)
            es = [jnp.exp(l - mx) for l in ls]
            num = functools.reduce(
                lambda a, b: a + b, [e * og_scr[g, pl.ds(r0, rows), :] for g, e in enumerate(es)])
            den = functools.reduce(lambda a, b: a + b, es)
            o_ref[pl.ds(r0, rows), :] = (num / den).astype(o_ref.dtype)
            return carry

        lax.fori_loop(0, seq // rows, merge, 0)


def _attention(main, rope_cos, rope_sin, *, batch, seq):
    t = batch * seq
    ng = len(ATT_GROUPS)

    def col(cb):
        return pl.BlockSpec((seq, HEAD_DIM),
                            lambda b, h, g, cb=cb: (b, cb + g * ATT_HEADS_PER_GROUP + h))

    tab = pl.BlockSpec((seq, LANES), lambda b, h, g: (0, 0))
    return pl.pallas_call(
        functools.partial(_attention_kernel, seq=seq),
        grid=(batch, ATT_HEADS_PER_GROUP, ng),
        in_specs=[col(CB_AQ), col(CB_AK), col(CB_AV), tab, tab],
        out_specs=pl.BlockSpec((seq, HEAD_DIM), lambda b, h, g: (b, h)),
        out_shape=jax.ShapeDtypeStruct((t, ATT_OUT_WIDTH), BF16),
        scratch_shapes=[
            pltpu.VMEM((seq, HEAD_DIM), F32),
            pltpu.VMEM((seq, HEAD_DIM), F32),
            pltpu.VMEM((ng, seq, HEAD_DIM), F32),
            pltpu.VMEM((ng, seq, LANES), F32),
        ],
        compiler_params=_cparams(("parallel", "parallel", "arbitrary")),
        name="dilated_attention",
    )(main, main, main, rope_cos, rope_sin)


def _rope_tables(seq):
    half = ROPE_DIM // 2
    inv_freq = jnp.power(ROPE_THETA, -jnp.arange(half, dtype=F32) * (2.0 / ROPE_DIM))
    ang = jnp.arange(seq).astype(F32)[:, None] * inv_freq[None, :]
    cos, sin = jnp.cos(ang), jnp.sin(ang)
    rest = HEAD_DIM - ROPE_DIM
    tc = jnp.concatenate([cos, cos, jnp.ones((seq, rest), F32)], axis=1)
    ts = jnp.concatenate([-sin, sin, jnp.zeros((seq, rest), F32)], axis=1)
    return tc, ts


def _merge_kernel(x_ref, oa_ref, ob_ref, ga_ref, gb_ref, wa_ref, wb_ref, wo_ref, o_ref):
    ya = _dot(oa_ref[...], wa_ref[...])
    yb = _dot(ob_ref[...], wb_ref[...])
    merged = _sigmoid(ga_ref[...]) * ya + _sigmoid(gb_ref[...]) * yb
    o_ref[...] = x_ref[...] + _dot(merged.astype(BF16), wo_ref[...])


def _merge(x2, oa, ob, main, w_a, w_b, w_o, *, tm):
    t, d = x2.shape
    gate_cb = CB_GA * LANES // d
    const = lambda shape: pl.BlockSpec(shape, lambda i: (0, 0))
    return pl.pallas_call(
        _merge_kernel,
        grid=(t // tm,),
        in_specs=[
            pl.BlockSpec((tm, d), lambda i: (i, 0)),
            pl.BlockSpec((tm, DN_WIDTH), lambda i: (i, 0)),
            pl.BlockSpec((tm, ATT_OUT_WIDTH), lambda i: (i, 0)),
            pl.BlockSpec((tm, d), lambda i: (i, gate_cb)),
            pl.BlockSpec((tm, d), lambda i: (i, gate_cb + 1)),
            const(w_a.shape), const(w_b.shape), const(w_o.shape),
        ],
        out_specs=pl.BlockSpec((tm, d), lambda i: (i, 0)),
        out_shape=jax.ShapeDtypeStruct((t, d), F32),
        compiler_params=_cparams(("parallel",)),
        name="branch_merge",
    )(x2, oa, ob, main, main, w_a, w_b, w_o)


def _ffn_kernel(x_ref, nw_ref, wg_ref, wu_ref, wd_ref, fw_ref, o_ref, h_ref, acc_ref, *, final_norm):
    f = pl.program_id(1)

    @pl.when(f == 0)
    def _():
        x = x_ref[...]
        h = x * lax.rsqrt(jnp.mean(x * x, axis=-1, keepdims=True) + EPS) * nw_ref[...]
        h_ref[...] = h.astype(BF16)
        acc_ref[...] = x

    h = h_ref[...]
    gate = _dot(h, wg_ref[...])
    up = _dot(h, wu_ref[...])
    act = gate * _sigmoid(gate) * up
    acc_ref[...] += _dot(act.astype(BF16), wd_ref[...])

    @pl.when(f == pl.num_programs(1) - 1)
    def _():
        y = acc_ref[...]
        if final_norm:
            y = y * lax.rsqrt(jnp.mean(y * y, axis=-1, keepdims=True) + EPS) * fw_ref[...]
        o_ref[...] = y


def _ffn(x1, norm_w, w_gate_up, w_down, final_w, *, tm, tf, final_norm):
    t, d = x1.shape
    d_ff = w_down.shape[0]
    nf = d_ff // tf
    return pl.pallas_call(
        functools.partial(_ffn_kernel, final_norm=final_norm),
        grid=(t // tm, nf),
        in_specs=[
            pl.BlockSpec((tm, d), lambda i, f: (i, 0)),
            pl.BlockSpec((1, d), lambda i, f: (0, 0)),
            pl.BlockSpec((d, tf), lambda i, f: (0, f)),
            pl.BlockSpec((d, tf), lambda i, f: (0, nf + f)),
            pl.BlockSpec((tf, d), lambda i, f: (f, 0)),
            pl.BlockSpec((1, d), lambda i, f: (0, 0)),
        ],
        out_specs=pl.BlockSpec((tm, d), lambda i, f: (i, 0)),
        out_shape=jax.ShapeDtypeStruct((t, d), F32),
        scratch_shapes=[pltpu.VMEM((tm, d), BF16), pltpu.VMEM((tm, d), F32)],
        compiler_params=_cparams(("parallel", "arbitrary")),
        name="swiglu_ffn",
    )(x1, norm_w, w_gate_up, w_gate_up, w_down, final_w)


def _pad_lanes(v, offset):
    out = jnp.zeros((1, LANES), F32)
    return lax.dynamic_update_slice(out, v.reshape(1, -1).astype(F32), (0, offset))


def kernel(x, norm1_w, w_in, conv_w, a_log, dt_bias, dn_norm_w, w_proj_a, w_proj_b, w_out,
           norm2_w, w_gate_up, w_down, final_norm_w):
    batch, seq, d_model = x.shape
    depth = w_in.shape[0]
    t = batch * seq
    assert d_model == 8 * LANES and seq % (ATT_GROUPS[-1][1] * ATT_BLOCK) == 0
    d_ff = w_down.shape[1]

    o_dz_end = 4 * DN_WIDTH
    o_small_end = o_dz_end + 2 * DN_HEADS
    o_att_end = o_small_end + 3 * ATT_WIDTH

    rope_cos, rope_sin = _rope_tables(seq)
    x2 = x.reshape(t, d_model)
    for i in range(depth):
        wi = w_in[i]
        w_main = jnp.concatenate(
            [wi[:, :o_dz_end], wi[:, o_att_end:], wi[:, o_small_end:o_att_end]], axis=1).astype(BF16)
        w_small = jnp.pad(wi[:, o_dz_end:o_small_end],
                          ((0, 0), (0, LANES - 2 * DN_HEADS))).astype(BF16)
        main, small = _in_projection(x2, norm1_w[i].reshape(1, d_model), w_main, w_small,
                                     tm=1024, tn=MAIN_WIDTH // 6)
        oa = _deltanet(main, small, conv_w[i], _pad_lanes(a_log[i], DN_HEADS),
                       _pad_lanes(dt_bias[i], DN_HEADS), dn_norm_w[i].reshape(1, HEAD_DIM),
                       batch=batch, seq=seq, lb=512)
        ob = _attention(main, rope_cos, rope_sin, batch=batch, seq=seq)
        x2 = _merge(x2, oa, ob, main, w_proj_a[i].astype(BF16), w_proj_b[i].astype(BF16),
                    w_out[i].astype(BF16), tm=512)
        x2 = _ffn(x2, norm2_w[i].reshape(1, d_model), w_gate_up[i].astype(BF16),
                  w_down[i].astype(BF16), final_norm_w.reshape(1, d_model), tm=512, tf=d_ff // 2,
                  final_norm=(i == depth - 1))
    return x2.reshape(batch, seq, d_model)
```

```python
import functools
import math

import jax
import jax.numpy as jnp
from jax import lax
from jax.experimental import pallas as pl
from jax.experimental.pallas import tpu as pltpu

F32 = jnp.float32
BF16 = jnp.bfloat16

EPS = 1e-6
LANES = 128
HEAD_DIM = 128
DN_HEADS = 8
DN_WIDTH = DN_HEADS * HEAD_DIM
CONV_WIDTH = 4
CHUNK = 64
ATT_GROUPS = ((128, 1), (512, 4), (2048, 16))
ATT_HEADS_PER_GROUP = 4
ATT_N_HEADS = len(ATT_GROUPS) * ATT_HEADS_PER_GROUP
ATT_WIDTH = ATT_N_HEADS * HEAD_DIM
ATT_OUT_WIDTH = ATT_HEADS_PER_GROUP * HEAD_DIM
ATT_BLOCK = 128
ROPE_THETA = 500000.0
ROPE_DIM = HEAD_DIM // 4
NEG_BIG = -1e30

CB_DQ, CB_DK, CB_DV, CB_DZ = 0, 8, 16, 24
CB_GA, CB_GB = 32, 40
CB_AQ, CB_AK, CB_AV = 48, 60, 72
MAIN_WIDTH = 84 * LANES

VMEM_LIMIT = 56 * 1024 * 1024


def _cparams(sem):
    return pltpu.CompilerParams(dimension_semantics=sem, vmem_limit_bytes=VMEM_LIMIT)


def _dot(a, b):
    return jnp.dot(a, b, preferred_element_type=F32)


def _dot_nt(a, b):
    return lax.dot_general(a, b, (((1,), (1,)), ((), ())), preferred_element_type=F32)


def _dot_f32(a, b):
    return jnp.dot(a, b, preferred_element_type=F32, precision=lax.Precision.HIGHEST)


def _sigmoid(x):
    return 0.5 * jnp.tanh(0.5 * x) + 0.5


def _split_bf16(x):
    hi = x.astype(BF16)
    lo = (x - hi.astype(F32)).astype(BF16)
    return jnp.concatenate([hi, lo], axis=1)


def _inproj_kernel(x_ref, nw_ref, w_ref, ws_ref, o_ref, os_ref, h_ref):
    @pl.when(pl.program_id(1) == 0)
    def _():
        x = x_ref[...]
        h = x * lax.rsqrt(jnp.mean(x * x, axis=-1, keepdims=True) + EPS) * nw_ref[...]
        hb = h.astype(BF16)
        h_ref[...] = hb
        os_ref[...] = _dot(hb, ws_ref[...])

    o_ref[...] = _dot(h_ref[...], w_ref[...])


def _in_projection(x2, norm_w, w_main, w_small, *, tm, tn):
    t, d = x2.shape
    n = w_main.shape[1]
    return pl.pallas_call(
        _inproj_kernel,
        grid=(t // tm, n // tn),
        in_specs=[
            pl.BlockSpec((tm, d), lambda i, j: (i, 0)),
            pl.BlockSpec((1, d), lambda i, j: (0, 0)),
            pl.BlockSpec((d, tn), lambda i, j: (0, j)),
            pl.BlockSpec((d, LANES), lambda i, j: (0, 0)),
        ],
        out_specs=[
            pl.BlockSpec((tm, tn), lambda i, j: (i, j)),
            pl.BlockSpec((tm, LANES), lambda i, j: (i, 0)),
        ],
        out_shape=[
            jax.ShapeDtypeStruct((t, n), F32),
            jax.ShapeDtypeStruct((t, LANES), F32),
        ],
        scratch_shapes=[pltpu.VMEM((tm, d), BF16)],
        compiler_params=_cparams(("parallel", "arbitrary")),
        name="in_projection",
    )(x2, norm_w, w_main, w_small)


def _deltanet_kernel(q_ref, k_ref, v_ref, z_ref, sm_ref, cw_ref, alog_ref, dtb_ref, nw_ref, o_ref,
                     s_scr, tail_scr, u_scr, wq_scr, qk_scr, kdt_scr, egl_scr,
                     pre_scr, gate_scr, gct_scr, sq_scr, *, lb):
    lstep = pl.program_id(1)
    nc = lb // CHUNK
    width = DN_WIDTH
    look = 8

    @pl.when(lstep == 0)
    def _():
        s_scr[...] = jnp.zeros_like(s_scr)
        tail_scr[...] = jnp.zeros_like(tail_scr)

    ii = lax.broadcasted_iota(jnp.int32, (CHUNK, CHUNK), 0)
    jj = lax.broadcasted_iota(jnp.int32, (CHUNK, CHUNK), 1)
    eye = (ii == jj).astype(F32)
    rowi = lax.broadcasted_iota(jnp.int32, (CHUNK, LANES), 0)

    heads = range(DN_HEADS)
    head_cols = [slice(h * HEAD_DIM, (h + 1) * HEAD_DIM) for h in heads]
    srcs = (q_ref, k_ref, v_ref)

    def anchor(value):
        rows8 = value[:look, :]
        zero = jnp.where(rows8 != rows8, rows8, 0.0)
        if zero.shape[1] < LANES:
            zero = jnp.concatenate([zero] * (LANES // zero.shape[1]), axis=1)
        return zero

    def conv_piece(which, h, c, r0, zero):
        x_ref, cols = srcs[which], head_cols[h]
        prev0 = pl.multiple_of(jnp.maximum(r0 - look, 0), look)
        top = lax.select(c == 0, tail_scr[which, :, cols], x_ref[pl.ds(prev0, look), cols])
        win = jnp.concatenate([top, x_ref[pl.ds(r0, CHUNK), cols]], axis=0)
        w0 = which * width + h * HEAD_DIM
        acc = None
        for j in range(CONV_WIDTH):
            back = CONV_WIDTH - 1 - j
            rows = win if back == 0 else pltpu.roll(win, back, axis=0)
            term = rows[look:, :] * (cw_ref[j:j + 1, w0:w0 + HEAD_DIM] + zero[0:1, :])
            acc = term if acc is None else acc + term
        y = acc * _sigmoid(acc)
        pre_scr[which, :, cols] = y
        if which < 2:
            sq_scr[(which * DN_HEADS + h) * CHUNK:(which * DN_HEADS + h + 1) * CHUNK, :] = y * y

    def gate_piece(r0, zero):
        sm = sm_ref[pl.ds(r0, CHUNK), :]
        xs = sm + (dtb_ref[...] + zero[0:1, :])
        softplus = jnp.maximum(xs, 0.0) + jnp.log(1.0 + jnp.exp(-jnp.abs(xs)))
        gc_all = -jnp.exp(alog_ref[...]) * softplus
        shift = 1
        while shift < CHUNK:
            gc_all = gc_all + jnp.where(rowi >= shift, pltpu.roll(gc_all, shift, axis=0), 0.0)
            shift *= 2
        gate_scr[0] = _sigmoid(sm)
        gate_scr[1] = gc_all
        gct_scr[...] = gc_all.T

    def front_pieces(c):
        r0 = pl.multiple_of(c * CHUNK, CHUNK)
        pieces = [functools.partial(conv_piece, which, h, c, r0)
                  for which in range(3) for h in heads]
        return pieces + [functools.partial(gate_piece, r0)]

    def front_finish():
        inv_norm = lax.rsqrt(
            _dot(_split_bf16(sq_scr[...]), jnp.ones((2 * HEAD_DIM, HEAD_DIM), BF16)) + EPS)
        for h in heads:
            cols = head_cols[h]
            pre_scr[0, :, cols] = pre_scr[0, :, cols] * (
                inv_norm[h * CHUNK:(h + 1) * CHUNK] * (HEAD_DIM ** -0.5))
            pre_scr[1, :, cols] = pre_scr[1, :, cols] * inv_norm[
                (DN_HEADS + h) * CHUNK:(DN_HEADS + h + 1) * CHUNK]

    for piece in front_pieces(0):
        piece(jnp.zeros((look, LANES), F32))
    front_finish()

    def prep(c, carry):
        r0 = pl.multiple_of(c * CHUNK, CHUNK)
        beta_all = gate_scr[0]
        gc_all = gate_scr[1]
        gc_t = gct_scr[...]
        ns, held = [], []
        qs, rhs = [None] * DN_HEADS, [None] * DN_HEADS
        for h in heads:
            cols = head_cols[h]
            q = pre_scr[0, :, cols]
            k = pre_scr[1, :, cols]
            v = pre_scr[2, :, cols]
            bb = jnp.broadcast_to(beta_all[:, h:h + 1], (CHUNK, LANES))
            gcb = jnp.broadcast_to(gc_all[:, DN_HEADS + h:DN_HEADS + h + 1], (CHUNK, LANES))
            g_row = gc_t[DN_HEADS + h:DN_HEADS + h + 1, :]
            decay = jnp.where(ii >= jj, jnp.exp(jnp.minimum(gcb[:, :CHUNK] - g_row, 0.0)), 0.0)
            kb = k * bb
            both = _dot_nt(jnp.concatenate([kb, q], axis=0).astype(BF16), k.astype(BF16))
            ns.append(jnp.where(ii > jj, -both[:CHUNK] * decay, 0.0))
            qk_scr[c, h] = (both[CHUNK:] * decay).astype(BF16)
            held.append((q, k, v, bb, gcb, kb))
            kb_last = kb

        def late_piece(h, zero):
            q, k, v, bb, gcb, kb = held[h]
            gcz = gcb + zero[0:1, :]
            g_last = gcz[CHUNK - 1:CHUNK, :]
            eg = jnp.exp(gcz)
            kdt_scr[c, h] = (k * jnp.exp(g_last - gcz)).T.astype(BF16)
            egl_scr[c, h] = jnp.broadcast_to(jnp.exp(g_last), (8, LANES))
            rhs[h] = jnp.concatenate([v * bb, kb * eg], axis=1).astype(BF16)
            qs[h] = q * eg

        pieces = ([functools.partial(late_piece, h) for h in heads]
                  + front_pieces(jnp.minimum(c + 1, nc - 1)))
        n_slots = 7

        def fill(slot, operand):
            zero = anchor(operand)
            for piece in pieces[slot::n_slots]:
                piece(zero)

        fill(0, kb_last)
        ps = []
        for h in heads:
            nb = ns[h].astype(BF16)
            ps.append(_dot(nb, nb))
        fill(1, ns[-1])
        xs_ = [eye + ns[h] for h in heads]
        for level in range(4):
            operand = xs_[-1]
            for h in heads:
                r = _dot(jnp.concatenate([xs_[h], ps[h]], axis=0).astype(BF16), ps[h].astype(BF16))
                xs_[h] = xs_[h] + r[:CHUNK]
                ps[h] = r[CHUNK:]
            fill(2 + level, operand)
        operand = xs_[-1]
        for h in heads:
            xs_[h] = xs_[h] + _dot(xs_[h].astype(BF16), ps[h].astype(BF16))
        fill(6, operand)
        for h in heads:
            uw = _dot(xs_[h].astype(BF16), rhs[h])
            u_scr[h, pl.ds(r0, CHUNK), :] = uw[:, :HEAD_DIM]
            wq_scr[c, h] = jnp.concatenate([uw[:, HEAD_DIM:], qs[h]], axis=0).astype(BF16)
        front_finish()
        return carry

    lax.fori_loop(0, nc, prep, 0)

    def step(c, carry):
        r0 = pl.multiple_of(c * CHUNK, CHUNK)
        heads = range(DN_HEADS)
        ss = [s_scr[h] for h in heads]
        rs = [_dot(wq_scr[c, h], ss[h].astype(BF16)) for h in heads]
        vbs = [(u_scr[h, pl.ds(r0, CHUNK), :] - rs[h][:CHUNK]).astype(BF16) for h in heads]
        for h in heads:
            s_scr[h] = ss[h] * egl_scr[c, h][0:1, :] + _dot(kdt_scr[c, h], vbs[h])
        for h in heads:
            cols = slice(h * HEAD_DIM, (h + 1) * HEAD_DIM)
            o = rs[h][CHUNK:] + _dot(qk_scr[c, h], vbs[h])
            on = o * lax.rsqrt(jnp.mean(o * o, axis=-1, keepdims=True) + EPS) * nw_ref[...]
            z = z_ref[pl.ds(r0, CHUNK), cols]
            o_ref[pl.ds(r0, CHUNK), cols] = (on * (z * _sigmoid(z))).astype(o_ref.dtype)
        return carry

    lax.fori_loop(0, nc, step, 0)

    for which, x_ref in enumerate((q_ref, k_ref, v_ref)):
        tail_scr[which] = x_ref[lb - look:lb, :]


def _deltanet(main, small, conv_w, alog_row, dtb_row, dn_norm_w, *, batch, seq, lb):
    t = batch * seq
    nlb = seq // lb
    nc = lb // CHUNK
    wblk = DN_WIDTH // LANES

    def col(cb):
        return pl.BlockSpec((lb, DN_WIDTH), lambda b, l, cb=cb: (b * nlb + l, cb // wblk))

    row = pl.BlockSpec((1, LANES), lambda b, l: (0, 0))
    return pl.pallas_call(
        functools.partial(_deltanet_kernel, lb=lb),
        grid=(batch, nlb),
        in_specs=[col(CB_DQ), col(CB_DK), col(CB_DV), col(CB_DZ),
                  pl.BlockSpec((lb, LANES), lambda b, l: (b * nlb + l, 0)),
                  pl.BlockSpec(conv_w.shape, lambda b, l: (0, 0)), row, row, row],
        out_specs=pl.BlockSpec((lb, DN_WIDTH), lambda b, l: (b * nlb + l, 0)),
        out_shape=jax.ShapeDtypeStruct((t, DN_WIDTH), BF16),
        scratch_shapes=[
            pltpu.VMEM((DN_HEADS, HEAD_DIM, HEAD_DIM), F32),
            pltpu.VMEM((3, 8, DN_WIDTH), F32),
            pltpu.VMEM((DN_HEADS, lb, HEAD_DIM), F32),
            pltpu.VMEM((nc, DN_HEADS, 2 * CHUNK, HEAD_DIM), BF16),
            pltpu.VMEM((nc, DN_HEADS, CHUNK, CHUNK), BF16),
            pltpu.VMEM((nc, DN_HEADS, HEAD_DIM, CHUNK), BF16),
            pltpu.VMEM((nc, DN_HEADS, 8, LANES), F32),
            pltpu.VMEM((3, CHUNK, DN_WIDTH), F32),
            pltpu.VMEM((2, CHUNK, LANES), F32),
            pltpu.VMEM((LANES, CHUNK), F32),
            pltpu.VMEM((2 * DN_HEADS * CHUNK, HEAD_DIM), F32),
        ],
        compiler_params=_cparams(("parallel", "arbitrary")),
        name="deltanet",
    )(main, main, main, main, small, conv_w, alog_row, dtb_row, dn_norm_w)


def _attention_kernel(q_ref, k_ref, v_ref, tc_ref, ts_ref, o_ref,
                      qr_scr, kr_scr, og_scr, lse_scr, *, seq):
    group = pl.program_id(2)
    rows = 256
    half = ROPE_DIM // 2
    lane = lax.broadcasted_iota(jnp.int32, (rows, LANES), 1)
    src_lane = jnp.bitwise_and(lax.broadcasted_iota(jnp.int32, (2 * LANES, LANES), 0), LANES - 1)
    dst_lane = lax.broadcasted_iota(jnp.int32, (2 * LANES, LANES), 1)
    want = jnp.where(dst_lane < half, dst_lane + half, jnp.where(dst_lane < ROPE_DIM, dst_lane - half, -1))
    swap = jnp.where(src_lane == want, 1.0, 0.0).astype(BF16)

    def rope(i, carry):
        r0 = pl.multiple_of(i * rows, rows)
        tc = tc_ref[pl.ds(r0, rows), :]
        ts = ts_ref[pl.ds(r0, rows), :]
        xq = q_ref[pl.ds(r0, rows), :]
        partner_q = jnp.where(lane < half, pltpu.roll(xq, LANES - half, axis=1),
                              pltpu.roll(xq, half, axis=1))
        qr_scr[pl.ds(r0, rows), :] = (xq * tc + partner_q * ts) * (HEAD_DIM ** -0.5)
        xk = k_ref[pl.ds(r0, rows), :]
        kr_scr[pl.ds(r0, rows), :] = xk * tc + _dot(_split_bf16(xk), swap) * ts
        return carry

    lax.fori_loop(0, seq // rows, rope, 0)

    ii = lax.broadcasted_iota(jnp.int32, (ATT_BLOCK, 2 * ATT_BLOCK), 0)
    jj = lax.broadcasted_iota(jnp.int32, (ATT_BLOCK, 2 * ATT_BLOCK), 1)
    dist = ii + ATT_BLOCK - jj
    band_mask = jnp.where(dist >= 0, jnp.where(dist <= ATT_BLOCK, 0.0, NEG_BIG), NEG_BIG)
    prev_half = jnp.where(jj < ATT_BLOCK, 1.0, 0.0)
    per_iter = 4

    def run_group(gi, dil):
        nb = seq // (dil * ATT_BLOCK)
        span = dil * ATT_BLOCK

        def rows_of(start):
            if dil == 1:
                return pl.ds(start, ATT_BLOCK)
            return pl.ds(start, ATT_BLOCK, stride=dil)

        def blocks(it, carry):
            where = []
            for u in range(per_iter):
                t = it * per_iter + u
                r = lax.shift_right_logical(t, nb.bit_length() - 1)
                n = jnp.bitwise_and(t, nb - 1)
                start = r + n * span
                where.append((n, start, jnp.maximum(start - span, r)))
            run = min(nb, per_iter)
            kcur = [kr_scr[rows_of(start), :].astype(BF16) for _, start, _ in where]
            vcur = [v_ref[rows_of(start), :].astype(BF16) for _, start, _ in where]

            def prev_of(u, cur, ref):
                if u % run:
                    return cur[u - 1]
                if nb <= per_iter:
                    return cur[u]
                return ref[rows_of(where[u][2]), :].astype(BF16)

            scores = []
            for u, (n, start, prev) in enumerate(where):
                qb = qr_scr[rows_of(start), :].astype(BF16)
                kcat = jnp.concatenate([prev_of(u, kcur, kr_scr), kcur[u]], axis=0)
                s = _dot_nt(qb, kcat)
                no_prev = jnp.where(n > 0, 0.0, NEG_BIG)
                s = s + (band_mask + prev_half * no_prev)
                scores.append(s)
            probs = []
            for s in scores:
                m = jnp.max(s, axis=1, keepdims=True)
                p = jnp.exp(s - m)
                probs.append((m, p, jnp.sum(p, axis=1, keepdims=True)))
            for u, ((n, start, prev), (m, p, den)) in enumerate(zip(where, probs)):
                vcat = jnp.concatenate([prev_of(u, vcur, v_ref), vcur[u]], axis=0)
                acc = _dot(p.astype(BF16), vcat)
                og_scr[gi, rows_of(start), :] = acc / den
                lse_scr[gi, rows_of(start), :] = jnp.broadcast_to(m + jnp.log(den),
                                                                  (ATT_BLOCK, LANES))
            return carry

        lax.fori_loop(0, seq // (ATT_BLOCK * per_iter), blocks, 0)

    for gi, (_, dil) in enumerate(ATT_GROUPS):
        @pl.when(group == gi)
        def _(gi=gi, dil=dil):
            run_group(gi, dil)

    @pl.when(group == len(ATT_GROUPS) - 1)
    def _():
        def merge(i, carry):
            r0 = pl.multiple_of(i * rows, rows)
            ls = [lse_scr[g, pl.ds(r0, rows), :] for g in range(len(ATT_GROUPS))]
            mx = functools.reduce(jnp.maximum, ls)
            es = [jnp.exp(l - mx) for l in ls]
            num = functools.reduce(
                lambda a, b: a + b, [e * og_scr[g, pl.ds(r0, rows), :] for g, e in enumerate(es)])
            den = functools.reduce(lambda a, b: a + b, es)
            o_ref[pl.ds(r0, rows), :] = (num / den).astype(o_ref.dtype)
            return carry

        lax.fori_loop(0, seq // rows, merge, 0)


def _attention(main, rope_cos, rope_sin, *, batch, seq):
    t = batch * seq
    ng = len(ATT_GROUPS)

    def col(cb):
        return pl.BlockSpec((seq, HEAD_DIM),
                            lambda b, h, g, cb=cb: (b, cb + g * ATT_HEADS_PER_GROUP + h))

    tab = pl.BlockSpec((seq, LANES), lambda b, h, g: (0, 0))
    return pl.pallas_call(
        functools.partial(_attention_kernel, seq=seq),
        grid=(batch, ATT_HEADS_PER_GROUP, ng),
        in_specs=[col(CB_AQ), col(CB_AK), col(CB_AV), tab, tab],
        out_specs=pl.BlockSpec((seq, HEAD_DIM), lambda b, h, g: (b, h)),
        out_shape=jax.ShapeDtypeStruct((t, ATT_OUT_WIDTH), BF16),
        scratch_shapes=[
            pltpu.VMEM((seq, HEAD_DIM), F32),
            pltpu.VMEM((seq, HEAD_DIM), F32),
            pltpu.VMEM((ng, seq, HEAD_DIM), F32),
            pltpu.VMEM((ng, seq, LANES), F32),
        ],
        compiler_params=_cparams(("parallel", "parallel", "arbitrary")),
        name="dilated_attention",
    )(main, main, main, rope_cos, rope_sin)


def _rope_tables(seq):
    half = ROPE_DIM // 2
    inv_freq = jnp.power(ROPE_THETA, -jnp.arange(half, dtype=F32) * (2.0 / ROPE_DIM))
    ang = jnp.arange(seq).astype(F32)[:, None] * inv_freq[None, :]
    cos, sin = jnp.cos(ang), jnp.sin(ang)
    rest = HEAD_DIM - ROPE_DIM
    tc = jnp.concatenate([cos, cos, jnp.ones((seq, rest), F32)], axis=1)
    ts = jnp.concatenate([-sin, sin, jnp.zeros((seq, rest), F32)], axis=1)
    return tc, ts


def _merge_kernel(x_ref, oa_ref, ob_ref, ga_ref, gb_ref, wa_ref, wb_ref, wo_ref, o_ref):
    ya = _dot(oa_ref[...], wa_ref[...])
    yb = _dot(ob_ref[...], wb_ref[...])
    merged = _sigmoid(ga_ref[...]) * ya + _sigmoid(gb_ref[...]) * yb
    o_ref[...] = x_ref[...] + _dot(merged.astype(BF16), wo_ref[...])


def _merge(x2, oa, ob, main, w_a, w_b, w_o, *, tm):
    t, d = x2.shape
    gate_cb = CB_GA * LANES // d
    const = lambda shape: pl.BlockSpec(shape, lambda i: (0, 0))
    return pl.pallas_call(
        _merge_kernel,
        grid=(t // tm,),
        in_specs=[
            pl.BlockSpec((tm, d), lambda i: (i, 0)),
            pl.BlockSpec((tm, DN_WIDTH), lambda i: (i, 0)),
            pl.BlockSpec((tm, ATT_OUT_WIDTH), lambda i: (i, 0)),
            pl.BlockSpec((tm, d), lambda i: (i, gate_cb)),
            pl.BlockSpec((tm, d), lambda i: (i, gate_cb + 1)),
            const(w_a.shape), const(w_b.shape), const(w_o.shape),
        ],
        out_specs=pl.BlockSpec((tm, d), lambda i: (i, 0)),
        out_shape=jax.ShapeDtypeStruct((t, d), F32),
        compiler_params=_cparams(("parallel",)),
        name="branch_merge",
    )(x2, oa, ob, main, main, w_a, w_b, w_o)


def _ffn_kernel(x_ref, nw_ref, wg_ref, wu_ref, wd_ref, fw_ref, o_ref, h_ref, acc_ref, *, final_norm):
    f = pl.program_id(1)

    @pl.when(f == 0)
    def _():
        x = x_ref[...]
        h = x * lax.rsqrt(jnp.mean(x * x, axis=-1, keepdims=True) + EPS) * nw_ref[...]
        h_ref[...] = h.astype(BF16)
        acc_ref[...] = x

    h = h_ref[...]
    gate = _dot(h, wg_ref[...])
    up = _dot(h, wu_ref[...])
    act = gate * _sigmoid(gate) * up
    acc_ref[...] += _dot(act.astype(BF16), wd_ref[...])

    @pl.when(f == pl.num_programs(1) - 1)
    def _():
        y = acc_ref[...]
        if final_norm:
            y = y * lax.rsqrt(jnp.mean(y * y, axis=-1, keepdims=True) + EPS) * fw_ref[...]
        o_ref[...] = y


def _ffn(x1, norm_w, w_gate_up, w_down, final_w, *, tm, tf, final_norm):
    t, d = x1.shape
    d_ff = w_down.shape[0]
    nf = d_ff // tf
    return pl.pallas_call(
        functools.partial(_ffn_kernel, final_norm=final_norm),
        grid=(t // tm, nf),
        in_specs=[
            pl.BlockSpec((tm, d), lambda i, f: (i, 0)),
            pl.BlockSpec((1, d), lambda i, f: (0, 0)),
            pl.BlockSpec((d, tf), lambda i, f: (0, f)),
            pl.BlockSpec((d, tf), lambda i, f: (0, nf + f)),
            pl.BlockSpec((tf, d), lambda i, f: (f, 0)),
            pl.BlockSpec((1, d), lambda i, f: (0, 0)),
        ],
        out_specs=pl.BlockSpec((tm, d), lambda i, f: (i, 0)),
        out_shape=jax.ShapeDtypeStruct((t, d), F32),
        scratch_shapes=[pltpu.VMEM((tm, d), BF16), pltpu.VMEM((tm, d), F32)],
        compiler_params=_cparams(("parallel", "arbitrary")),
        name="swiglu_ffn",
    )(x1, norm_w, w_gate_up, w_gate_up, w_down, final_w)


def _pad_lanes(v, offset):
    out = jnp.zeros((1, LANES), F32)
    return lax.dynamic_update_slice(out, v.reshape(1, -1).astype(F32), (0, offset))


def kernel(x, norm1_w, w_in, conv_w, a_log, dt_bias, dn_norm_w, w_proj_a, w_proj_b, w_out,
           norm2_w, w_gate_up, w_down, final_norm_w):
    batch, seq, d_model = x.shape
    depth = w_in.shape[0]
    t = batch * seq
    assert d_model == 8 * LANES and seq % (ATT_GROUPS[-1][1] * ATT_BLOCK) == 0
    d_ff = w_down.shape[1]

    o_dz_end = 4 * DN_WIDTH
    o_small_end = o_dz_end + 2 * DN_HEADS
    o_att_end = o_small_end + 3 * ATT_WIDTH

    rope_cos, rope_sin = _rope_tables(seq)
    x2 = x.reshape(t, d_model)
    for i in range(depth):
        wi = w_in[i]
        w_main = jnp.concatenate(
            [wi[:, :o_dz_end], wi[:, o_att_end:], wi[:, o_small_end:o_att_end]], axis=1).astype(BF16)
        w_small = jnp.pad(wi[:, o_dz_end:o_small_end],
                          ((0, 0), (0, LANES - 2 * DN_HEADS))).astype(BF16)
        main, small = _in_projection(x2, norm1_w[i].reshape(1, d_model), w_main, w_small,
                                     tm=1024, tn=MAIN_WIDTH // 6)
        oa = _deltanet(main, small, conv_w[i], _pad_lanes(a_log[i], DN_HEADS),
                       _pad_lanes(dt_bias[i], DN_HEADS), dn_norm_w[i].reshape(1, HEAD_DIM),
                       batch=batch, seq=seq, lb=512)
        ob = _attention(main, rope_cos, rope_sin, batch=batch, seq=seq)
        x2 = _merge(x2, oa, ob, main, w_proj_a[i].astype(BF16), w_proj_b[i].astype(BF16),
                    w_out[i].astype(BF16), tm=512)
        x2 = _ffn(x2, norm2_w[i].reshape(1, d_model), w_gate_up[i].astype(BF16),
                  w_down[i].astype(BF16), final_norm_w.reshape(1, d_model), tm=512, tf=d_ff // 2,
                  final_norm=(i == depth - 1))
    return x2.reshape(batch, seq, d_model)
```

```python
import functools
import math

import jax
import jax.numpy as jnp
from jax import lax
from jax.experimental import pallas as pl
from jax.experimental.pallas import tpu as pltpu

F32 = jnp.float32
BF16 = jnp.bfloat16

EPS = 1e-6
LANES = 128
MXU_WIDTH = 256
HEAD_DIM = 128
DN_HEADS = 8
DN_WIDTH = DN_HEADS * HEAD_DIM
CONV_WIDTH = 4
CHUNK = 64
ATT_GROUPS = ((128, 1), (512, 4), (2048, 16))
ATT_HEADS_PER_GROUP = 4
ATT_N_HEADS = len(ATT_GROUPS) * ATT_HEADS_PER_GROUP
ATT_WIDTH = ATT_N_HEADS * HEAD_DIM
ATT_OUT_WIDTH = ATT_HEADS_PER_GROUP * HEAD_DIM
ATT_BLOCK = 128
ROPE_THETA = 500000.0
ROPE_DIM = HEAD_DIM // 4
NEG_BIG = -1e30

CB_DQ, CB_DK, CB_DV, CB_DZ = 0, 8, 16, 24
CB_GA, CB_GB = 32, 40
CB_AQ, CB_AK, CB_AV = 48, 60, 72
MAIN_WIDTH = 84 * LANES

VMEM_LIMIT = 56 * 1024 * 1024


def _cparams(sem):
    return pltpu.CompilerParams(dimension_semantics=sem, vmem_limit_bytes=VMEM_LIMIT)


def _dot(a, b):
    return jnp.dot(a, b, preferred_element_type=F32)


def _dot_nt(a, b):
    return lax.dot_general(a, b, (((1,), (1,)), ((), ())), preferred_element_type=F32)


def _dot_f32(a, b):
    return jnp.dot(a, b, preferred_element_type=F32, precision=lax.Precision.HIGHEST)


def _sigmoid(x):
    return 0.5 * jnp.tanh(0.5 * x) + 0.5


def _split_bf16(x):
    hi = x.astype(BF16)
    lo = (x - hi.astype(F32)).astype(BF16)
    return jnp.concatenate([hi, lo], axis=1)


def _inproj_kernel(x_ref, nw_ref, w_ref, ws_ref, o_ref, os_ref, h_ref):
    @pl.when(pl.program_id(1) == 0)
    def _():
        x = x_ref[...]
        h = x * lax.rsqrt(jnp.mean(x * x, axis=-1, keepdims=True) + EPS) * nw_ref[...]
        hb = h.astype(BF16)
        h_ref[...] = hb
        os_ref[...] = _dot(hb, ws_ref[...])

    o_ref[...] = _dot(h_ref[...], w_ref[...])


def _in_projection(x2, norm_w, w_main, w_small, *, tm, tn):
    t, d = x2.shape
    n = w_main.shape[1]
    return pl.pallas_call(
        _inproj_kernel,
        grid=(t // tm, n // tn),
        in_specs=[
            pl.BlockSpec((tm, d), lambda i, j: (i, 0)),
            pl.BlockSpec((1, d), lambda i, j: (0, 0)),
            pl.BlockSpec((d, tn), lambda i, j: (0, j)),
            pl.BlockSpec((d, LANES), lambda i, j: (0, 0)),
        ],
        out_specs=[
            pl.BlockSpec((tm, tn), lambda i, j: (i, j)),
            pl.BlockSpec((tm, LANES), lambda i, j: (i, 0)),
        ],
        out_shape=[
            jax.ShapeDtypeStruct((t, n), F32),
            jax.ShapeDtypeStruct((t, LANES), F32),
        ],
        scratch_shapes=[pltpu.VMEM((tm, d), BF16)],
        compiler_params=_cparams(("parallel", "arbitrary")),
        name="in_projection",
    )(x2, norm_w, w_main, w_small)


def _deltanet_kernel(q_ref, k_ref, v_ref, z_ref, sm_ref, cw_ref, alog_ref, dtb_ref, nw_ref, o_ref,
                     s_scr, tail_scr, u_scr, wq_scr, qk_scr, kdt_scr, egl_scr,
                     pre_scr, gate_scr, gct_scr, sq_scr, *, lb):
    lstep = pl.program_id(1)
    nc = lb // CHUNK
    width = DN_WIDTH
    look = 8

    @pl.when(lstep == 0)
    def _():
        s_scr[...] = jnp.zeros_like(s_scr)
        tail_scr[...] = jnp.zeros_like(tail_scr)
        u_scr[:, 0:CHUNK, :] = jnp.zeros((DN_HEADS, CHUNK, HEAD_DIM), F32)
        wq_scr[0] = jnp.zeros(wq_scr.shape[1:], BF16)
        qk_scr[0] = jnp.zeros(qk_scr.shape[1:], BF16)
        kdt_scr[0] = jnp.zeros(kdt_scr.shape[1:], BF16)
        egl_scr[0] = jnp.zeros(egl_scr.shape[1:], F32)

    ii = lax.broadcasted_iota(jnp.int32, (CHUNK, CHUNK), 0)
    jj = lax.broadcasted_iota(jnp.int32, (CHUNK, CHUNK), 1)
    eye = (ii == jj).astype(F32)
    rowi = lax.broadcasted_iota(jnp.int32, (CHUNK, LANES), 0)

    heads = range(DN_HEADS)
    head_cols = [slice(h * HEAD_DIM, (h + 1) * HEAD_DIM) for h in heads]
    srcs = (q_ref, k_ref, v_ref)

    def anchor(value):
        rows8 = value[:look, :]
        zero = jnp.where(rows8 != rows8, rows8, 0.0)
        if zero.shape[1] < LANES:
            zero = jnp.concatenate([zero] * (LANES // zero.shape[1]), axis=1)
        return zero

    def conv_piece(which, h, c, r0, zero):
        x_ref, cols = srcs[which], head_cols[h]
        prev0 = pl.multiple_of(jnp.maximum(r0 - look, 0), look)
        top = lax.select(c == 0, tail_scr[which, :, cols], x_ref[pl.ds(prev0, look), cols])
        win = jnp.concatenate([top, x_ref[pl.ds(r0, CHUNK), cols]], axis=0)
        w0 = which * width + h * HEAD_DIM
        acc = None
        for j in range(CONV_WIDTH):
            back = CONV_WIDTH - 1 - j
            rows = win if back == 0 else pltpu.roll(win, back, axis=0)
            term = rows[look:, :] * (cw_ref[j:j + 1, w0:w0 + HEAD_DIM] + zero[0:1, :])
            acc = term if acc is None else acc + term
        y = acc * _sigmoid(acc)
        pre_scr[which, :, cols] = y
        if which < 2:
            sq_scr[(which * DN_HEADS + h) * CHUNK:(which * DN_HEADS + h + 1) * CHUNK, :] = y * y

    def gate_piece(r0, zero):
        sm = sm_ref[pl.ds(r0, CHUNK), :]
        xs = sm + (dtb_ref[...] + zero[0:1, :])
        softplus = jnp.maximum(xs, 0.0) + jnp.log(1.0 + jnp.exp(-jnp.abs(xs)))
        gc_all = -jnp.exp(alog_ref[...]) * softplus
        shift = 1
        while shift < CHUNK:
            gc_all = gc_all + jnp.where(rowi >= shift, pltpu.roll(gc_all, shift, axis=0), 0.0)
            shift *= 2
        gate_scr[0] = _sigmoid(sm)
        gate_scr[1] = gc_all
        gct_scr[...] = gc_all.T

    def front_pieces(c):
        r0 = pl.multiple_of(c * CHUNK, CHUNK)
        pieces = [functools.partial(conv_piece, which, h, c, r0)
                  for which in range(3) for h in heads]
        return pieces + [functools.partial(gate_piece, r0)]

    def front_finish():
        inv_norm = lax.rsqrt(
            _dot(_split_bf16(sq_scr[...]), jnp.ones((2 * HEAD_DIM, HEAD_DIM), BF16)) + EPS)
        for h in heads:
            cols = head_cols[h]
            pre_scr[0, :, cols] = pre_scr[0, :, cols] * (
                inv_norm[h * CHUNK:(h + 1) * CHUNK] * (HEAD_DIM ** -0.5))
            pre_scr[1, :, cols] = pre_scr[1, :, cols] * inv_norm[
                (DN_HEADS + h) * CHUNK:(DN_HEADS + h + 1) * CHUNK]

    for piece in front_pieces(0):
        piece(jnp.zeros((look, LANES), F32))
    front_finish()

    def recur_read(cs):
        ss = [s_scr[h] for h in heads]
        rs = [_dot(wq_scr[cs, h], ss[h].astype(BF16)) for h in heads]
        return ss, rs

    def recur_update(cs, ss, rs, live):
        rc0 = pl.multiple_of(cs * CHUNK, CHUNK)
        vbs = []
        for h in heads:
            v_new = u_scr[h, pl.ds(rc0, CHUNK), :] - rs[h][:CHUNK]
            egl = egl_scr[cs, h][0:1, :]
            if live is not None:
                v_new = lax.select(live, v_new, jnp.zeros_like(v_new))
                egl = lax.select(live, egl, jnp.ones_like(egl))
            vbs.append(v_new.astype(BF16))
            s_scr[h] = ss[h] * egl + _dot(kdt_scr[cs, h], vbs[h])
        return [rs[h][CHUNK:] + _dot(qk_scr[cs, h], vbs[h]) for h in heads]

    def recur_finish(cs, outs):
        rc0 = pl.multiple_of(cs * CHUNK, CHUNK)
        for h in heads:
            o = outs[h]
            on = o * lax.rsqrt(jnp.mean(o * o, axis=-1, keepdims=True) + EPS) * nw_ref[...]
            z = z_ref[pl.ds(rc0, CHUNK), head_cols[h]]
            o_ref[pl.ds(rc0, CHUNK), head_cols[h]] = (on * (z * _sigmoid(z))).astype(o_ref.dtype)

    def prep(c, carry):
        r0 = pl.multiple_of(c * CHUNK, CHUNK)
        behind = jnp.maximum(c - 1, 0)
        live = c > 0
        beta_all = gate_scr[0]
        gc_all = gate_scr[1]
        gc_t = gct_scr[...]
        ns, held = [], []
        qs, rhs = [None] * DN_HEADS, [None] * DN_HEADS
        for h in heads:
            cols = head_cols[h]
            q = pre_scr[0, :, cols]
            k = pre_scr[1, :, cols]
            v = pre_scr[2, :, cols]
            bb = jnp.broadcast_to(beta_all[:, h:h + 1], (CHUNK, LANES))
            gcb = jnp.broadcast_to(gc_all[:, DN_HEADS + h:DN_HEADS + h + 1], (CHUNK, LANES))
            g_row = gc_t[DN_HEADS + h:DN_HEADS + h + 1, :]
            decay = jnp.where(ii >= jj, jnp.exp(jnp.minimum(gcb[:, :CHUNK] - g_row, 0.0)), 0.0)
            kb = k * bb
            both = _dot_nt(jnp.concatenate([kb, q], axis=0).astype(BF16), k.astype(BF16))
            ns.append(jnp.where(ii > jj, -both[:CHUNK] * decay, 0.0))
            qk_scr[c, h] = (both[CHUNK:] * decay).astype(BF16)
            held.append((q, k, v, bb, gcb, kb))
            kb_last = kb

        def late_piece(h, zero):
            q, k, v, bb, gcb, kb = held[h]
            gcz = gcb + zero[0:1, :]
            g_last = gcz[CHUNK - 1:CHUNK, :]
            eg = jnp.exp(gcz)
            kdt_scr[c, h] = (k * jnp.exp(g_last - gcz)).T.astype(BF16)
            egl_scr[c, h] = jnp.broadcast_to(jnp.exp(g_last), (8, LANES))
            rhs[h] = jnp.concatenate([v * bb, kb * eg], axis=1).astype(BF16)
            qs[h] = q * eg

        pieces = ([functools.partial(late_piece, h) for h in heads]
                  + front_pieces(jnp.minimum(c + 1, nc - 1)))
        n_slots = 7

        def fill(slot, operand):
            zero = anchor(operand)
            for piece in pieces[slot::n_slots]:
                piece(zero)

        state_in = recur_read(behind)
        fill(0, kb_last)
        ps = []
        for h in heads:
            nb = ns[h].astype(BF16)
            ps.append(_dot(nb, nb))
        fill(1, ns[-1])
        xs_ = [eye + ns[h] for h in heads]
        for level in range(4):
            operand = xs_[-1]
            for h in heads:
                r = _dot(jnp.concatenate([xs_[h], ps[h]], axis=0).astype(BF16), ps[h].astype(BF16))
                xs_[h] = xs_[h] + r[:CHUNK]
                ps[h] = r[CHUNK:]
            if level == 0:
                outs = recur_update(behind, *state_in, live)
            if level == 2:
                recur_finish(behind, outs)
            fill(2 + level, operand)
        operand = xs_[-1]
        for h in heads:
            xs_[h] = xs_[h] + _dot(xs_[h].astype(BF16), ps[h].astype(BF16))
        fill(6, operand)
        for h in heads:
            uw = _dot(xs_[h].astype(BF16), rhs[h])
            u_scr[h, pl.ds(r0, CHUNK), :] = uw[:, :HEAD_DIM]
            wq_scr[c, h] = jnp.concatenate([uw[:, HEAD_DIM:], qs[h]], axis=0).astype(BF16)
        front_finish()
        return carry

    lax.fori_loop(0, nc, prep, 0)
    last = nc - 1
    recur_finish(last, recur_update(last, *recur_read(last), None))

    for which, x_ref in enumerate((q_ref, k_ref, v_ref)):
        tail_scr[which] = x_ref[lb - look:lb, :]


def _deltanet(main, small, conv_w, alog_row, dtb_row, dn_norm_w, *, batch, seq, lb):
    t = batch * seq
    nlb = seq // lb
    nc = lb // CHUNK
    wblk = DN_WIDTH // LANES

    def col(cb):
        return pl.BlockSpec((lb, DN_WIDTH), lambda b, l, cb=cb: (b * nlb + l, cb // wblk))

    row = pl.BlockSpec((1, LANES), lambda b, l: (0, 0))
    return pl.pallas_call(
        functools.partial(_deltanet_kernel, lb=lb),
        grid=(batch, nlb),
        in_specs=[col(CB_DQ), col(CB_DK), col(CB_DV), col(CB_DZ),
                  pl.BlockSpec((lb, LANES), lambda b, l: (b * nlb + l, 0)),
                  pl.BlockSpec(conv_w.shape, lambda b, l: (0, 0)), row, row, row],
        out_specs=pl.BlockSpec((lb, DN_WIDTH), lambda b, l: (b * nlb + l, 0)),
        out_shape=jax.ShapeDtypeStruct((t, DN_WIDTH), BF16),
        scratch_shapes=[
            pltpu.VMEM((DN_HEADS, HEAD_DIM, HEAD_DIM), F32),
            pltpu.VMEM((3, 8, DN_WIDTH), F32),
            pltpu.VMEM((DN_HEADS, lb, HEAD_DIM), F32),
            pltpu.VMEM((nc, DN_HEADS, 2 * CHUNK, HEAD_DIM), BF16),
            pltpu.VMEM((nc, DN_HEADS, CHUNK, CHUNK), BF16),
            pltpu.VMEM((nc, DN_HEADS, HEAD_DIM, CHUNK), BF16),
            pltpu.VMEM((nc, DN_HEADS, 8, LANES), F32),
            pltpu.VMEM((3, CHUNK, DN_WIDTH), F32),
            pltpu.VMEM((2, CHUNK, LANES), F32),
            pltpu.VMEM((LANES, CHUNK), F32),
            pltpu.VMEM((2 * DN_HEADS * CHUNK, HEAD_DIM), F32),
        ],
        compiler_params=_cparams(("parallel", "arbitrary")),
        name="deltanet",
    )(main, main, main, main, small, conv_w, alog_row, dtb_row, dn_norm_w)


def _attention_kernel(q_ref, k_ref, v_ref, tc_ref, ts_ref, o_ref,
                      qr_scr, kr_scr, og_scr, lse_scr, *, seq):
    group = pl.program_id(2)
    rows = 256
    half = ROPE_DIM // 2
    lane = lax.broadcasted_iota(jnp.int32, (rows, LANES), 1)
    src_lane = jnp.bitwise_and(lax.broadcasted_iota(jnp.int32, (2 * LANES, LANES), 0), LANES - 1)
    dst_lane = lax.broadcasted_iota(jnp.int32, (2 * LANES, LANES), 1)
    want = jnp.where(dst_lane < half, dst_lane + half, jnp.where(dst_lane < ROPE_DIM, dst_lane - half, -1))
    swap = jnp.where(src_lane == want, 1.0, 0.0).astype(BF16)

    def rope(i, carry):
        r0 = pl.multiple_of(i * rows, rows)
        tc = tc_ref[pl.ds(r0, rows), :]
        ts = ts_ref[pl.ds(r0, rows), :]
        xq = q_ref[pl.ds(r0, rows), :]
        partner_q = jnp.where(lane < half, pltpu.roll(xq, LANES - half, axis=1),
                              pltpu.roll(xq, half, axis=1))
        qr_scr[pl.ds(r0, rows), :] = (xq * tc + partner_q * ts) * (HEAD_DIM ** -0.5)
        xk = k_ref[pl.ds(r0, rows), :]
        kr_scr[pl.ds(r0, rows), :] = xk * tc + _dot(_split_bf16(xk), swap) * ts
        return carry

    lax.fori_loop(0, seq // rows, rope, 0)

    ii = lax.broadcasted_iota(jnp.int32, (ATT_BLOCK, 2 * ATT_BLOCK), 0)
    jj = lax.broadcasted_iota(jnp.int32, (ATT_BLOCK, 2 * ATT_BLOCK), 1)
    dist = ii + ATT_BLOCK - jj
    band_mask = jnp.where(dist >= 0, jnp.where(dist <= ATT_BLOCK, 0.0, NEG_BIG), NEG_BIG)
    prev_half = jnp.where(jj < ATT_BLOCK, 1.0, 0.0)
    per_iter = 8

    def run_group(gi, dil):
        nb = seq // (dil * ATT_BLOCK)
        span = dil * ATT_BLOCK

        def rows_of(start):
            if dil == 1:
                return pl.ds(start, ATT_BLOCK)
            return pl.ds(start, ATT_BLOCK, stride=dil)

        def blocks(it, carry):
            where = []
            for u in range(per_iter):
                t = it * per_iter + u
                r = lax.shift_right_logical(t, nb.bit_length() - 1)
                n = jnp.bitwise_and(t, nb - 1)
                start = r + n * span
                where.append((n, start, jnp.maximum(start - span, r)))
            run = min(nb, per_iter)
            kcur = [kr_scr[rows_of(start), :].astype(BF16) for _, start, _ in where]
            vcur = [v_ref[rows_of(start), :].astype(BF16) for _, start, _ in where]

            def prev_of(u, cur, ref):
                if u % run:
                    return cur[u - 1]
                if nb <= per_iter:
                    return cur[u]
                return ref[rows_of(where[u][2]), :].astype(BF16)

            scores = []
            for u, (n, start, prev) in enumerate(where):
                qb = qr_scr[rows_of(start), :].astype(BF16)
                kcat = jnp.concatenate([prev_of(u, kcur, kr_scr), kcur[u]], axis=0)
                s = _dot_nt(qb, kcat)
                no_prev = jnp.where(n > 0, 0.0, NEG_BIG)
                s = s + (band_mask + prev_half * no_prev)
                scores.append(s)
            probs = []
            for s in scores:
                m = jnp.max(s, axis=1, keepdims=True)
                p = jnp.exp(s - m)
                probs.append((m, p, jnp.sum(p, axis=1, keepdims=True)))
            for u, ((n, start, prev), (m, p, den)) in enumerate(zip(where, probs)):
                vcat = jnp.concatenate([prev_of(u, vcur, v_ref), vcur[u]], axis=0)
                acc = _dot(p.astype(BF16), vcat)
                og_scr[gi, rows_of(start), :] = acc / den
                lse_scr[gi, rows_of(start), :] = jnp.broadcast_to(m + jnp.log(den),
                                                                  (ATT_BLOCK, LANES))
            return carry

        lax.fori_loop(0, seq // (ATT_BLOCK * per_iter), blocks, 0)

    for gi, (_, dil) in enumerate(ATT_GROUPS):
        @pl.when(group == gi)
        def _(gi=gi, dil=dil):
            run_group(gi, dil)

    @pl.when(group == len(ATT_GROUPS) - 1)
    def _():
        def merge(i, carry):
            r0 = pl.multiple_of(i * rows, rows)
            ls = [lse_scr[g, pl.ds(r0, rows), :] for g in range(len(ATT_GROUPS))]
            mx = functools.reduce(jnp.maximum, ls)
            es = [jnp.exp(l - mx) for l in ls]
            num = functools.reduce(
                lambda a, b: a + b, [e * og_scr[g, pl.ds(r0, rows), :] for g, e in enumerate(es)])
            den = functools.reduce(lambda a, b: a + b, es)
            o_ref[pl.ds(r0, rows), :] = (num / den).astype(o_ref.dtype)
            return carry

        lax.fori_loop(0, seq // rows, merge, 0)


def _attention(main, rope_cos, rope_sin, *, batch, seq):
    t = batch * seq
    ng = len(ATT_GROUPS)

    def col(cb):
        return pl.BlockSpec((seq, HEAD_DIM),
                            lambda b, h, g, cb=cb: (b, cb + g * ATT_HEADS_PER_GROUP + h))

    tab = pl.BlockSpec((seq, LANES), lambda b, h, g: (0, 0))
    return pl.pallas_call(
        functools.partial(_attention_kernel, seq=seq),
        grid=(batch, ATT_HEADS_PER_GROUP, ng),
        in_specs=[col(CB_AQ), col(CB_AK), col(CB_AV), tab, tab],
        out_specs=pl.BlockSpec((seq, HEAD_DIM), lambda b, h, g: (b, h)),
        out_shape=jax.ShapeDtypeStruct((t, ATT_OUT_WIDTH), BF16),
        scratch_shapes=[
            pltpu.VMEM((seq, HEAD_DIM), F32),
            pltpu.VMEM((seq, HEAD_DIM), F32),
            pltpu.VMEM((ng, seq, HEAD_DIM), F32),
            pltpu.VMEM((ng, seq, LANES), F32),
        ],
        compiler_params=_cparams(("parallel", "parallel", "arbitrary")),
        name="dilated_attention",
    )(main, main, main, rope_cos, rope_sin)


def _rope_tables(seq):
    half = ROPE_DIM // 2
    inv_freq = jnp.power(ROPE_THETA, -jnp.arange(half, dtype=F32) * (2.0 / ROPE_DIM))
    ang = jnp.arange(seq).astype(F32)[:, None] * inv_freq[None, :]
    cos, sin = jnp.cos(ang), jnp.sin(ang)
    rest = HEAD_DIM - ROPE_DIM
    tc = jnp.concatenate([cos, cos, jnp.ones((seq, rest), F32)], axis=1)
    ts = jnp.concatenate([-sin, sin, jnp.zeros((seq, rest), F32)], axis=1)
    return tc, ts


def _merge_kernel(x_ref, oa_ref, ob_ref, ga_ref, gb_ref, wa_ref, wb_ref, wo_ref, o_ref):
    ya = _dot(oa_ref[...], wa_ref[...])
    yb = _dot(ob_ref[...], wb_ref[...])
    merged = _sigmoid(ga_ref[...]) * ya + _sigmoid(gb_ref[...]) * yb
    o_ref[...] = x_ref[...] + _dot(merged.astype(BF16), wo_ref[...])


def _merge(x2, oa, ob, main, w_a, w_b, w_o, *, tm):
    t, d = x2.shape
    gate_cb = CB_GA * LANES // d
    const = lambda shape: pl.BlockSpec(shape, lambda i: (0, 0))
    return pl.pallas_call(
        _merge_kernel,
        grid=(t // tm,),
        in_specs=[
            pl.BlockSpec((tm, d), lambda i: (i, 0)),
            pl.BlockSpec((tm, DN_WIDTH), lambda i: (i, 0)),
            pl.BlockSpec((tm, ATT_OUT_WIDTH), lambda i: (i, 0)),
            pl.BlockSpec((tm, d), lambda i: (i, gate_cb)),
            pl.BlockSpec((tm, d), lambda i: (i, gate_cb + 1)),
            const(w_a.shape), const(w_b.shape), const(w_o.shape),
        ],
        out_specs=pl.BlockSpec((tm, d), lambda i: (i, 0)),
        out_shape=jax.ShapeDtypeStruct((t, d), F32),
        compiler_params=_cparams(("parallel",)),
        name="branch_merge",
    )(x2, oa, ob, main, main, w_a, w_b, w_o)


def _ffn_kernel(x_ref, nw_ref, wg_ref, wu_ref, wd_ref, fw_ref, o_ref, *, final_norm, tf):
    x = x_ref[...]
    h = (x * lax.rsqrt(jnp.mean(x * x, axis=-1, keepdims=True) + EPS) * nw_ref[...]).astype(BF16)
    d_ff = wd_ref.shape[0]
    tiles = [slice(f0, min(f0 + tf, d_ff)) for f0 in range(0, d_ff, tf)]
    y = x
    pending = None
    for cols in tiles + [None]:
        issued = None if cols is None else (_dot(h, wg_ref[:, cols]), _dot(h, wu_ref[:, cols]))
        if pending is not None:
            pcols, (gate, up) = pending
            y = y + _dot((gate * _sigmoid(gate) * up).astype(BF16), wd_ref[pcols, :])
        pending = (cols, issued)
    if final_norm:
        y = y * lax.rsqrt(jnp.mean(y * y, axis=-1, keepdims=True) + EPS) * fw_ref[...]
    o_ref[...] = y


def _ffn(x1, norm_w, w_gate_up, w_down, final_w, *, tm, tf, final_norm):
    t, d = x1.shape
    d_ff = w_down.shape[0]
    once = pl.Buffered(1)
    return pl.pallas_call(
        functools.partial(_ffn_kernel, final_norm=final_norm, tf=tf),
        grid=(t // tm,),
        in_specs=[
            pl.BlockSpec((tm, d), lambda i: (i, 0)),
            pl.BlockSpec((1, d), lambda i: (0, 0)),
            pl.BlockSpec((d, d_ff), lambda i: (0, 0), pipeline_mode=once),
            pl.BlockSpec((d, d_ff), lambda i: (0, 1), pipeline_mode=once),
            pl.BlockSpec((d_ff, d), lambda i: (0, 0), pipeline_mode=once),
            pl.BlockSpec((1, d), lambda i: (0, 0)),
        ],
        out_specs=pl.BlockSpec((tm, d), lambda i: (i, 0)),
        out_shape=jax.ShapeDtypeStruct((t, d), F32),
        compiler_params=_cparams(("parallel",)),
        name="swiglu_ffn",
    )(x1, norm_w, w_gate_up, w_gate_up, w_down, final_w)


def _pad_lanes(v, offset):
    out = jnp.zeros((1, LANES), F32)
    return lax.dynamic_update_slice(out, v.reshape(1, -1).astype(F32), (0, offset))


def kernel(x, norm1_w, w_in, conv_w, a_log, dt_bias, dn_norm_w, w_proj_a, w_proj_b, w_out,
           norm2_w, w_gate_up, w_down, final_norm_w):
    batch, seq, d_model = x.shape
    depth = w_in.shape[0]
    t = batch * seq
    assert d_model == 8 * LANES and seq % (ATT_GROUPS[-1][1] * ATT_BLOCK) == 0
    d_ff = w_down.shape[1]

    o_dz_end = 4 * DN_WIDTH
    o_small_end = o_dz_end + 2 * DN_HEADS
    o_att_end = o_small_end + 3 * ATT_WIDTH

    rope_cos, rope_sin = _rope_tables(seq)
    x2 = x.reshape(t, d_model)
    for i in range(depth):
        wi = w_in[i]
        w_main = jnp.concatenate(
            [wi[:, :o_dz_end], wi[:, o_att_end:], wi[:, o_small_end:o_att_end]], axis=1).astype(BF16)
        w_small = jnp.pad(wi[:, o_dz_end:o_small_end],
                          ((0, 0), (0, LANES - 2 * DN_HEADS))).astype(BF16)
        main, small = _in_projection(x2, norm1_w[i].reshape(1, d_model), w_main, w_small,
                                     tm=1024, tn=MAIN_WIDTH // 6)
        oa = _deltanet(main, small, conv_w[i], _pad_lanes(a_log[i], DN_HEADS),
                       _pad_lanes(dt_bias[i], DN_HEADS), dn_norm_w[i].reshape(1, HEAD_DIM),
                       batch=batch, seq=seq, lb=512)
        ob = _attention(main, rope_cos, rope_sin, batch=batch, seq=seq)
        x2 = _merge(x2, oa, ob, main, w_proj_a[i].astype(BF16), w_proj_b[i].astype(BF16),
                    w_out[i].astype(BF16), tm=512)
        x2 = _ffn(x2, norm2_w[i].reshape(1, d_model), w_gate_up[i].astype(BF16),
                  w_down[i].astype(BF16), final_norm_w.reshape(1, d_model), tm=512, tf=3 * MXU_WIDTH,
                  final_norm=(i == depth - 1))
    return x2.reshape(batch, seq, d_model)
```

```python
import functools
import math

import jax
import jax.numpy as jnp
from jax import lax
from jax.experimental import pallas as pl
from jax.experimental.pallas import tpu as pltpu

F32 = jnp.float32
BF16 = jnp.bfloat16

EPS = 1e-6
LANES = 128
MXU_WIDTH = 256
HEAD_DIM = 128
DN_HEADS = 8
DN_WIDTH = DN_HEADS * HEAD_DIM
CONV_WIDTH = 4
CHUNK = 64
ATT_GROUPS = ((128, 1), (512, 4), (2048, 16))
ATT_HEADS_PER_GROUP = 4
ATT_N_HEADS = len(ATT_GROUPS) * ATT_HEADS_PER_GROUP
ATT_WIDTH = ATT_N_HEADS * HEAD_DIM
ATT_OUT_WIDTH = ATT_HEADS_PER_GROUP * HEAD_DIM
ATT_BLOCK = 128
ROPE_THETA = 500000.0
ROPE_DIM = HEAD_DIM // 4
NEG_BIG = -1e30

CB_DQ, CB_DK, CB_DV, CB_DZ = 0, 8, 16, 24
CB_GA, CB_GB = 32, 40
CB_AQ, CB_AK, CB_AV = 48, 60, 72
MAIN_WIDTH = 84 * LANES

VMEM_LIMIT = 56 * 1024 * 1024


def _cparams(sem):
    return pltpu.CompilerParams(dimension_semantics=sem, vmem_limit_bytes=VMEM_LIMIT)


def _dot(a, b):
    return jnp.dot(a, b, preferred_element_type=F32)


def _dot_nt(a, b):
    return lax.dot_general(a, b, (((1,), (1,)), ((), ())), preferred_element_type=F32)


def _dot_f32(a, b):
    return jnp.dot(a, b, preferred_element_type=F32, precision=lax.Precision.HIGHEST)


def _sigmoid(x):
    return 0.5 * jnp.tanh(0.5 * x) + 0.5


def _silu(x):
    half = 0.5 * x
    return half * (jnp.tanh(half) + 1.0)


def _split_bf16(x):
    hi = x.astype(BF16)
    lo = (x - hi.astype(F32)).astype(BF16)
    return jnp.concatenate([hi, lo], axis=1)


def _inproj_kernel(x_ref, nw_ref, w_ref, ws_ref, o_ref, os_ref, h_ref):
    @pl.when(pl.program_id(1) == 0)
    def _():
        x = x_ref[...]
        h = x * lax.rsqrt(jnp.mean(x * x, axis=-1, keepdims=True) + EPS) * nw_ref[...]
        hb = h.astype(BF16)
        h_ref[...] = hb
        os_ref[...] = _dot(hb, ws_ref[...])

    o_ref[...] = _dot(h_ref[...], w_ref[...])


def _in_projection(x2, norm_w, w_main, w_small, *, tm, tn):
    t, d = x2.shape
    n = w_main.shape[1]
    return pl.pallas_call(
        _inproj_kernel,
        grid=(t // tm, n // tn),
        in_specs=[
            pl.BlockSpec((tm, d), lambda i, j: (i, 0)),
            pl.BlockSpec((1, d), lambda i, j: (0, 0)),
            pl.BlockSpec((d, tn), lambda i, j: (0, j)),
            pl.BlockSpec((d, LANES), lambda i, j: (0, 0)),
        ],
        out_specs=[
            pl.BlockSpec((tm, tn), lambda i, j: (i, j)),
            pl.BlockSpec((tm, LANES), lambda i, j: (i, 0)),
        ],
        out_shape=[
            jax.ShapeDtypeStruct((t, n), F32),
            jax.ShapeDtypeStruct((t, LANES), F32),
        ],
        scratch_shapes=[pltpu.VMEM((tm, d), BF16)],
        compiler_params=_cparams(("parallel", "arbitrary")),
        name="in_projection",
    )(x2, norm_w, w_main, w_small)


def _deltanet_kernel(q_ref, k_ref, v_ref, z_ref, sm_ref, cw_ref, alog_ref, dtb_ref, nw_ref, o_ref,
                     s_scr, tail_scr, u_scr, wq_scr, qk_scr, kdt_scr, egl_scr,
                     pre_scr, gate_scr, gct_scr, sq_scr, *, lb):
    lstep = pl.program_id(1)
    nc = lb // CHUNK
    width = DN_WIDTH
    look = 8

    @pl.when(lstep == 0)
    def _():
        s_scr[...] = jnp.zeros_like(s_scr)
        tail_scr[...] = jnp.zeros_like(tail_scr)
        u_scr[:, 0:CHUNK, :] = jnp.zeros((DN_HEADS, CHUNK, HEAD_DIM), F32)
        wq_scr[0] = jnp.zeros(wq_scr.shape[1:], BF16)
        qk_scr[0] = jnp.zeros(qk_scr.shape[1:], BF16)
        kdt_scr[0] = jnp.zeros(kdt_scr.shape[1:], BF16)
        egl_scr[0] = jnp.zeros(egl_scr.shape[1:], F32)

    ii = lax.broadcasted_iota(jnp.int32, (CHUNK, CHUNK), 0)
    jj = lax.broadcasted_iota(jnp.int32, (CHUNK, CHUNK), 1)
    eye = (ii == jj).astype(F32)
    rowi = lax.broadcasted_iota(jnp.int32, (CHUNK, LANES), 0)

    heads = range(DN_HEADS)
    head_cols = [slice(h * HEAD_DIM, (h + 1) * HEAD_DIM) for h in heads]
    srcs = (q_ref, k_ref, v_ref)

    def anchor(value):
        rows8 = value[:look, :]
        zero = jnp.where(rows8 != rows8, rows8, 0.0)
        if zero.shape[1] < LANES:
            zero = jnp.concatenate([zero] * (LANES // zero.shape[1]), axis=1)
        return zero

    def conv_piece(which, h, c, r0, zero):
        x_ref, cols = srcs[which], head_cols[h]
        prev0 = pl.multiple_of(jnp.maximum(r0 - look, 0), look)
        top = lax.select(c == 0, tail_scr[which, :, cols], x_ref[pl.ds(prev0, look), cols])
        win = jnp.concatenate([top, x_ref[pl.ds(r0, CHUNK), cols]], axis=0)
        w0 = which * width + h * HEAD_DIM
        acc = None
        for j in range(CONV_WIDTH):
            back = CONV_WIDTH - 1 - j
            rows = win if back == 0 else pltpu.roll(win, back, axis=0)
            term = rows[look:, :] * (cw_ref[j:j + 1, w0:w0 + HEAD_DIM] + zero[0:1, :])
            acc = term if acc is None else acc + term
        y = _silu(acc)
        pre_scr[which, :, cols] = y
        if which < 2:
            sq_scr[(which * DN_HEADS + h) * CHUNK:(which * DN_HEADS + h + 1) * CHUNK, :] = y * y

    def gate_piece(r0, zero):
        sm = sm_ref[pl.ds(r0, CHUNK), :]
        xs = sm + (dtb_ref[...] + zero[0:1, :])
        softplus = jnp.maximum(xs, 0.0) + jnp.log(1.0 + jnp.exp(-jnp.abs(xs)))
        gc_all = -jnp.exp(alog_ref[...]) * softplus
        shift = 1
        while shift < CHUNK:
            gc_all = gc_all + jnp.where(rowi >= shift, pltpu.roll(gc_all, shift, axis=0), 0.0)
            shift *= 2
        gate_scr[0] = _sigmoid(sm)
        gate_scr[1] = gc_all
        gct_scr[...] = gc_all.T

    def front_pieces(c):
        r0 = pl.multiple_of(c * CHUNK, CHUNK)
        pieces = [functools.partial(conv_piece, which, h, c, r0)
                  for which in range(3) for h in heads]
        return pieces + [functools.partial(gate_piece, r0)]

    def front_finish():
        inv_norm = lax.rsqrt(
            _dot(_split_bf16(sq_scr[...]), jnp.ones((2 * HEAD_DIM, HEAD_DIM), BF16)) + EPS)
        for h in heads:
            cols = head_cols[h]
            pre_scr[0, :, cols] = pre_scr[0, :, cols] * (
                inv_norm[h * CHUNK:(h + 1) * CHUNK] * (HEAD_DIM ** -0.5))
            pre_scr[1, :, cols] = pre_scr[1, :, cols] * inv_norm[
                (DN_HEADS + h) * CHUNK:(DN_HEADS + h + 1) * CHUNK]

    for piece in front_pieces(0):
        piece(jnp.zeros((look, LANES), F32))
    front_finish()

    def recur_read(cs):
        ss = [s_scr[h] for h in heads]
        rs = [_dot(wq_scr[cs, h], ss[h].astype(BF16)) for h in heads]
        return ss, rs

    def recur_update(cs, ss, rs, live):
        rc0 = pl.multiple_of(cs * CHUNK, CHUNK)
        vbs = []
        for h in heads:
            v_new = u_scr[h, pl.ds(rc0, CHUNK), :] - rs[h][:CHUNK]
            egl = egl_scr[cs, h][0:1, :]
            if live is not None:
                v_new = lax.select(live, v_new, jnp.zeros_like(v_new))
                egl = lax.select(live, egl, jnp.ones_like(egl))
            vbs.append(v_new.astype(BF16))
            s_scr[h] = ss[h] * egl + _dot(kdt_scr[cs, h], vbs[h])
        return [rs[h][CHUNK:] + _dot(qk_scr[cs, h], vbs[h]) for h in heads]

    def recur_finish(cs, outs):
        rc0 = pl.multiple_of(cs * CHUNK, CHUNK)
        for h in heads:
            o = outs[h]
            on = o * lax.rsqrt(jnp.mean(o * o, axis=-1, keepdims=True) + EPS) * nw_ref[...]
            z = z_ref[pl.ds(rc0, CHUNK), head_cols[h]]
            o_ref[pl.ds(rc0, CHUNK), head_cols[h]] = (on * _silu(z)).astype(o_ref.dtype)

    def prep(c, carry):
        r0 = pl.multiple_of(c * CHUNK, CHUNK)
        behind = jnp.maximum(c - 1, 0)
        live = c > 0
        beta_all = gate_scr[0]
        gc_all = gate_scr[1]
        gc_t = gct_scr[...]
        ns, held = [], []
        qs, rhs = [None] * DN_HEADS, [None] * DN_HEADS
        for h in heads:
            cols = head_cols[h]
            q = pre_scr[0, :, cols]
            k = pre_scr[1, :, cols]
            v = pre_scr[2, :, cols]
            bb = jnp.broadcast_to(beta_all[:, h:h + 1], (CHUNK, LANES))
            gcb = jnp.broadcast_to(gc_all[:, DN_HEADS + h:DN_HEADS + h + 1], (CHUNK, LANES))
            g_row = gc_t[DN_HEADS + h:DN_HEADS + h + 1, :]
            decay = jnp.where(ii >= jj, jnp.exp(jnp.minimum(gcb[:, :CHUNK] - g_row, 0.0)), 0.0)
            kb = k * bb
            both = _dot_nt(jnp.concatenate([kb, q], axis=0).astype(BF16), k.astype(BF16))
            ns.append(jnp.where(ii > jj, -both[:CHUNK] * decay, 0.0))
            qk_scr[c, h] = (both[CHUNK:] * decay).astype(BF16)
            held.append((q, k, v, bb, gcb, kb))
            kb_last = kb

        def late_piece(h, zero):
            q, k, v, bb, gcb, kb = held[h]
            gcz = gcb + zero[0:1, :]
            g_last = gcz[CHUNK - 1:CHUNK, :]
            eg = jnp.exp(gcz)
            kdt_scr[c, h] = (k * jnp.exp(g_last - gcz)).T.astype(BF16)
            egl_scr[c, h] = jnp.broadcast_to(jnp.exp(g_last), (8, LANES))
            rhs[h] = jnp.concatenate([v * bb, kb * eg], axis=1).astype(BF16)
            qs[h] = q * eg

        pieces = ([functools.partial(late_piece, h) for h in heads]
                  + front_pieces(jnp.minimum(c + 1, nc - 1)))
        n_slots = 7

        def fill(slot, operand):
            zero = anchor(operand)
            for piece in pieces[slot::n_slots]:
                piece(zero)

        state_in = recur_read(behind)
        fill(0, kb_last)
        ps = []
        for h in heads:
            nb = ns[h].astype(BF16)
            ps.append(_dot(nb, nb))
        fill(1, ns[-1])
        xs_ = [eye + ns[h] for h in heads]
        for level in range(4):
            operand = xs_[-1]
            for h in heads:
                r = _dot(jnp.concatenate([xs_[h], ps[h]], axis=0).astype(BF16), ps[h].astype(BF16))
                xs_[h] = xs_[h] + r[:CHUNK]
                ps[h] = r[CHUNK:]
            if level == 0:
                outs = recur_update(behind, *state_in, live)
            if level == 2:
                recur_finish(behind, outs)
            fill(2 + level, operand)
        operand = xs_[-1]
        for h in heads:
            xs_[h] = xs_[h] + _dot(xs_[h].astype(BF16), ps[h].astype(BF16))
        fill(6, operand)
        for h in heads:
            uw = _dot(xs_[h].astype(BF16), rhs[h])
            u_scr[h, pl.ds(r0, CHUNK), :] = uw[:, :HEAD_DIM]
            wq_scr[c, h] = jnp.concatenate([uw[:, HEAD_DIM:], qs[h]], axis=0).astype(BF16)
        front_finish()
        return carry

    lax.fori_loop(0, nc, prep, 0)
    last = nc - 1
    recur_finish(last, recur_update(last, *recur_read(last), None))

    for which, x_ref in enumerate((q_ref, k_ref, v_ref)):
        tail_scr[which] = x_ref[lb - look:lb, :]


def _deltanet(main, small, conv_w, alog_row, dtb_row, dn_norm_w, *, batch, seq, lb):
    t = batch * seq
    nlb = seq // lb
    nc = lb // CHUNK
    wblk = DN_WIDTH // LANES

    def col(cb):
        return pl.BlockSpec((lb, DN_WIDTH), lambda b, l, cb=cb: (b * nlb + l, cb // wblk))

    row = pl.BlockSpec((1, LANES), lambda b, l: (0, 0))
    return pl.pallas_call(
        functools.partial(_deltanet_kernel, lb=lb),
        grid=(batch, nlb),
        in_specs=[col(CB_DQ), col(CB_DK), col(CB_DV), col(CB_DZ),
                  pl.BlockSpec((lb, LANES), lambda b, l: (b * nlb + l, 0)),
                  pl.BlockSpec(conv_w.shape, lambda b, l: (0, 0)), row, row, row],
        out_specs=pl.BlockSpec((lb, DN_WIDTH), lambda b, l: (b * nlb + l, 0)),
        out_shape=jax.ShapeDtypeStruct((t, DN_WIDTH), BF16),
        scratch_shapes=[
            pltpu.VMEM((DN_HEADS, HEAD_DIM, HEAD_DIM), F32),
            pltpu.VMEM((3, 8, DN_WIDTH), F32),
            pltpu.VMEM((DN_HEADS, lb, HEAD_DIM), F32),
            pltpu.VMEM((nc, DN_HEADS, 2 * CHUNK, HEAD_DIM), BF16),
            pltpu.VMEM((nc, DN_HEADS, CHUNK, CHUNK), BF16),
            pltpu.VMEM((nc, DN_HEADS, HEAD_DIM, CHUNK), BF16),
            pltpu.VMEM((nc, DN_HEADS, 8, LANES), F32),
            pltpu.VMEM((3, CHUNK, DN_WIDTH), F32),
            pltpu.VMEM((2, CHUNK, LANES), F32),
            pltpu.VMEM((LANES, CHUNK), F32),
            pltpu.VMEM((2 * DN_HEADS * CHUNK, HEAD_DIM), F32),
        ],
        compiler_params=_cparams(("parallel", "arbitrary")),
        name="deltanet",
    )(main, main, main, main, small, conv_w, alog_row, dtb_row, dn_norm_w)


def _attention_kernel(q_ref, k_ref, v_ref, tc_ref, ts_ref, o_ref,
                      qr_scr, kr_scr, og_scr, lse_scr, *, seq):
    group = pl.program_id(2)
    rows = 256
    half = ROPE_DIM // 2
    lane = lax.broadcasted_iota(jnp.int32, (rows, LANES), 1)
    src_lane = jnp.bitwise_and(lax.broadcasted_iota(jnp.int32, (2 * LANES, LANES), 0), LANES - 1)
    dst_lane = lax.broadcasted_iota(jnp.int32, (2 * LANES, LANES), 1)
    want = jnp.where(dst_lane < half, dst_lane + half, jnp.where(dst_lane < ROPE_DIM, dst_lane - half, -1))
    swap = jnp.where(src_lane == want, 1.0, 0.0).astype(BF16)

    def rope(i, carry):
        r0 = pl.multiple_of(i * rows, rows)
        tc = tc_ref[pl.ds(r0, rows), :]
        ts = ts_ref[pl.ds(r0, rows), :]
        xq = q_ref[pl.ds(r0, rows), :]
        partner_q = jnp.where(lane < half, pltpu.roll(xq, LANES - half, axis=1),
                              pltpu.roll(xq, half, axis=1))
        qr_scr[pl.ds(r0, rows), :] = (xq * tc + partner_q * ts) * (HEAD_DIM ** -0.5)
        xk = k_ref[pl.ds(r0, rows), :]
        kr_scr[pl.ds(r0, rows), :] = xk * tc + _dot(_split_bf16(xk), swap) * ts
        return carry

    lax.fori_loop(0, seq // rows, rope, 0, unroll=4)

    ii = lax.broadcasted_iota(jnp.int32, (ATT_BLOCK, 2 * ATT_BLOCK), 0)
    jj = lax.broadcasted_iota(jnp.int32, (ATT_BLOCK, 2 * ATT_BLOCK), 1)
    dist = ii + ATT_BLOCK - jj
    band_mask = jnp.where(dist >= 0, jnp.where(dist <= ATT_BLOCK, 0.0, NEG_BIG), NEG_BIG)
    prev_half = jnp.where(jj < ATT_BLOCK, 1.0, 0.0)
    per_iter = 8

    def run_group(gi, dil):
        nb = seq // (dil * ATT_BLOCK)
        span = dil * ATT_BLOCK

        def rows_of(start):
            if dil == 1:
                return pl.ds(start, ATT_BLOCK)
            return pl.ds(start, ATT_BLOCK, stride=dil)

        def blocks(it, carry):
            where = []
            for u in range(per_iter):
                t = it * per_iter + u
                r = lax.shift_right_logical(t, nb.bit_length() - 1)
                n = jnp.bitwise_and(t, nb - 1)
                start = r + n * span
                where.append((n, start, jnp.maximum(start - span, r)))
            run = min(nb, per_iter)
            kcur = [kr_scr[rows_of(start), :].astype(BF16) for _, start, _ in where]
            vcur = [v_ref[rows_of(start), :].astype(BF16) for _, start, _ in where]

            def prev_of(u, cur, ref):
                if u % run:
                    return cur[u - 1]
                if nb <= per_iter:
                    return cur[u]
                return ref[rows_of(where[u][2]), :].astype(BF16)

            scores = []
            for u, (n, start, prev) in enumerate(where):
                qb = qr_scr[rows_of(start), :].astype(BF16)
                kcat = jnp.concatenate([prev_of(u, kcur, kr_scr), kcur[u]], axis=0)
                s = _dot_nt(qb, kcat)
                no_prev = jnp.where(n > 0, 0.0, NEG_BIG)
                s = s + (band_mask + prev_half * no_prev)
                scores.append(s)
            probs = []
            for s in scores:
                m = jnp.max(s, axis=1, keepdims=True)
                p = jnp.exp(s - m)
                probs.append((m, p, jnp.sum(p, axis=1, keepdims=True)))
            for u, ((n, start, prev), (m, p, den)) in enumerate(zip(where, probs)):
                vcat = jnp.concatenate([prev_of(u, vcur, v_ref), vcur[u]], axis=0)
                acc = _dot(p.astype(BF16), vcat)
                og_scr[gi, rows_of(start), :] = acc / den
                lse_scr[gi, rows_of(start), :] = jnp.broadcast_to(m + jnp.log(den),
                                                                  (ATT_BLOCK, LANES))
            return carry

        lax.fori_loop(0, seq // (ATT_BLOCK * per_iter), blocks, 0)

    for gi, (_, dil) in enumerate(ATT_GROUPS):
        @pl.when(group == gi)
        def _(gi=gi, dil=dil):
            run_group(gi, dil)

    @pl.when(group == len(ATT_GROUPS) - 1)
    def _():
        def merge(i, carry):
            r0 = pl.multiple_of(i * rows, rows)
            ls = [lse_scr[g, pl.ds(r0, rows), :] for g in range(len(ATT_GROUPS))]
            mx = functools.reduce(jnp.maximum, ls)
            es = [jnp.exp(l - mx) for l in ls]
            num = functools.reduce(
                lambda a, b: a + b, [e * og_scr[g, pl.ds(r0, rows), :] for g, e in enumerate(es)])
            den = functools.reduce(lambda a, b: a + b, es)
            o_ref[pl.ds(r0, rows), :] = (num / den).astype(o_ref.dtype)
            return carry

        lax.fori_loop(0, seq // rows, merge, 0)


def _attention(main, rope_cos, rope_sin, *, batch, seq):
    t = batch * seq
    ng = len(ATT_GROUPS)

    def col(cb):
        return pl.BlockSpec((seq, HEAD_DIM),
                            lambda b, h, g, cb=cb: (b, cb + g * ATT_HEADS_PER_GROUP + h))

    tab = pl.BlockSpec((seq, LANES), lambda b, h, g: (0, 0))
    return pl.pallas_call(
        functools.partial(_attention_kernel, seq=seq),
        grid=(batch, ATT_HEADS_PER_GROUP, ng),
        in_specs=[col(CB_AQ), col(CB_AK), col(CB_AV), tab, tab],
        out_specs=pl.BlockSpec((seq, HEAD_DIM), lambda b, h, g: (b, h)),
        out_shape=jax.ShapeDtypeStruct((t, ATT_OUT_WIDTH), BF16),
        scratch_shapes=[
            pltpu.VMEM((seq, HEAD_DIM), F32),
            pltpu.VMEM((seq, HEAD_DIM), F32),
            pltpu.VMEM((ng, seq, HEAD_DIM), F32),
            pltpu.VMEM((ng, seq, LANES), F32),
        ],
        compiler_params=_cparams(("parallel", "parallel", "arbitrary")),
        name="dilated_attention",
    )(main, main, main, rope_cos, rope_sin)


def _rope_tables(seq):
    half = ROPE_DIM // 2
    inv_freq = jnp.power(ROPE_THETA, -jnp.arange(half, dtype=F32) * (2.0 / ROPE_DIM))
    ang = jnp.arange(seq).astype(F32)[:, None] * inv_freq[None, :]
    cos, sin = jnp.cos(ang), jnp.sin(ang)
    rest = HEAD_DIM - ROPE_DIM
    tc = jnp.concatenate([cos, cos, jnp.ones((seq, rest), F32)], axis=1)
    ts = jnp.concatenate([-sin, sin, jnp.zeros((seq, rest), F32)], axis=1)
    return tc, ts


def _merge_ffn_kernel(x_ref, oa_ref, ob_ref, ga_ref, gb_ref, wa_ref, wb_ref, wo_ref,
                      nw_ref, wg_ref, wu_ref, wd_ref, fw_ref, o_ref, *, final_norm, tf):
    ya = _dot(oa_ref[...], wa_ref[...])
    yb = _dot(ob_ref[...], wb_ref[...])
    merged = _sigmoid(ga_ref[...]) * ya + _sigmoid(gb_ref[...]) * yb
    x = x_ref[...] + _dot(merged.astype(BF16), wo_ref[...])
    h = (x * lax.rsqrt(jnp.mean(x * x, axis=-1, keepdims=True) + EPS) * nw_ref[...]).astype(BF16)
    d_ff = wd_ref.shape[0]
    tiles = [slice(f0, min(f0 + tf, d_ff)) for f0 in range(0, d_ff, tf)]
    y = x
    pending = None
    for cols in tiles + [None]:
        issued = None if cols is None else (_dot(h, wg_ref[:, cols]), _dot(h, wu_ref[:, cols]))
        if pending is not None:
            pcols, (gate, up) = pending
            y = y + _dot((_silu(gate) * up).astype(BF16), wd_ref[pcols, :])
        pending = (cols, issued)
    if final_norm:
        y = y * lax.rsqrt(jnp.mean(y * y, axis=-1, keepdims=True) + EPS) * fw_ref[...]
    o_ref[...] = y


def _merge_ffn(x2, oa, ob, main, w_a, w_b, w_o, norm_w, w_gate_up, w_down, final_w, *,
               tm, tf, final_norm):
    t, d = x2.shape
    d_ff = w_down.shape[0]
    gate_cb = CB_GA * LANES // d
    once = pl.Buffered(1)

    def weight(shape, col=0):
        return pl.BlockSpec(shape, lambda i, col=col: (0, col), pipeline_mode=once)

    row = pl.BlockSpec((1, d), lambda i: (0, 0))
    return pl.pallas_call(
        functools.partial(_merge_ffn_kernel, final_norm=final_norm, tf=tf),
        grid=(t // tm,),
        in_specs=[
            pl.BlockSpec((tm, d), lambda i: (i, 0)),
            pl.BlockSpec((tm, DN_WIDTH), lambda i: (i, 0)),
            pl.BlockSpec((tm, ATT_OUT_WIDTH), lambda i: (i, 0)),
            pl.BlockSpec((tm, d), lambda i: (i, gate_cb)),
            pl.BlockSpec((tm, d), lambda i: (i, gate_cb + 1)),
            weight(w_a.shape), weight(w_b.shape), weight(w_o.shape),
            row, weight((d, d_ff)), weight((d, d_ff), 1), weight((d_ff, d)), row,
        ],
        out_specs=pl.BlockSpec((tm, d), lambda i: (i, 0)),
        out_shape=jax.ShapeDtypeStruct((t, d), F32),
        compiler_params=_cparams(("parallel",)),
        name="merge_ffn",
    )(x2, oa, ob, main, main, w_a, w_b, w_o, norm_w, w_gate_up, w_gate_up, w_down, final_w)


def _pad_lanes(v, offset):
    out = jnp.zeros((1, LANES), F32)
    return lax.dynamic_update_slice(out, v.reshape(1, -1).astype(F32), (0, offset))


def kernel(x, norm1_w, w_in, conv_w, a_log, dt_bias, dn_norm_w, w_proj_a, w_proj_b, w_out,
           norm2_w, w_gate_up, w_down, final_norm_w):
    batch, seq, d_model = x.shape
    depth = w_in.shape[0]
    t = batch * seq
    assert d_model == 8 * LANES and seq % (ATT_GROUPS[-1][1] * ATT_BLOCK) == 0
    d_ff = w_down.shape[1]

    o_dz_end = 4 * DN_WIDTH
    o_small_end = o_dz_end + 2 * DN_HEADS
    o_att_end = o_small_end + 3 * ATT_WIDTH

    rope_cos, rope_sin = _rope_tables(seq)
    x2 = x.reshape(t, d_model)
    for i in range(depth):
        wi = w_in[i]
        w_main = jnp.concatenate(
            [wi[:, :o_dz_end], wi[:, o_att_end:], wi[:, o_small_end:o_att_end]], axis=1).astype(BF16)
        w_small = jnp.pad(wi[:, o_dz_end:o_small_end],
                          ((0, 0), (0, LANES - 2 * DN_HEADS))).astype(BF16)
        main, small = _in_projection(x2, norm1_w[i].reshape(1, d_model), w_main, w_small,
                                     tm=1024, tn=MAIN_WIDTH // 6)
        oa = _deltanet(main, small, conv_w[i], _pad_lanes(a_log[i], DN_HEADS),
                       _pad_lanes(dt_bias[i], DN_HEADS), dn_norm_w[i].reshape(1, HEAD_DIM),
                       batch=batch, seq=seq, lb=512)
        ob = _attention(main, rope_cos, rope_sin, batch=batch, seq=seq)
        x2 = _merge_ffn(x2, oa, ob, main, w_proj_a[i].astype(BF16), w_proj_b[i].astype(BF16),
                        w_out[i].astype(BF16), norm2_w[i].reshape(1, d_model),
                        w_gate_up[i].astype(BF16), w_down[i].astype(BF16),
                        final_norm_w.reshape(1, d_model), tm=512, tf=3 * MXU_WIDTH,
                        final_norm=(i == depth - 1))
    return x2.reshape(batch, seq, d_model)
```

```python
import functools
import math

import jax
import jax.numpy as jnp
from jax import lax
from jax.experimental import pallas as pl
from jax.experimental.pallas import tpu as pltpu

F32 = jnp.float32
BF16 = jnp.bfloat16

EPS = 1e-6
LANES = 128
MXU_WIDTH = 256
HEAD_DIM = 128
DN_HEADS = 8
DN_WIDTH = DN_HEADS * HEAD_DIM
CONV_WIDTH = 4
CHUNK = 64
ATT_GROUPS = ((128, 1), (512, 4), (2048, 16))
ATT_HEADS_PER_GROUP = 4
ATT_N_HEADS = len(ATT_GROUPS) * ATT_HEADS_PER_GROUP
ATT_WIDTH = ATT_N_HEADS * HEAD_DIM
ATT_OUT_WIDTH = ATT_HEADS_PER_GROUP * HEAD_DIM
ATT_BLOCK = 128
ROPE_THETA = 500000.0
ROPE_DIM = HEAD_DIM // 4
NEG_BIG = -1e30

CB_DQ, CB_DK, CB_DV, CB_DZ = 0, 8, 16, 24
CB_GA, CB_GB = 32, 40
CB_AQ, CB_AK, CB_AV = 48, 60, 72
MAIN_WIDTH = 84 * LANES

VMEM_LIMIT = 56 * 1024 * 1024


def _cparams(sem):
    return pltpu.CompilerParams(dimension_semantics=sem, vmem_limit_bytes=VMEM_LIMIT)


def _dot(a, b):
    return jnp.dot(a, b, preferred_element_type=F32)


def _dot_nt(a, b):
    return lax.dot_general(a, b, (((1,), (1,)), ((), ())), preferred_element_type=F32)


def _dot_f32(a, b):
    return jnp.dot(a, b, preferred_element_type=F32, precision=lax.Precision.HIGHEST)


def _sigmoid(x):
    return 0.5 * jnp.tanh(0.5 * x) + 0.5


def _silu(x):
    half = 0.5 * x
    return half * (jnp.tanh(half) + 1.0)


def _split_bf16(x):
    hi = x.astype(BF16)
    lo = (x - hi.astype(F32)).astype(BF16)
    return jnp.concatenate([hi, lo], axis=1)


def _inproj_kernel(x_ref, nw_ref, w_ref, ws_ref, o_ref, os_ref, *, tn):
    x = x_ref[...]
    h = (x * lax.rsqrt(jnp.mean(x * x, axis=-1, keepdims=True) + EPS) * nw_ref[...]).astype(BF16)
    os_ref[...] = _dot(h, ws_ref[...])
    for n0 in range(0, w_ref.shape[1], tn):
        o_ref[:, n0:n0 + tn] = _dot(h, w_ref[:, n0:n0 + tn]).astype(o_ref.dtype)


def _in_projection(x2, norm_w, w_main, w_small, *, tm, tn):
    t, d = x2.shape
    n = w_main.shape[1]
    once = pl.Buffered(1)
    return pl.pallas_call(
        functools.partial(_inproj_kernel, tn=tn),
        grid=(t // tm,),
        in_specs=[
            pl.BlockSpec((tm, d), lambda i: (i, 0)),
            pl.BlockSpec((1, d), lambda i: (0, 0)),
            pl.BlockSpec((d, n), lambda i: (0, 0), pipeline_mode=once),
            pl.BlockSpec((d, LANES), lambda i: (0, 0), pipeline_mode=once),
        ],
        out_specs=[
            pl.BlockSpec((tm, n), lambda i: (i, 0)),
            pl.BlockSpec((tm, LANES), lambda i: (i, 0)),
        ],
        out_shape=[
            jax.ShapeDtypeStruct((t, n), BF16),
            jax.ShapeDtypeStruct((t, LANES), F32),
        ],
        compiler_params=_cparams(("parallel",)),
        name="in_projection",
    )(x2, norm_w, w_main, w_small)


def _deltanet_kernel(q_ref, k_ref, v_ref, z_ref, sm_ref, cw_ref, alog_ref, dtb_ref, nw_ref, o_ref,
                     s_scr, tail_scr, u_scr, wq_scr, qk_scr, kdt_scr, egl_scr,
                     pre_scr, gate_scr, gct_scr, sq_scr, *, lb):
    lstep = pl.program_id(1)
    nc = lb // CHUNK
    width = DN_WIDTH
    look = 16

    @pl.when(lstep == 0)
    def _():
        s_scr[...] = jnp.zeros_like(s_scr)
        tail_scr[...] = jnp.zeros_like(tail_scr)
        u_scr[:, 0:CHUNK, :] = jnp.zeros((DN_HEADS, CHUNK, HEAD_DIM), F32)
        wq_scr[0] = jnp.zeros(wq_scr.shape[1:], BF16)
        qk_scr[0] = jnp.zeros(qk_scr.shape[1:], BF16)
        kdt_scr[0] = jnp.zeros(kdt_scr.shape[1:], BF16)
        egl_scr[0] = jnp.zeros(egl_scr.shape[1:], F32)

    ii = lax.broadcasted_iota(jnp.int32, (CHUNK, CHUNK), 0)
    jj = lax.broadcasted_iota(jnp.int32, (CHUNK, CHUNK), 1)
    eye = (ii == jj).astype(F32)
    rowi = lax.broadcasted_iota(jnp.int32, (CHUNK, LANES), 0)

    heads = range(DN_HEADS)
    head_cols = [slice(h * HEAD_DIM, (h + 1) * HEAD_DIM) for h in heads]
    srcs = (q_ref, k_ref, v_ref)

    def anchor(value):
        rows8 = value[:8, :]
        zero = jnp.where(rows8 != rows8, rows8, 0.0)
        if zero.shape[1] < LANES:
            zero = jnp.concatenate([zero] * (LANES // zero.shape[1]), axis=1)
        return zero

    def conv_piece(which, h, c, r0, zero):
        x_ref, cols = srcs[which], head_cols[h]
        prev0 = pl.multiple_of(jnp.maximum(r0 - look, 0), look)
        top = lax.select(c == 0, tail_scr[which, :, cols],
                         x_ref[pl.ds(prev0, look), cols].astype(F32))
        win = jnp.concatenate([top, x_ref[pl.ds(r0, CHUNK), cols].astype(F32)], axis=0)
        w0 = which * width + h * HEAD_DIM
        acc = None
        for j in range(CONV_WIDTH):
            back = CONV_WIDTH - 1 - j
            rows = win if back == 0 else pltpu.roll(win, back, axis=0)
            term = rows[look:, :] * (cw_ref[j:j + 1, w0:w0 + HEAD_DIM] + zero[0:1, :])
            acc = term if acc is None else acc + term
        y = _silu(acc)
        pre_scr[which, :, cols] = y
        if which < 2:
            sq_scr[(which * DN_HEADS + h) * CHUNK:(which * DN_HEADS + h + 1) * CHUNK, :] = y * y

    def gate_piece(r0, zero):
        sm = sm_ref[pl.ds(r0, CHUNK), :]
        xs = sm + (dtb_ref[...] + zero[0:1, :])
        softplus = jnp.maximum(xs, 0.0) + jnp.log(1.0 + jnp.exp(-jnp.abs(xs)))
        gc_all = -jnp.exp(alog_ref[...]) * softplus
        shift = 1
        while shift < CHUNK:
            gc_all = gc_all + jnp.where(rowi >= shift, pltpu.roll(gc_all, shift, axis=0), 0.0)
            shift *= 2
        gate_scr[0] = _sigmoid(sm)
        gate_scr[1] = gc_all
        gct_scr[...] = gc_all.T

    def front_pieces(c):
        r0 = pl.multiple_of(c * CHUNK, CHUNK)
        pieces = [functools.partial(conv_piece, which, h, c, r0)
                  for which in range(3) for h in heads]
        return pieces + [functools.partial(gate_piece, r0)]

    def front_finish():
        inv_norm = lax.rsqrt(
            _dot(_split_bf16(sq_scr[...]), jnp.ones((2 * HEAD_DIM, HEAD_DIM), BF16)) + EPS)
        for h in heads:
            cols = head_cols[h]
            pre_scr[0, :, cols] = pre_scr[0, :, cols] * (
                inv_norm[h * CHUNK:(h + 1) * CHUNK] * (HEAD_DIM ** -0.5))
            pre_scr[1, :, cols] = pre_scr[1, :, cols] * inv_norm[
                (DN_HEADS + h) * CHUNK:(DN_HEADS + h + 1) * CHUNK]

    for piece in front_pieces(0):
        piece(jnp.zeros((8, LANES), F32))
    front_finish()

    def recur_read(cs):
        ss = [s_scr[h] for h in heads]
        rs = [_dot(wq_scr[cs, h], ss[h].astype(BF16)) for h in heads]
        return ss, rs

    def recur_update(cs, ss, rs, live):
        rc0 = pl.multiple_of(cs * CHUNK, CHUNK)
        vbs = []
        for h in heads:
            v_new = u_scr[h, pl.ds(rc0, CHUNK), :] - rs[h][:CHUNK]
            egl = egl_scr[cs, h][0:1, :]
            if live is not None:
                v_new = lax.select(live, v_new, jnp.zeros_like(v_new))
                egl = lax.select(live, egl, jnp.ones_like(egl))
            vbs.append(v_new.astype(BF16))
            s_scr[h] = ss[h] * egl + _dot(kdt_scr[cs, h], vbs[h])
        return [rs[h][CHUNK:] + _dot(qk_scr[cs, h], vbs[h]) for h in heads]

    def recur_finish(cs, outs):
        rc0 = pl.multiple_of(cs * CHUNK, CHUNK)
        for h in heads:
            o = outs[h]
            on = o * lax.rsqrt(jnp.mean(o * o, axis=-1, keepdims=True) + EPS) * nw_ref[...]
            z = z_ref[pl.ds(rc0, CHUNK), head_cols[h]].astype(F32)
            o_ref[pl.ds(rc0, CHUNK), head_cols[h]] = (on * _silu(z)).astype(o_ref.dtype)

    def prep(c, carry):
        r0 = pl.multiple_of(c * CHUNK, CHUNK)
        behind = jnp.maximum(c - 1, 0)
        live = c > 0
        beta_all = gate_scr[0]
        gc_all = gate_scr[1]
        gc_t = gct_scr[...]
        ns, held = [], []
        qs, rhs = [None] * DN_HEADS, [None] * DN_HEADS
        for h in heads:
            cols = head_cols[h]
            q = pre_scr[0, :, cols]
            k = pre_scr[1, :, cols]
            v = pre_scr[2, :, cols]
            bb = jnp.broadcast_to(beta_all[:, h:h + 1], (CHUNK, LANES))
            gcb = jnp.broadcast_to(gc_all[:, DN_HEADS + h:DN_HEADS + h + 1], (CHUNK, LANES))
            g_row = gc_t[DN_HEADS + h:DN_HEADS + h + 1, :]
            decay = jnp.where(ii >= jj, jnp.exp(jnp.minimum(gcb[:, :CHUNK] - g_row, 0.0)), 0.0)
            kb = k * bb
            both = _dot_nt(jnp.concatenate([kb, q], axis=0).astype(BF16), k.astype(BF16))
            ns.append(jnp.where(ii > jj, -both[:CHUNK] * decay, 0.0))
            qk_scr[c, h] = (both[CHUNK:] * decay).astype(BF16)
            held.append((q, k, v, bb, gcb, kb))
            kb_last = kb

        def late_piece(h, zero):
            q, k, v, bb, gcb, kb = held[h]
            gcz = gcb + zero[0:1, :]
            g_last = gcz[CHUNK - 1:CHUNK, :]
            eg = jnp.exp(gcz)
            kdt_scr[c, h] = (k * jnp.exp(g_last - gcz)).T.astype(BF16)
            egl_scr[c, h] = jnp.broadcast_to(jnp.exp(g_last), (8, LANES))
            rhs[h] = jnp.concatenate([v * bb, kb * eg], axis=1).astype(BF16)
            qs[h] = q * eg

        pieces = ([functools.partial(late_piece, h) for h in heads]
                  + front_pieces(jnp.minimum(c + 1, nc - 1)))
        n_slots = 7

        def fill(slot, operand):
            zero = anchor(operand)
            for piece in pieces[slot::n_slots]:
                piece(zero)

        state_in = recur_read(behind)
        fill(0, kb_last)
        ps = []
        for h in heads:
            nb = ns[h].astype(BF16)
            ps.append(_dot(nb, nb))
        fill(1, ns[-1])
        xs_ = [eye + ns[h] for h in heads]
        for level in range(4):
            operand = xs_[-1]
            for h in heads:
                r = _dot(jnp.concatenate([xs_[h], ps[h]], axis=0).astype(BF16), ps[h].astype(BF16))
                xs_[h] = xs_[h] + r[:CHUNK]
                ps[h] = r[CHUNK:]
            if level == 0:
                outs = recur_update(behind, *state_in, live)
            if level == 2:
                recur_finish(behind, outs)
            fill(2 + level, operand)
        operand = xs_[-1]
        for h in heads:
            xs_[h] = xs_[h] + _dot(xs_[h].astype(BF16), ps[h].astype(BF16))
        fill(6, operand)
        for h in heads:
            uw = _dot(xs_[h].astype(BF16), rhs[h])
            u_scr[h, pl.ds(r0, CHUNK), :] = uw[:, :HEAD_DIM]
            wq_scr[c, h] = jnp.concatenate([uw[:, HEAD_DIM:], qs[h]], axis=0).astype(BF16)
        front_finish()
        return carry

    lax.fori_loop(0, nc, prep, 0)
    last = nc - 1
    recur_finish(last, recur_update(last, *recur_read(last), None))

    for which, x_ref in enumerate((q_ref, k_ref, v_ref)):
        tail_scr[which] = x_ref[lb - look:lb, :].astype(F32)


def _deltanet(main, small, conv_w, alog_row, dtb_row, dn_norm_w, *, batch, seq, lb):
    t = batch * seq
    nlb = seq // lb
    nc = lb // CHUNK
    wblk = DN_WIDTH // LANES

    def col(cb):
        return pl.BlockSpec((lb, DN_WIDTH), lambda b, l, cb=cb: (b * nlb + l, cb // wblk))

    row = pl.BlockSpec((1, LANES), lambda b, l: (0, 0))
    return pl.pallas_call(
        functools.partial(_deltanet_kernel, lb=lb),
        grid=(batch, nlb),
        in_specs=[col(CB_DQ), col(CB_DK), col(CB_DV), col(CB_DZ),
                  pl.BlockSpec((lb, LANES), lambda b, l: (b * nlb + l, 0)),
                  pl.BlockSpec(conv_w.shape, lambda b, l: (0, 0)), row, row, row],
        out_specs=pl.BlockSpec((lb, DN_WIDTH), lambda b, l: (b * nlb + l, 0)),
        out_shape=jax.ShapeDtypeStruct((t, DN_WIDTH), BF16),
        scratch_shapes=[
            pltpu.VMEM((DN_HEADS, HEAD_DIM, HEAD_DIM), F32),
            pltpu.VMEM((3, 16, DN_WIDTH), F32),
            pltpu.VMEM((DN_HEADS, lb, HEAD_DIM), F32),
            pltpu.VMEM((nc, DN_HEADS, 2 * CHUNK, HEAD_DIM), BF16),
            pltpu.VMEM((nc, DN_HEADS, CHUNK, CHUNK), BF16),
            pltpu.VMEM((nc, DN_HEADS, HEAD_DIM, CHUNK), BF16),
            pltpu.VMEM((nc, DN_HEADS, 8, LANES), F32),
            pltpu.VMEM((3, CHUNK, DN_WIDTH), F32),
            pltpu.VMEM((2, CHUNK, LANES), F32),
            pltpu.VMEM((LANES, CHUNK), F32),
            pltpu.VMEM((2 * DN_HEADS * CHUNK, HEAD_DIM), F32),
        ],
        compiler_params=_cparams(("parallel", "arbitrary")),
        name="deltanet",
    )(main, main, main, main, small, conv_w, alog_row, dtb_row, dn_norm_w)


def _attention_kernel(q_ref, k_ref, v_ref, tc_ref, ts_ref, o_ref,
                      qr_scr, kr_scr, v_scr, og_scr, lse_scr, *, seq):
    group = pl.program_id(2)
    rows = 256
    half = ROPE_DIM // 2
    lane = lax.broadcasted_iota(jnp.int32, (rows, LANES), 1)
    src_lane = lax.broadcasted_iota(jnp.int32, (LANES, LANES), 0)
    dst_lane = lax.broadcasted_iota(jnp.int32, (LANES, LANES), 1)
    want = jnp.where(dst_lane < half, dst_lane + half, jnp.where(dst_lane < ROPE_DIM, dst_lane - half, -1))
    swap = jnp.where(src_lane == want, 1.0, 0.0).astype(BF16)

    def rope(i, carry):
        r0 = pl.multiple_of(i * rows, rows)
        tc = tc_ref[pl.ds(r0, rows), :]
        ts = ts_ref[pl.ds(r0, rows), :]
        xq = q_ref[pl.ds(r0, rows), :].astype(F32)
        partner_q = jnp.where(lane < half, pltpu.roll(xq, LANES - half, axis=1),
                              pltpu.roll(xq, half, axis=1))
        qr_scr[pl.ds(r0, rows), :] = (xq * tc + partner_q * ts) * (HEAD_DIM ** -0.5)
        xk = k_ref[pl.ds(r0, rows), :]
        kr_scr[pl.ds(r0, rows), :] = xk.astype(F32) * tc + _dot(xk, swap) * ts
        v_scr[pl.ds(r0, rows), :] = v_ref[pl.ds(r0, rows), :].astype(F32)
        return carry

    lax.fori_loop(0, seq // rows, rope, 0, unroll=4)

    ii = lax.broadcasted_iota(jnp.int32, (ATT_BLOCK, 2 * ATT_BLOCK), 0)
    jj = lax.broadcasted_iota(jnp.int32, (ATT_BLOCK, 2 * ATT_BLOCK), 1)
    dist = ii + ATT_BLOCK - jj
    band_mask = jnp.where(dist >= 0, jnp.where(dist <= ATT_BLOCK, 0.0, NEG_BIG), NEG_BIG)
    prev_half = jnp.where(jj < ATT_BLOCK, 1.0, 0.0)
    per_iter = 8

    def run_group(gi, dil):
        nb = seq // (dil * ATT_BLOCK)
        span = dil * ATT_BLOCK

        def rows_of(start):
            if dil == 1:
                return pl.ds(start, ATT_BLOCK)
            return pl.ds(start, ATT_BLOCK, stride=dil)

        def blocks(it, carry):
            where = []
            for u in range(per_iter):
                t = it * per_iter + u
                r = lax.shift_right_logical(t, nb.bit_length() - 1)
                n = jnp.bitwise_and(t, nb - 1)
                start = r + n * span
                where.append((n, start, jnp.maximum(start - span, r)))
            run = min(nb, per_iter)
            kcur = [kr_scr[rows_of(start), :].astype(BF16) for _, start, _ in where]
            vcur = [v_scr[rows_of(start), :].astype(BF16) for _, start, _ in where]

            def prev_of(u, cur, ref):
                if u % run:
                    return cur[u - 1]
                if nb <= per_iter:
                    return cur[u]
                return ref[rows_of(where[u][2]), :].astype(BF16)

            scores = []
            for u, (n, start, prev) in enumerate(where):
                qb = qr_scr[rows_of(start), :].astype(BF16)
                kcat = jnp.concatenate([prev_of(u, kcur, kr_scr), kcur[u]], axis=0)
                s = _dot_nt(qb, kcat)
                no_prev = jnp.where(n > 0, 0.0, NEG_BIG)
                s = s + (band_mask + prev_half * no_prev)
                scores.append(s)
            probs = []
            for s in scores:
                m = jnp.max(s, axis=1, keepdims=True)
                p = jnp.exp(s - m)
                probs.append((m, p, jnp.sum(p, axis=1, keepdims=True)))
            for u, ((n, start, prev), (m, p, den)) in enumerate(zip(where, probs)):
                vcat = jnp.concatenate([prev_of(u, vcur, v_scr), vcur[u]], axis=0)
                acc = _dot(p.astype(BF16), vcat)
                og_scr[gi, rows_of(start), :] = acc / den
                lse_scr[gi, rows_of(start), :] = jnp.broadcast_to(m + jnp.log(den),
                                                                  (ATT_BLOCK, LANES))
            return carry

        lax.fori_loop(0, seq // (ATT_BLOCK * per_iter), blocks, 0)

    for gi, (_, dil) in enumerate(ATT_GROUPS):
        @pl.when(group == gi)
        def _(gi=gi, dil=dil):
            run_group(gi, dil)

    @pl.when(group == len(ATT_GROUPS) - 1)
    def _():
        def merge(i, carry):
            r0 = pl.multiple_of(i * rows, rows)
            ls = [lse_scr[g, pl.ds(r0, rows), :] for g in range(len(ATT_GROUPS))]
            mx = functools.reduce(jnp.maximum, ls)
            es = [jnp.exp(l - mx) for l in ls]
            num = functools.reduce(
                lambda a, b: a + b, [e * og_scr[g, pl.ds(r0, rows), :] for g, e in enumerate(es)])
            den = functools.reduce(lambda a, b: a + b, es)
            o_ref[pl.ds(r0, rows), :] = (num / den).astype(o_ref.dtype)
            return carry

        lax.fori_loop(0, seq // rows, merge, 0)


def _attention(main, rope_cos, rope_sin, *, batch, seq):
    t = batch * seq
    ng = len(ATT_GROUPS)

    def col(cb):
        return pl.BlockSpec((seq, HEAD_DIM),
                            lambda b, h, g, cb=cb: (b, cb + g * ATT_HEADS_PER_GROUP + h))

    tab = pl.BlockSpec((seq, LANES), lambda b, h, g: (0, 0))
    return pl.pallas_call(
        functools.partial(_attention_kernel, seq=seq),
        grid=(batch, ATT_HEADS_PER_GROUP, ng),
        in_specs=[col(CB_AQ), col(CB_AK), col(CB_AV), tab, tab],
        out_specs=pl.BlockSpec((seq, HEAD_DIM), lambda b, h, g: (b, h)),
        out_shape=jax.ShapeDtypeStruct((t, ATT_OUT_WIDTH), BF16),
        scratch_shapes=[
            pltpu.VMEM((seq, HEAD_DIM), F32),
            pltpu.VMEM((seq, HEAD_DIM), F32),
            pltpu.VMEM((seq, HEAD_DIM), F32),
            pltpu.VMEM((ng, seq, HEAD_DIM), F32),
            pltpu.VMEM((ng, seq, LANES), F32),
        ],
        compiler_params=_cparams(("parallel", "parallel", "arbitrary")),
        name="dilated_attention",
    )(main, main, main, rope_cos, rope_sin)


def _rope_tables(seq):
    half = ROPE_DIM // 2
    inv_freq = jnp.power(ROPE_THETA, -jnp.arange(half, dtype=F32) * (2.0 / ROPE_DIM))
    ang = jnp.arange(seq).astype(F32)[:, None] * inv_freq[None, :]
    cos, sin = jnp.cos(ang), jnp.sin(ang)
    rest = HEAD_DIM - ROPE_DIM
    tc = jnp.concatenate([cos, cos, jnp.ones((seq, rest), F32)], axis=1)
    ts = jnp.concatenate([-sin, sin, jnp.zeros((seq, rest), F32)], axis=1)
    return tc, ts


def _merge_ffn_kernel(x_ref, oa_ref, ob_ref, ga_ref, gb_ref, wa_ref, wb_ref, wo_ref,
                      nw_ref, wg_ref, wu_ref, wd_ref, fw_ref, o_ref, *, final_norm, tf):
    ya = _dot(oa_ref[...], wa_ref[...])
    yb = _dot(ob_ref[...], wb_ref[...])
    merged = (_sigmoid(ga_ref[...].astype(F32)) * ya + _sigmoid(gb_ref[...].astype(F32)) * yb)
    x = x_ref[...] + _dot(merged.astype(BF16), wo_ref[...])
    h = (x * lax.rsqrt(jnp.mean(x * x, axis=-1, keepdims=True) + EPS) * nw_ref[...]).astype(BF16)
    d_ff = wd_ref.shape[0]
    tiles = [slice(f0, min(f0 + tf, d_ff)) for f0 in range(0, d_ff, tf)]
    y = x
    pending = None
    for cols in tiles + [None]:
        issued = None if cols is None else (_dot(h, wg_ref[:, cols]), _dot(h, wu_ref[:, cols]))
        if pending is not None:
            pcols, (gate, up) = pending
            y = y + _dot((_silu(gate) * up).astype(BF16), wd_ref[pcols, :])
        pending = (cols, issued)
    if final_norm:
        y = y * lax.rsqrt(jnp.mean(y * y, axis=-1, keepdims=True) + EPS) * fw_ref[...]
    o_ref[...] = y


def _merge_ffn(x2, oa, ob, main, w_a, w_b, w_o, norm_w, w_gate_up, w_down, final_w, *,
               tm, tf, final_norm):
    t, d = x2.shape
    d_ff = w_down.shape[0]
    gate_cb = CB_GA * LANES // d
    once = pl.Buffered(1)

    def weight(shape, col=0):
        return pl.BlockSpec(shape, lambda i, col=col: (0, col), pipeline_mode=once)

    row = pl.BlockSpec((1, d), lambda i: (0, 0))
    return pl.pallas_call(
        functools.partial(_merge_ffn_kernel, final_norm=final_norm, tf=tf),
        grid=(t // tm,),
        in_specs=[
            pl.BlockSpec((tm, d), lambda i: (i, 0)),
            pl.BlockSpec((tm, DN_WIDTH), lambda i: (i, 0)),
            pl.BlockSpec((tm, ATT_OUT_WIDTH), lambda i: (i, 0)),
            pl.BlockSpec((tm, d), lambda i: (i, gate_cb)),
            pl.BlockSpec((tm, d), lambda i: (i, gate_cb + 1)),
            weight(w_a.shape), weight(w_b.shape), weight(w_o.shape),
            row, weight((d, d_ff)), weight((d, d_ff), 1), weight((d_ff, d)), row,
        ],
        out_specs=pl.BlockSpec((tm, d), lambda i: (i, 0)),
        out_shape=jax.ShapeDtypeStruct((t, d), F32),
        compiler_params=_cparams(("parallel",)),
        name="merge_ffn",
    )(x2, oa, ob, main, main, w_a, w_b, w_o, norm_w, w_gate_up, w_gate_up, w_down, final_w)


def _pad_lanes(v, offset):
    out = jnp.zeros((1, LANES), F32)
    return lax.dynamic_update_slice(out, v.reshape(1, -1).astype(F32), (0, offset))


def kernel(x, norm1_w, w_in, conv_w, a_log, dt_bias, dn_norm_w, w_proj_a, w_proj_b, w_out,
           norm2_w, w_gate_up, w_down, final_norm_w):
    batch, seq, d_model = x.shape
    depth = w_in.shape[0]
    t = batch * seq
    assert d_model == 8 * LANES and seq % (ATT_GROUPS[-1][1] * ATT_BLOCK) == 0
    d_ff = w_down.shape[1]

    o_dz_end = 4 * DN_WIDTH
    o_small_end = o_dz_end + 2 * DN_HEADS
    o_att_end = o_small_end + 3 * ATT_WIDTH

    rope_cos, rope_sin = _rope_tables(seq)
    x2 = x.reshape(t, d_model)
    for i in range(depth):
        wi = w_in[i].astype(BF16)
        w_main = jnp.concatenate(
            [wi[:, :o_dz_end], wi[:, o_att_end:], wi[:, o_small_end:o_att_end]], axis=1)
        w_small = jnp.pad(wi[:, o_dz_end:o_small_end], ((0, 0), (0, LANES - 2 * DN_HEADS)))
        main, small = _in_projection(x2, norm1_w[i].reshape(1, d_model), w_main, w_small,
                                     tm=512, tn=6 * MXU_WIDTH)
        oa = _deltanet(main, small, conv_w[i], _pad_lanes(a_log[i], DN_HEADS),
                       _pad_lanes(dt_bias[i], DN_HEADS), dn_norm_w[i].reshape(1, HEAD_DIM),
                       batch=batch, seq=seq, lb=512)
        ob = _attention(main, rope_cos, rope_sin, batch=batch, seq=seq)
        x2 = _merge_ffn(x2, oa, ob, main, w_proj_a[i].astype(BF16), w_proj_b[i].astype(BF16),
                        w_out[i].astype(BF16), norm2_w[i].reshape(1, d_model),
                        w_gate_up[i].astype(BF16), w_down[i].astype(BF16),
                        final_norm_w.reshape(1, d_model), tm=512, tf=3 * MXU_WIDTH,
                        final_norm=(i == depth - 1))
    return x2.reshape(batch, seq, d_model)
```

```python
import functools
import math

import jax
import jax.numpy as jnp
from jax import lax
from jax.experimental import pallas as pl
from jax.experimental.pallas import tpu as pltpu

F32 = jnp.float32
BF16 = jnp.bfloat16

EPS = 1e-6
LANES = 128
MXU_WIDTH = 256
HEAD_DIM = 128
DN_HEADS = 8
DN_WIDTH = DN_HEADS * HEAD_DIM
CONV_WIDTH = 4
CHUNK = 64
ATT_GROUPS = ((128, 1), (512, 4), (2048, 16))
ATT_HEADS_PER_GROUP = 4
ATT_N_HEADS = len(ATT_GROUPS) * ATT_HEADS_PER_GROUP
ATT_WIDTH = ATT_N_HEADS * HEAD_DIM
ATT_OUT_WIDTH = ATT_HEADS_PER_GROUP * HEAD_DIM
ATT_BLOCK = 128
ROPE_THETA = 500000.0
ROPE_DIM = HEAD_DIM // 4
NEG_BIG = -1e30

CB_DQ, CB_DK, CB_DV, CB_DZ = 0, 8, 16, 24
CB_GA, CB_GB = 32, 40
CB_AQ, CB_AK, CB_AV = 48, 60, 72
MAIN_WIDTH = 84 * LANES

VMEM_LIMIT = 56 * 1024 * 1024


def _cparams(sem):
    return pltpu.CompilerParams(dimension_semantics=sem, vmem_limit_bytes=VMEM_LIMIT)


def _dot(a, b):
    return jnp.dot(a, b, preferred_element_type=F32)


def _dot_nt(a, b):
    return lax.dot_general(a, b, (((1,), (1,)), ((), ())), preferred_element_type=F32)


def _dot_f32(a, b):
    return jnp.dot(a, b, preferred_element_type=F32, precision=lax.Precision.HIGHEST)


def _sigmoid(x):
    return 0.5 * jnp.tanh(0.5 * x) + 0.5


def _silu(x):
    half = 0.5 * x
    return half * (jnp.tanh(half) + 1.0)


def _split_bf16(x):
    hi = x.astype(BF16)
    lo = (x - hi.astype(F32)).astype(BF16)
    return jnp.concatenate([hi, lo], axis=1)


SRC_SMALL = 4 * DN_WIDTH
SRC_ATT = SRC_SMALL + 2 * DN_HEADS
SRC_GATES = SRC_ATT + 3 * ATT_WIDTH
PREP_TILE = 4 * LANES
PREP_SHIFT = 2 * DN_HEADS


def _prep_src_block(j):
    per = PREP_TILE // LANES
    n_direct = SRC_SMALL // PREP_TILE
    n_gates = (2 * DN_WIDTH) // PREP_TILE
    gates_at = (SRC_GATES - PREP_SHIFT) // LANES - per * n_direct
    att_at = (SRC_ATT - PREP_SHIFT) // LANES - per * (n_direct + n_gates)
    return per * j + jnp.where(j < n_direct, 0, jnp.where(j < n_direct + n_gates, gates_at, att_at))


def _weight_prep_kernel(*refs):
    *src_refs, small_ref, o_ref, os_ref = refs
    j = pl.program_id(0)
    n_direct = SRC_SMALL // PREP_TILE

    @pl.when(j < n_direct)
    def _():
        for k, ref in enumerate(src_refs[:-1]):
            o_ref[:, k * LANES:(k + 1) * LANES] = ref[...].astype(BF16)

    @pl.when(j >= n_direct)
    def _():
        wide = jnp.concatenate([ref[...] for ref in src_refs], axis=1)
        o_ref[...] = wide[:, PREP_SHIFT:PREP_SHIFT + PREP_TILE].astype(BF16)

    lane = lax.broadcasted_iota(jnp.int32, small_ref.shape, 1)
    os_ref[...] = jnp.where(lane < PREP_SHIFT, small_ref[...], 0.0).astype(BF16)


def _weight_prep(w):
    d = w.shape[0]
    n_src = PREP_TILE // LANES + 1
    last = (w.shape[1] - 1) // LANES

    def src(k):
        return pl.BlockSpec((d, LANES), lambda j, k=k: (0, jnp.minimum(_prep_src_block(j) + k, last)))

    return pl.pallas_call(
        _weight_prep_kernel,
        grid=(MAIN_WIDTH // PREP_TILE,),
        in_specs=[src(k) for k in range(n_src)]
        + [pl.BlockSpec((d, LANES), lambda j: (0, SRC_SMALL // LANES))],
        out_specs=[pl.BlockSpec((d, PREP_TILE), lambda j: (0, j)),
                   pl.BlockSpec((d, LANES), lambda j: (0, 0))],
        out_shape=[jax.ShapeDtypeStruct((d, MAIN_WIDTH), BF16),
                   jax.ShapeDtypeStruct((d, LANES), BF16)],
        compiler_params=_cparams(("arbitrary",)),
        name="weight_prep",
    )(*([w] * (n_src + 1)))


def _inproj_kernel(x_ref, nw_ref, w_ref, ws_ref, o_ref, os_ref, *, tn):
    x = x_ref[...]
    h = (x * lax.rsqrt(jnp.mean(x * x, axis=-1, keepdims=True) + EPS) * nw_ref[...]).astype(BF16)
    os_ref[...] = _dot(h, ws_ref[...])
    for n0 in range(0, w_ref.shape[1], tn):
        o_ref[:, n0:n0 + tn] = _dot(h, w_ref[:, n0:n0 + tn]).astype(o_ref.dtype)


def _in_projection(x2, norm_w, w_main, w_small, *, tm, tn):
    t, d = x2.shape
    n = w_main.shape[1]
    once = pl.Buffered(1)
    return pl.pallas_call(
        functools.partial(_inproj_kernel, tn=tn),
        grid=(t // tm,),
        in_specs=[
            pl.BlockSpec((tm, d), lambda i: (i, 0)),
            pl.BlockSpec((1, d), lambda i: (0, 0)),
            pl.BlockSpec((d, n), lambda i: (0, 0), pipeline_mode=once),
            pl.BlockSpec((d, LANES), lambda i: (0, 0), pipeline_mode=once),
        ],
        out_specs=[
            pl.BlockSpec((tm, n), lambda i: (i, 0)),
            pl.BlockSpec((tm, LANES), lambda i: (i, 0)),
        ],
        out_shape=[
            jax.ShapeDtypeStruct((t, n), BF16),
            jax.ShapeDtypeStruct((t, LANES), F32),
        ],
        compiler_params=_cparams(("parallel",)),
        name="in_projection",
    )(x2, norm_w, w_main, w_small)


def _deltanet_kernel(q_ref, k_ref, v_ref, z_ref, sm_ref, cw_ref, alog_ref, dtb_ref, nw_ref, o_ref,
                     s_scr, tail_scr, u_scr, wq_scr, qk_scr, kdt_scr, egl_scr,
                     pre_scr, gate_scr, gct_scr, sq_scr, *, lb):
    lstep = pl.program_id(1)
    nc = lb // CHUNK
    width = DN_WIDTH
    look = 16

    @pl.when(lstep == 0)
    def _():
        s_scr[...] = jnp.zeros_like(s_scr)
        tail_scr[...] = jnp.zeros_like(tail_scr)
        u_scr[:, 0:CHUNK, :] = jnp.zeros((DN_HEADS, CHUNK, HEAD_DIM), F32)
        wq_scr[0] = jnp.zeros(wq_scr.shape[1:], BF16)
        qk_scr[0] = jnp.zeros(qk_scr.shape[1:], BF16)
        kdt_scr[0] = jnp.zeros(kdt_scr.shape[1:], BF16)
        egl_scr[0] = jnp.zeros(egl_scr.shape[1:], F32)

    ii = lax.broadcasted_iota(jnp.int32, (CHUNK, CHUNK), 0)
    jj = lax.broadcasted_iota(jnp.int32, (CHUNK, CHUNK), 1)
    eye = (ii == jj).astype(F32)
    rowi = lax.broadcasted_iota(jnp.int32, (CHUNK, LANES), 0)

    heads = range(DN_HEADS)
    head_cols = [slice(h * HEAD_DIM, (h + 1) * HEAD_DIM) for h in heads]
    srcs = (q_ref, k_ref, v_ref)

    def anchor(value):
        rows8 = value[:8, :]
        zero = jnp.where(rows8 != rows8, rows8, 0.0)
        if zero.shape[1] < LANES:
            zero = jnp.concatenate([zero] * (LANES // zero.shape[1]), axis=1)
        return zero

    def conv_piece(which, h, c, r0, zero):
        x_ref, cols = srcs[which], head_cols[h]
        prev0 = pl.multiple_of(jnp.maximum(r0 - look, 0), look)
        top = lax.select(c == 0, tail_scr[which, :, cols],
                         x_ref[pl.ds(prev0, look), cols].astype(F32))
        win = jnp.concatenate([top, x_ref[pl.ds(r0, CHUNK), cols].astype(F32)], axis=0)
        w0 = which * width + h * HEAD_DIM
        acc = None
        for j in range(CONV_WIDTH):
            back = CONV_WIDTH - 1 - j
            rows = win if back == 0 else pltpu.roll(win, back, axis=0)
            term = rows[look:, :] * (cw_ref[j:j + 1, w0:w0 + HEAD_DIM] + zero[0:1, :])
            acc = term if acc is None else acc + term
        y = _silu(acc)
        pre_scr[which, :, cols] = y
        if which < 2:
            sq_scr[(which * DN_HEADS + h) * CHUNK:(which * DN_HEADS + h + 1) * CHUNK, :] = y * y

    def gate_piece(r0, zero):
        sm = sm_ref[pl.ds(r0, CHUNK), :]
        xs = sm + (dtb_ref[...] + zero[0:1, :])
        softplus = jnp.maximum(xs, 0.0) + jnp.log(1.0 + jnp.exp(-jnp.abs(xs)))
        gc_all = -jnp.exp(alog_ref[...]) * softplus
        shift = 1
        while shift < CHUNK:
            gc_all = gc_all + jnp.where(rowi >= shift, pltpu.roll(gc_all, shift, axis=0), 0.0)
            shift *= 2
        gate_scr[0] = _sigmoid(sm)
        gate_scr[1] = gc_all
        gct_scr[...] = gc_all.T

    def front_pieces(c):
        r0 = pl.multiple_of(c * CHUNK, CHUNK)
        pieces = [functools.partial(conv_piece, which, h, c, r0)
                  for which in range(3) for h in heads]
        return pieces + [functools.partial(gate_piece, r0)]

    def front_finish():
        inv_norm = lax.rsqrt(
            _dot(_split_bf16(sq_scr[...]), jnp.ones((2 * HEAD_DIM, HEAD_DIM), BF16)) + EPS)
        for h in heads:
            cols = head_cols[h]
            pre_scr[0, :, cols] = pre_scr[0, :, cols] * (
                inv_norm[h * CHUNK:(h + 1) * CHUNK] * (HEAD_DIM ** -0.5))
            pre_scr[1, :, cols] = pre_scr[1, :, cols] * inv_norm[
                (DN_HEADS + h) * CHUNK:(DN_HEADS + h + 1) * CHUNK]

    for piece in front_pieces(0):
        piece(jnp.zeros((8, LANES), F32))
    front_finish()

    def recur_read(cs):
        ss = [s_scr[h] for h in heads]
        rs = [_dot(wq_scr[cs, h], ss[h].astype(BF16)) for h in heads]
        return ss, rs

    def recur_update(cs, ss, rs, live):
        rc0 = pl.multiple_of(cs * CHUNK, CHUNK)
        vbs = []
        for h in heads:
            v_new = u_scr[h, pl.ds(rc0, CHUNK), :] - rs[h][:CHUNK]
            egl = egl_scr[cs, h][0:1, :]
            if live is not None:
                v_new = lax.select(live, v_new, jnp.zeros_like(v_new))
                egl = lax.select(live, egl, jnp.ones_like(egl))
            vbs.append(v_new.astype(BF16))
            s_scr[h] = ss[h] * egl + _dot(kdt_scr[cs, h], vbs[h])
        return [rs[h][CHUNK:] + _dot(qk_scr[cs, h], vbs[h]) for h in heads]

    def recur_finish(cs, outs):
        rc0 = pl.multiple_of(cs * CHUNK, CHUNK)
        for h in heads:
            o = outs[h]
            on = o * lax.rsqrt(jnp.mean(o * o, axis=-1, keepdims=True) + EPS) * nw_ref[...]
            z = z_ref[pl.ds(rc0, CHUNK), head_cols[h]].astype(F32)
            o_ref[pl.ds(rc0, CHUNK), head_cols[h]] = (on * _silu(z)).astype(o_ref.dtype)

    def prep(c, carry):
        r0 = pl.multiple_of(c * CHUNK, CHUNK)
        behind = jnp.maximum(c - 1, 0)
        live = c > 0
        beta_all = gate_scr[0]
        gc_all = gate_scr[1]
        gc_t = gct_scr[...]
        ns, held = [], []
        qs, rhs = [None] * DN_HEADS, [None] * DN_HEADS
        for h in heads:
            cols = head_cols[h]
            q = pre_scr[0, :, cols]
            k = pre_scr[1, :, cols]
            v = pre_scr[2, :, cols]
            bb = jnp.broadcast_to(beta_all[:, h:h + 1], (CHUNK, LANES))
            gcb = jnp.broadcast_to(gc_all[:, DN_HEADS + h:DN_HEADS + h + 1], (CHUNK, LANES))
            g_row = gc_t[DN_HEADS + h:DN_HEADS + h + 1, :]
            decay = jnp.where(ii >= jj, jnp.exp(jnp.minimum(gcb[:, :CHUNK] - g_row, 0.0)), 0.0)
            kb = k * bb
            both = _dot_nt(jnp.concatenate([kb, q], axis=0).astype(BF16), k.astype(BF16))
            ns.append(jnp.where(ii > jj, -both[:CHUNK] * decay, 0.0))
            qk_scr[c, h] = (both[CHUNK:] * decay).astype(BF16)
            held.append((q, k, v, bb, gcb, kb))
            kb_last = kb

        def late_piece(h, zero):
            q, k, v, bb, gcb, kb = held[h]
            gcz = gcb + zero[0:1, :]
            g_last = gcz[CHUNK - 1:CHUNK, :]
            eg = jnp.exp(gcz)
            kdt_scr[c, h] = (k * jnp.exp(g_last - gcz)).T.astype(BF16)
            egl_scr[c, h] = jnp.broadcast_to(jnp.exp(g_last), (8, LANES))
            rhs[h] = jnp.concatenate([v * bb, kb * eg], axis=1).astype(BF16)
            qs[h] = q * eg

        pieces = ([functools.partial(late_piece, h) for h in heads]
                  + front_pieces(jnp.minimum(c + 1, nc - 1)))
        n_slots = 7

        def fill(slot, operand):
            zero = anchor(operand)
            for piece in pieces[slot::n_slots]:
                piece(zero)

        state_in = recur_read(behind)
        fill(0, kb_last)
        ps = []
        for h in heads:
            nb = ns[h].astype(BF16)
            ps.append(_dot(nb, nb))
        fill(1, ns[-1])
        xs_ = [eye + ns[h] for h in heads]
        for level in range(4):
            operand = xs_[-1]
            for h in heads:
                r = _dot(jnp.concatenate([xs_[h], ps[h]], axis=0).astype(BF16), ps[h].astype(BF16))
                xs_[h] = xs_[h] + r[:CHUNK]
                ps[h] = r[CHUNK:]
            if level == 0:
                outs = recur_update(behind, *state_in, live)
            if level == 2:
                recur_finish(behind, outs)
            fill(2 + level, operand)
        operand = xs_[-1]
        for h in heads:
            xs_[h] = xs_[h] + _dot(xs_[h].astype(BF16), ps[h].astype(BF16))
        fill(6, operand)
        for h in heads:
            uw = _dot(xs_[h].astype(BF16), rhs[h])
            u_scr[h, pl.ds(r0, CHUNK), :] = uw[:, :HEAD_DIM]
            wq_scr[c, h] = jnp.concatenate([uw[:, HEAD_DIM:], qs[h]], axis=0).astype(BF16)
        front_finish()
        return carry

    lax.fori_loop(0, nc, prep, 0)
    last = nc - 1
    recur_finish(last, recur_update(last, *recur_read(last), None))

    for which, x_ref in enumerate((q_ref, k_ref, v_ref)):
        tail_scr[which] = x_ref[lb - look:lb, :].astype(F32)


def _deltanet(main, small, conv_w, alog_row, dtb_row, dn_norm_w, *, batch, seq, lb):
    t = batch * seq
    nlb = seq // lb
    nc = lb // CHUNK
    wblk = DN_WIDTH // LANES

    def col(cb):
        return pl.BlockSpec((lb, DN_WIDTH), lambda b, l, cb=cb: (b * nlb + l, cb // wblk))

    row = pl.BlockSpec((1, LANES), lambda b, l: (0, 0))
    return pl.pallas_call(
        functools.partial(_deltanet_kernel, lb=lb),
        grid=(batch, nlb),
        in_specs=[col(CB_DQ), col(CB_DK), col(CB_DV), col(CB_DZ),
                  pl.BlockSpec((lb, LANES), lambda b, l: (b * nlb + l, 0)),
                  pl.BlockSpec(conv_w.shape, lambda b, l: (0, 0)), row, row, row],
        out_specs=pl.BlockSpec((lb, DN_WIDTH), lambda b, l: (b * nlb + l, 0)),
        out_shape=jax.ShapeDtypeStruct((t, DN_WIDTH), BF16),
        scratch_shapes=[
            pltpu.VMEM((DN_HEADS, HEAD_DIM, HEAD_DIM), F32),
            pltpu.VMEM((3, 16, DN_WIDTH), F32),
            pltpu.VMEM((DN_HEADS, lb, HEAD_DIM), F32),
            pltpu.VMEM((nc, DN_HEADS, 2 * CHUNK, HEAD_DIM), BF16),
            pltpu.VMEM((nc, DN_HEADS, CHUNK, CHUNK), BF16),
            pltpu.VMEM((nc, DN_HEADS, HEAD_DIM, CHUNK), BF16),
            pltpu.VMEM((nc, DN_HEADS, 8, LANES), F32),
            pltpu.VMEM((3, CHUNK, DN_WIDTH), F32),
            pltpu.VMEM((2, CHUNK, LANES), F32),
            pltpu.VMEM((LANES, CHUNK), F32),
            pltpu.VMEM((2 * DN_HEADS * CHUNK, HEAD_DIM), F32),
        ],
        compiler_params=_cparams(("parallel", "arbitrary")),
        name="deltanet",
    )(main, main, main, main, small, conv_w, alog_row, dtb_row, dn_norm_w)


def _attention_kernel(q_ref, k_ref, v_ref, tc_ref, ts_ref, o_ref,
                      qr_scr, kr_scr, v_scr, og_scr, lse_scr, *, seq):
    group = pl.program_id(2)
    rows = 256
    half = ROPE_DIM // 2
    lane = lax.broadcasted_iota(jnp.int32, (rows, LANES), 1)
    src_lane = lax.broadcasted_iota(jnp.int32, (LANES, LANES), 0)
    dst_lane = lax.broadcasted_iota(jnp.int32, (LANES, LANES), 1)
    want = jnp.where(dst_lane < half, dst_lane + half, jnp.where(dst_lane < ROPE_DIM, dst_lane - half, -1))
    swap = jnp.where(src_lane == want, 1.0, 0.0).astype(BF16)

    def rope(i, carry):
        r0 = pl.multiple_of(i * rows, rows)
        tc = tc_ref[pl.ds(r0, rows), :]
        ts = ts_ref[pl.ds(r0, rows), :]
        xq = q_ref[pl.ds(r0, rows), :].astype(F32)
        partner_q = jnp.where(lane < half, pltpu.roll(xq, LANES - half, axis=1),
                              pltpu.roll(xq, half, axis=1))
        qr_scr[pl.ds(r0, rows), :] = (xq * tc + partner_q * ts) * (HEAD_DIM ** -0.5)
        xk = k_ref[pl.ds(r0, rows), :]
        kr_scr[pl.ds(r0, rows), :] = xk.astype(F32) * tc + _dot(xk, swap) * ts
        v_scr[pl.ds(r0, rows), :] = v_ref[pl.ds(r0, rows), :].astype(F32)
        return carry

    lax.fori_loop(0, seq // rows, rope, 0, unroll=4)

    ii = lax.broadcasted_iota(jnp.int32, (ATT_BLOCK, 2 * ATT_BLOCK), 0)
    jj = lax.broadcasted_iota(jnp.int32, (ATT_BLOCK, 2 * ATT_BLOCK), 1)
    dist = ii + ATT_BLOCK - jj
    band_mask = jnp.where(dist >= 0, jnp.where(dist <= ATT_BLOCK, 0.0, NEG_BIG), NEG_BIG)
    prev_half = jnp.where(jj < ATT_BLOCK, 1.0, 0.0)
    per_iter = 8

    def run_group(gi, dil):
        nb = seq // (dil * ATT_BLOCK)
        span = dil * ATT_BLOCK

        def rows_of(start):
            if dil == 1:
                return pl.ds(start, ATT_BLOCK)
            return pl.ds(start, ATT_BLOCK, stride=dil)

        def blocks(it, carry):
            where = []
            for u in range(per_iter):
                t = it * per_iter + u
                r = lax.shift_right_logical(t, nb.bit_length() - 1)
                n = jnp.bitwise_and(t, nb - 1)
                start = r + n * span
                where.append((n, start, jnp.maximum(start - span, r)))
            run = min(nb, per_iter)
            kcur = [kr_scr[rows_of(start), :].astype(BF16) for _, start, _ in where]
            vcur = [v_scr[rows_of(start), :].astype(BF16) for _, start, _ in where]

            def prev_of(u, cur, ref):
                if u % run:
                    return cur[u - 1]
                if nb <= per_iter:
                    return cur[u]
                return ref[rows_of(where[u][2]), :].astype(BF16)

            scores = []
            for u, (n, start, prev) in enumerate(where):
                qb = qr_scr[rows_of(start), :].astype(BF16)
                kcat = jnp.concatenate([prev_of(u, kcur, kr_scr), kcur[u]], axis=0)
                s = _dot_nt(qb, kcat)
                no_prev = jnp.where(n > 0, 0.0, NEG_BIG)
                s = s + (band_mask + prev_half * no_prev)
                scores.append(s)
            probs = []
            for s in scores:
                m = jnp.max(s, axis=1, keepdims=True)
                p = jnp.exp(s - m)
                probs.append((m, p, jnp.sum(p, axis=1, keepdims=True)))
            for u, ((n, start, prev), (m, p, den)) in enumerate(zip(where, probs)):
                vcat = jnp.concatenate([prev_of(u, vcur, v_scr), vcur[u]], axis=0)
                acc = _dot(p.astype(BF16), vcat)
                og_scr[gi, rows_of(start), :] = acc / den
                lse_scr[gi, rows_of(start), :] = jnp.broadcast_to(m + jnp.log(den),
                                                                  (ATT_BLOCK, LANES))
            return carry

        lax.fori_loop(0, seq // (ATT_BLOCK * per_iter), blocks, 0)

    for gi, (_, dil) in enumerate(ATT_GROUPS):
        @pl.when(group == gi)
        def _(gi=gi, dil=dil):
            run_group(gi, dil)

    @pl.when(group == len(ATT_GROUPS) - 1)
    def _():
        def merge(i, carry):
            r0 = pl.multiple_of(i * rows, rows)
            ls = [lse_scr[g, pl.ds(r0, rows), :] for g in range(len(ATT_GROUPS))]
            mx = functools.reduce(jnp.maximum, ls)
            es = [jnp.exp(l - mx) for l in ls]
            num = functools.reduce(
                lambda a, b: a + b, [e * og_scr[g, pl.ds(r0, rows), :] for g, e in enumerate(es)])
            den = functools.reduce(lambda a, b: a + b, es)
            o_ref[pl.ds(r0, rows), :] = (num / den).astype(o_ref.dtype)
            return carry

        lax.fori_loop(0, seq // rows, merge, 0)


def _attention(main, rope_cos, rope_sin, *, batch, seq):
    t = batch * seq
    ng = len(ATT_GROUPS)

    def col(cb):
        return pl.BlockSpec((seq, HEAD_DIM),
                            lambda b, h, g, cb=cb: (b, cb + g * ATT_HEADS_PER_GROUP + h))

    tab = pl.BlockSpec((seq, LANES), lambda b, h, g: (0, 0))
    return pl.pallas_call(
        functools.partial(_attention_kernel, seq=seq),
        grid=(batch, ATT_HEADS_PER_GROUP, ng),
        in_specs=[col(CB_AQ), col(CB_AK), col(CB_AV), tab, tab],
        out_specs=pl.BlockSpec((seq, HEAD_DIM), lambda b, h, g: (b, h)),
        out_shape=jax.ShapeDtypeStruct((t, ATT_OUT_WIDTH), BF16),
        scratch_shapes=[
            pltpu.VMEM((seq, HEAD_DIM), F32),
            pltpu.VMEM((seq, HEAD_DIM), F32),
            pltpu.VMEM((seq, HEAD_DIM), F32),
            pltpu.VMEM((ng, seq, HEAD_DIM), F32),
            pltpu.VMEM((ng, seq, LANES), F32),
        ],
        compiler_params=_cparams(("parallel", "parallel", "arbitrary")),
        name="dilated_attention",
    )(main, main, main, rope_cos, rope_sin)


def _rope_tables(seq):
    half = ROPE_DIM // 2
    inv_freq = jnp.power(ROPE_THETA, -jnp.arange(half, dtype=F32) * (2.0 / ROPE_DIM))
    ang = jnp.arange(seq).astype(F32)[:, None] * inv_freq[None, :]
    cos, sin = jnp.cos(ang), jnp.sin(ang)
    rest = HEAD_DIM - ROPE_DIM
    tc = jnp.concatenate([cos, cos, jnp.ones((seq, rest), F32)], axis=1)
    ts = jnp.concatenate([-sin, sin, jnp.zeros((seq, rest), F32)], axis=1)
    return tc, ts


def _merge_ffn_kernel(x_ref, oa_ref, ob_ref, ga_ref, gb_ref, wa_ref, wb_ref, wo_ref,
                      nw_ref, wg_ref, wu_ref, wd_ref, fw_ref, o_ref, *, final_norm, tf):
    ya = _dot(oa_ref[...], wa_ref[...])
    yb = _dot(ob_ref[...], wb_ref[...])
    merged = (_sigmoid(ga_ref[...].astype(F32)) * ya + _sigmoid(gb_ref[...].astype(F32)) * yb)
    x = x_ref[...] + _dot(merged.astype(BF16), wo_ref[...])
    h = (x * lax.rsqrt(jnp.mean(x * x, axis=-1, keepdims=True) + EPS) * nw_ref[...]).astype(BF16)
    d_ff = wd_ref.shape[0]
    tiles = [slice(f0, min(f0 + tf, d_ff)) for f0 in range(0, d_ff, tf)]
    y = x
    pending = None
    for cols in tiles + [None]:
        issued = None if cols is None else (_dot(h, wg_ref[:, cols]), _dot(h, wu_ref[:, cols]))
        if pending is not None:
            pcols, (gate, up) = pending
            y = y + _dot((_silu(gate) * up).astype(BF16), wd_ref[pcols, :])
        pending = (cols, issued)
    if final_norm:
        y = y * lax.rsqrt(jnp.mean(y * y, axis=-1, keepdims=True) + EPS) * fw_ref[...]
    o_ref[...] = y


def _merge_ffn(x2, oa, ob, main, w_a, w_b, w_o, norm_w, w_gate_up, w_down, final_w, *,
               tm, tf, final_norm):
    t, d = x2.shape
    d_ff = w_down.shape[0]
    gate_cb = CB_GA * LANES // d
    once = pl.Buffered(1)

    def weight(shape, col=0):
        return pl.BlockSpec(shape, lambda i, col=col: (0, col), pipeline_mode=once)

    row = pl.BlockSpec((1, d), lambda i: (0, 0))
    return pl.pallas_call(
        functools.partial(_merge_ffn_kernel, final_norm=final_norm, tf=tf),
        grid=(t // tm,),
        in_specs=[
            pl.BlockSpec((tm, d), lambda i: (i, 0)),
            pl.BlockSpec((tm, DN_WIDTH), lambda i: (i, 0)),
            pl.BlockSpec((tm, ATT_OUT_WIDTH), lambda i: (i, 0)),
            pl.BlockSpec((tm, d), lambda i: (i, gate_cb)),
            pl.BlockSpec((tm, d), lambda i: (i, gate_cb + 1)),
            weight(w_a.shape), weight(w_b.shape), weight(w_o.shape),
            row, weight((d, d_ff)), weight((d, d_ff), 1), weight((d_ff, d)), row,
        ],
        out_specs=pl.BlockSpec((tm, d), lambda i: (i, 0)),
        out_shape=jax.ShapeDtypeStruct((t, d), F32),
        compiler_params=_cparams(("parallel",)),
        name="merge_ffn",
    )(x2, oa, ob, main, main, w_a, w_b, w_o, norm_w, w_gate_up, w_gate_up, w_down, final_w)


def _pad_lanes(v, offset):
    out = jnp.zeros((1, LANES), F32)
    return lax.dynamic_update_slice(out, v.reshape(1, -1).astype(F32), (0, offset))


def kernel(x, norm1_w, w_in, conv_w, a_log, dt_bias, dn_norm_w, w_proj_a, w_proj_b, w_out,
           norm2_w, w_gate_up, w_down, final_norm_w):
    batch, seq, d_model = x.shape
    depth = w_in.shape[0]
    t = batch * seq
    assert d_model == 8 * LANES and seq % (ATT_GROUPS[-1][1] * ATT_BLOCK) == 0
    assert w_in.shape[2] == SRC_GATES + 2 * d_model

    rope_cos, rope_sin = _rope_tables(seq)
    x2 = x.reshape(t, d_model)
    for i in range(depth):
        w_main, w_small = _weight_prep(w_in[i])
        main, small = _in_projection(x2, norm1_w[i].reshape(1, d_model), w_main, w_small,
                                     tm=512, tn=6 * MXU_WIDTH)
        oa = _deltanet(main, small, conv_w[i], _pad_lanes(a_log[i], DN_HEADS),
                       _pad_lanes(dt_bias[i], DN_HEADS), dn_norm_w[i].reshape(1, HEAD_DIM),
                       batch=batch, seq=seq, lb=1024)
        ob = _attention(main, rope_cos, rope_sin, batch=batch, seq=seq)
        x2 = _merge_ffn(x2, oa, ob, main, w_proj_a[i].astype(BF16), w_proj_b[i].astype(BF16),
                        w_out[i].astype(BF16), norm2_w[i].reshape(1, d_model),
                        w_gate_up[i].astype(BF16), w_down[i].astype(BF16),
                        final_norm_w.reshape(1, d_model), tm=512, tf=3 * MXU_WIDTH,
                        final_norm=(i == depth - 1))
    return x2.reshape(batch, seq, d_model)
```

```python
import functools

import jax
import numpy as np
import jax.numpy as jnp
from jax import lax
from jax.experimental import pallas as pl
from jax.experimental.pallas import tpu as pltpu

F32 = jnp.float32
BF16 = jnp.bfloat16

EPS = 1e-6
LANES = 128
MXU_WIDTH = 256
HEAD_DIM = 128
DN_HEADS = 8
DN_WIDTH = DN_HEADS * HEAD_DIM
CONV_WIDTH = 4
CHUNK = 64
ATT_GROUPS = ((128, 1), (512, 4), (2048, 16))
ATT_HEADS_PER_GROUP = 4
ATT_N_HEADS = len(ATT_GROUPS) * ATT_HEADS_PER_GROUP
ATT_WIDTH = ATT_N_HEADS * HEAD_DIM
ATT_OUT_WIDTH = ATT_HEADS_PER_GROUP * HEAD_DIM
ATT_BLOCK = 128
ROPE_THETA = 500000.0
ROPE_DIM = HEAD_DIM // 4
NEG_BIG = -1e30

CB_DQ, CB_DK, CB_DV, CB_DZ = 0, 8, 16, 24
CB_GA, CB_GB = 32, 40
CB_AQ, CB_AK, CB_AV = 48, 60, 72
MAIN_WIDTH = 84 * LANES

VMEM_LIMIT = 56 * 1024 * 1024


def _cparams(sem):
    return pltpu.CompilerParams(dimension_semantics=sem, vmem_limit_bytes=VMEM_LIMIT)


def _dot(a, b):
    return jnp.dot(a, b, preferred_element_type=F32)


def _dot_nt(a, b):
    return lax.dot_general(a, b, (((1,), (1,)), ((), ())), preferred_element_type=F32)


def _dot_f32(a, b):
    return jnp.dot(a, b, preferred_element_type=F32, precision=lax.Precision.HIGHEST)


def _sigmoid(x):
    return 0.5 * jnp.tanh(0.5 * x) + 0.5


def _silu(x):
    half = 0.5 * x
    return half * (jnp.tanh(half) + 1.0)


def _split_bf16(x):
    hi = x.astype(BF16)
    lo = (x - hi.astype(F32)).astype(BF16)
    return jnp.concatenate([hi, lo], axis=1)


SRC_SMALL = 4 * DN_WIDTH
SRC_ATT = SRC_SMALL + 2 * DN_HEADS
SRC_GATES = SRC_ATT + 3 * ATT_WIDTH


def _projection_tiles(tn):
    groups = ((0, 0, SRC_SMALL),
              (CB_GA * LANES, SRC_GATES, 2 * DN_WIDTH),
              (CB_AQ * LANES, SRC_ATT, 3 * ATT_WIDTH))
    return [(dst + off, src + off, min(tn, width - off))
            for dst, src, width in groups for off in range(0, width, tn)]


def _inproj_kernel(x_ref, nw_ref, wt_ref, o_ref, os_ref, *, tiles):
    x = x_ref[...]
    h = (x * lax.rsqrt(jnp.mean(x * x, axis=-1, keepdims=True) + EPS) * nw_ref[...]).astype(BF16)
    os_ref[...] = _dot_nt(h, wt_ref[SRC_SMALL:SRC_SMALL + LANES, :])
    for dst, src, width in tiles:
        o_ref[:, dst:dst + width] = _dot_nt(h, wt_ref[src:src + width, :]).astype(o_ref.dtype)


def _in_projection(x2, norm_w, w_t, *, tm, tn):
    t, d = x2.shape
    once = pl.Buffered(1)
    return pl.pallas_call(
        functools.partial(_inproj_kernel, tiles=_projection_tiles(tn)),
        grid=(t // tm,),
        in_specs=[
            pl.BlockSpec((tm, d), lambda i: (i, 0)),
            pl.BlockSpec((1, d), lambda i: (0, 0)),
            pl.BlockSpec(w_t.shape, lambda i: (0, 0), pipeline_mode=once),
        ],
        out_specs=[
            pl.BlockSpec((tm, MAIN_WIDTH), lambda i: (i, 0)),
            pl.BlockSpec((tm, LANES), lambda i: (i, 0)),
        ],
        out_shape=[
            jax.ShapeDtypeStruct((t, MAIN_WIDTH), BF16),
            jax.ShapeDtypeStruct((t, LANES), F32),
        ],
        compiler_params=_cparams(("parallel",)),
        name="in_projection",
    )(x2, norm_w, w_t)


def _deltanet_kernel(q_ref, k_ref, v_ref, z_ref, sm_ref, cw_ref, alog_ref, dtb_ref, nw_ref, o_ref,
                     s_scr, tail_scr, u_scr, wq_scr, qk_scr, kdt_scr, egl_scr,
                     pre_scr, gate_scr, gct_scr, sq_scr, *, lb):
    lstep = pl.program_id(1)
    nc = lb // CHUNK
    width = DN_WIDTH
    look = 16

    @pl.when(lstep == 0)
    def _():
        s_scr[...] = jnp.zeros_like(s_scr)
        tail_scr[...] = jnp.zeros_like(tail_scr)
        u_scr[:, 0:CHUNK, :] = jnp.zeros((DN_HEADS, CHUNK, HEAD_DIM), F32)
        wq_scr[0] = jnp.zeros(wq_scr.shape[1:], BF16)
        qk_scr[0] = jnp.zeros(qk_scr.shape[1:], BF16)
        kdt_scr[0] = jnp.zeros(kdt_scr.shape[1:], BF16)
        egl_scr[0] = jnp.zeros(egl_scr.shape[1:], F32)

    ii = lax.broadcasted_iota(jnp.int32, (CHUNK, CHUNK), 0)
    jj = lax.broadcasted_iota(jnp.int32, (CHUNK, CHUNK), 1)
    eye = (ii == jj).astype(F32)
    rowi = lax.broadcasted_iota(jnp.int32, (CHUNK, LANES), 0)

    heads = range(DN_HEADS)
    head_cols = [slice(h * HEAD_DIM, (h + 1) * HEAD_DIM) for h in heads]
    srcs = (q_ref, k_ref, v_ref)

    def anchor(value):
        rows8 = value[:8, :]
        zero = jnp.where(rows8 != rows8, rows8, 0.0)
        if zero.shape[1] < LANES:
            zero = jnp.concatenate([zero] * (LANES // zero.shape[1]), axis=1)
        return zero

    def conv_piece(which, h, c, r0, zero):
        x_ref, cols = srcs[which], head_cols[h]
        prev0 = pl.multiple_of(jnp.maximum(r0 - look, 0), look)
        top = lax.select(c == 0, tail_scr[which, :, cols],
                         x_ref[pl.ds(prev0, look), cols].astype(F32))
        win = jnp.concatenate([top, x_ref[pl.ds(r0, CHUNK), cols].astype(F32)], axis=0)
        w0 = which * width + h * HEAD_DIM
        acc = None
        for j in range(CONV_WIDTH):
            back = CONV_WIDTH - 1 - j
            rows = win if back == 0 else pltpu.roll(win, back, axis=0)
            term = rows[look:, :] * (cw_ref[j:j + 1, w0:w0 + HEAD_DIM] + zero[0:1, :])
            acc = term if acc is None else acc + term
        y = _silu(acc)
        pre_scr[which, :, cols] = y
        if which < 2:
            sq_scr[(which * DN_HEADS + h) * CHUNK:(which * DN_HEADS + h + 1) * CHUNK, :] = y * y

    def gate_piece(r0, zero):
        sm = sm_ref[pl.ds(r0, CHUNK), :]
        xs = sm + (dtb_ref[...] + zero[0:1, :])
        softplus = jnp.maximum(xs, 0.0) + jnp.log(1.0 + jnp.exp(-jnp.abs(xs)))
        gc_all = -jnp.exp(alog_ref[...]) * softplus
        shift = 1
        while shift < CHUNK:
            gc_all = gc_all + jnp.where(rowi >= shift, pltpu.roll(gc_all, shift, axis=0), 0.0)
            shift *= 2
        gate_scr[0] = _sigmoid(sm)
        gate_scr[1] = gc_all
        gct_scr[...] = gc_all.T

    def front_pieces(c):
        r0 = pl.multiple_of(c * CHUNK, CHUNK)
        pieces = [functools.partial(conv_piece, which, h, c, r0)
                  for which in range(3) for h in heads]
        return pieces + [functools.partial(gate_piece, r0)]

    def front_finish():
        inv_norm = lax.rsqrt(
            _dot(_split_bf16(sq_scr[...]), jnp.ones((2 * HEAD_DIM, HEAD_DIM), BF16)) + EPS)
        for h in heads:
            cols = head_cols[h]
            pre_scr[0, :, cols] = pre_scr[0, :, cols] * (
                inv_norm[h * CHUNK:(h + 1) * CHUNK] * (HEAD_DIM ** -0.5))
            pre_scr[1, :, cols] = pre_scr[1, :, cols] * inv_norm[
                (DN_HEADS + h) * CHUNK:(DN_HEADS + h + 1) * CHUNK]

    for piece in front_pieces(0):
        piece(jnp.zeros((8, LANES), F32))
    front_finish()

    def recur_read(cs):
        ss = [s_scr[h] for h in heads]
        rs = [_dot(wq_scr[cs, h], ss[h].astype(BF16)) for h in heads]
        return ss, rs

    def recur_update(cs, ss, rs, live):
        rc0 = pl.multiple_of(cs * CHUNK, CHUNK)
        vbs = []
        for h in heads:
            v_new = u_scr[h, pl.ds(rc0, CHUNK), :] - rs[h][:CHUNK]
            egl = egl_scr[cs, h][0:1, :]
            if live is not None:
                v_new = lax.select(live, v_new, jnp.zeros_like(v_new))
                egl = lax.select(live, egl, jnp.ones_like(egl))
            vbs.append(v_new.astype(BF16))
            s_scr[h] = ss[h] * egl + _dot(kdt_scr[cs, h], vbs[h])
        return [rs[h][CHUNK:] + _dot(qk_scr[cs, h], vbs[h]) for h in heads]

    def recur_finish(cs, outs):
        rc0 = pl.multiple_of(cs * CHUNK, CHUNK)
        for h in heads:
            o = outs[h]
            on = o * lax.rsqrt(jnp.mean(o * o, axis=-1, keepdims=True) + EPS) * nw_ref[...]
            z = z_ref[pl.ds(rc0, CHUNK), head_cols[h]].astype(F32)
            o_ref[pl.ds(rc0, CHUNK), head_cols[h]] = (on * _silu(z)).astype(o_ref.dtype)

    def prep(c, carry):
        r0 = pl.multiple_of(c * CHUNK, CHUNK)
        behind = jnp.maximum(c - 1, 0)
        live = c > 0
        beta_all = gate_scr[0]
        gc_all = gate_scr[1]
        gc_t = gct_scr[...]
        ns, held = [], []
        qs, rhs = [None] * DN_HEADS, [None] * DN_HEADS
        for h in heads:
            cols = head_cols[h]
            q = pre_scr[0, :, cols]
            k = pre_scr[1, :, cols]
            v = pre_scr[2, :, cols]
            bb = jnp.broadcast_to(beta_all[:, h:h + 1], (CHUNK, LANES))
            gcb = jnp.broadcast_to(gc_all[:, DN_HEADS + h:DN_HEADS + h + 1], (CHUNK, LANES))
            g_row = gc_t[DN_HEADS + h:DN_HEADS + h + 1, :]
            decay = jnp.where(ii >= jj, jnp.exp(jnp.minimum(gcb[:, :CHUNK] - g_row, 0.0)), 0.0)
            kb = k * bb
            both = _dot_nt(jnp.concatenate([kb, q], axis=0).astype(BF16), k.astype(BF16))
            ns.append(jnp.where(ii > jj, -both[:CHUNK] * decay, 0.0))
            qk_scr[c, h] = (both[CHUNK:] * decay).astype(BF16)
            held.append((q, k, v, bb, gcb, kb))
            kb_last = kb

        def late_piece(h, zero):
            q, k, v, bb, gcb, kb = held[h]
            gcz = gcb + zero[0:1, :]
            g_last = gcz[CHUNK - 1:CHUNK, :]
            eg = jnp.exp(gcz)
            kdt_scr[c, h] = (k * jnp.exp(g_last - gcz)).T.astype(BF16)
            egl_scr[c, h] = jnp.broadcast_to(jnp.exp(g_last), (8, LANES))
            rhs[h] = jnp.concatenate([v * bb, kb * eg], axis=1).astype(BF16)
            qs[h] = q * eg

        pieces = ([functools.partial(late_piece, h) for h in heads]
                  + front_pieces(jnp.minimum(c + 1, nc - 1)))
        n_slots = 7

        def fill(slot, operand):
            zero = anchor(operand)
            for piece in pieces[slot::n_slots]:
                piece(zero)

        state_in = recur_read(behind)
        fill(0, kb_last)
        ps = []
        for h in heads:
            nb = ns[h].astype(BF16)
            ps.append(_dot(nb, nb))
        fill(1, ns[-1])
        xs_ = [eye + ns[h] for h in heads]
        for level in range(4):
            operand = xs_[-1]
            for h in heads:
                r = _dot(jnp.concatenate([xs_[h], ps[h]], axis=0).astype(BF16), ps[h].astype(BF16))
                xs_[h] = xs_[h] + r[:CHUNK]
                ps[h] = r[CHUNK:]
            if level == 0:
                outs = recur_update(behind, *state_in, live)
            if level == 2:
                recur_finish(behind, outs)
            fill(2 + level, operand)
        operand = xs_[-1]
        for h in heads:
            xs_[h] = xs_[h] + _dot(xs_[h].astype(BF16), ps[h].astype(BF16))
        fill(6, operand)
        for h in heads:
            uw = _dot(xs_[h].astype(BF16), rhs[h])
            u_scr[h, pl.ds(r0, CHUNK), :] = uw[:, :HEAD_DIM]
            wq_scr[c, h] = jnp.concatenate([uw[:, HEAD_DIM:], qs[h]], axis=0).astype(BF16)
        front_finish()
        return carry

    lax.fori_loop(0, nc, prep, 0)
    last = nc - 1
    recur_finish(last, recur_update(last, *recur_read(last), None))

    for which, x_ref in enumerate((q_ref, k_ref, v_ref)):
        tail_scr[which] = x_ref[lb - look:lb, :].astype(F32)


def _deltanet(main, small, conv_w, alog_row, dtb_row, dn_norm_w, *, batch, seq, lb):
    t = batch * seq
    nlb = seq // lb
    nc = lb // CHUNK
    wblk = DN_WIDTH // LANES

    def col(cb):
        return pl.BlockSpec((lb, DN_WIDTH), lambda b, l, cb=cb: (b * nlb + l, cb // wblk))

    row = pl.BlockSpec((1, LANES), lambda b, l: (0, 0))
    return pl.pallas_call(
        functools.partial(_deltanet_kernel, lb=lb),
        grid=(batch, nlb),
        in_specs=[col(CB_DQ), col(CB_DK), col(CB_DV), col(CB_DZ),
                  pl.BlockSpec((lb, LANES), lambda b, l: (b * nlb + l, 0)),
                  pl.BlockSpec(conv_w.shape, lambda b, l: (0, 0)), row, row, row],
        out_specs=pl.BlockSpec((lb, DN_WIDTH), lambda b, l: (b * nlb + l, 0)),
        out_shape=jax.ShapeDtypeStruct((t, DN_WIDTH), BF16),
        scratch_shapes=[
            pltpu.VMEM((DN_HEADS, HEAD_DIM, HEAD_DIM), F32),
            pltpu.VMEM((3, 16, DN_WIDTH), F32),
            pltpu.VMEM((DN_HEADS, lb, HEAD_DIM), F32),
            pltpu.VMEM((nc, DN_HEADS, 2 * CHUNK, HEAD_DIM), BF16),
            pltpu.VMEM((nc, DN_HEADS, CHUNK, CHUNK), BF16),
            pltpu.VMEM((nc, DN_HEADS, HEAD_DIM, CHUNK), BF16),
            pltpu.VMEM((nc, DN_HEADS, 8, LANES), F32),
            pltpu.VMEM((3, CHUNK, DN_WIDTH), F32),
            pltpu.VMEM((2, CHUNK, LANES), F32),
            pltpu.VMEM((LANES, CHUNK), F32),
            pltpu.VMEM((2 * DN_HEADS * CHUNK, HEAD_DIM), F32),
        ],
        compiler_params=_cparams(("parallel", "arbitrary")),
        name="deltanet",
    )(main, main, main, main, small, conv_w, alog_row, dtb_row, dn_norm_w)


def _attention_kernel(q_ref, k_ref, v_ref, tc_ref, ts_ref, o_ref,
                      qr_scr, kr_scr, v_scr, og_scr, lse_scr, *, seq):
    group = pl.program_id(2)
    rows = 256
    half = ROPE_DIM // 2
    lane = lax.broadcasted_iota(jnp.int32, (rows, LANES), 1)
    src_lane = lax.broadcasted_iota(jnp.int32, (LANES, LANES), 0)
    dst_lane = lax.broadcasted_iota(jnp.int32, (LANES, LANES), 1)
    want = jnp.where(dst_lane < half, dst_lane + half, jnp.where(dst_lane < ROPE_DIM, dst_lane - half, -1))
    swap = jnp.where(src_lane == want, 1.0, 0.0).astype(BF16)

    def rope(i, carry):
        r0 = pl.multiple_of(i * rows, rows)
        tc = tc_ref[pl.ds(r0, rows), :]
        ts = ts_ref[pl.ds(r0, rows), :]
        xq = q_ref[pl.ds(r0, rows), :].astype(F32)
        partner_q = jnp.where(lane < half, pltpu.roll(xq, LANES - half, axis=1),
                              pltpu.roll(xq, half, axis=1))
        qr_scr[pl.ds(r0, rows), :] = (xq * tc + partner_q * ts) * (HEAD_DIM ** -0.5)
        xk = k_ref[pl.ds(r0, rows), :]
        kr_scr[pl.ds(r0, rows), :] = xk.astype(F32) * tc + _dot(xk, swap) * ts
        v_scr[pl.ds(r0, rows), :] = v_ref[pl.ds(r0, rows), :].astype(F32)
        return carry

    lax.fori_loop(0, seq // rows, rope, 0, unroll=4)

    ii = lax.broadcasted_iota(jnp.int32, (ATT_BLOCK, 2 * ATT_BLOCK), 0)
    jj = lax.broadcasted_iota(jnp.int32, (ATT_BLOCK, 2 * ATT_BLOCK), 1)
    dist = ii + ATT_BLOCK - jj
    band_mask = jnp.where(dist >= 0, jnp.where(dist <= ATT_BLOCK, 0.0, NEG_BIG), NEG_BIG)
    prev_half = jnp.where(jj < ATT_BLOCK, 1.0, 0.0)
    per_iter = 8

    def run_group(gi, dil):
        nb = seq // (dil * ATT_BLOCK)
        span = dil * ATT_BLOCK

        def rows_of(start):
            if dil == 1:
                return pl.ds(start, ATT_BLOCK)
            return pl.ds(start, ATT_BLOCK, stride=dil)

        def blocks(it, carry):
            where = []
            for u in range(per_iter):
                t = it * per_iter + u
                r = lax.shift_right_logical(t, nb.bit_length() - 1)
                n = jnp.bitwise_and(t, nb - 1)
                start = r + n * span
                where.append((n, start, jnp.maximum(start - span, r)))
            run = min(nb, per_iter)
            kcur = [kr_scr[rows_of(start), :].astype(BF16) for _, start, _ in where]
            vcur = [v_scr[rows_of(start), :].astype(BF16) for _, start, _ in where]

            def prev_of(u, cur, ref):
                if u % run:
                    return cur[u - 1]
                if nb <= per_iter:
                    return cur[u]
                return ref[rows_of(where[u][2]), :].astype(BF16)

            scores = []
            for u, (n, start, prev) in enumerate(where):
                qb = qr_scr[rows_of(start), :].astype(BF16)
                kcat = jnp.concatenate([prev_of(u, kcur, kr_scr), kcur[u]], axis=0)
                s = _dot_nt(qb, kcat)
                no_prev = jnp.where(n > 0, 0.0, NEG_BIG)
                s = s + (band_mask + prev_half * no_prev)
                scores.append(s)
            probs = []
            for s in scores:
                m = jnp.max(s, axis=1, keepdims=True)
                p = jnp.exp(s - m)
                probs.append((m, p, jnp.sum(p, axis=1, keepdims=True)))
            for u, ((n, start, prev), (m, p, den)) in enumerate(zip(where, probs)):
                vcat = jnp.concatenate([prev_of(u, vcur, v_scr), vcur[u]], axis=0)
                acc = _dot(p.astype(BF16), vcat)
                og_scr[gi, rows_of(start), :] = acc / den
                lse_scr[gi, rows_of(start), :] = jnp.broadcast_to(m + jnp.log(den),
                                                                  (ATT_BLOCK, LANES))
            return carry

        lax.fori_loop(0, seq // (ATT_BLOCK * per_iter), blocks, 0)

    for gi, (_, dil) in enumerate(ATT_GROUPS):
        @pl.when(group == gi)
        def _(gi=gi, dil=dil):
            run_group(gi, dil)

    @pl.when(group == len(ATT_GROUPS) - 1)
    def _():
        def merge(i, carry):
            r0 = pl.multiple_of(i * rows, rows)
            ls = [lse_scr[g, pl.ds(r0, rows), :] for g in range(len(ATT_GROUPS))]
            mx = functools.reduce(jnp.maximum, ls)
            es = [jnp.exp(l - mx) for l in ls]
            num = functools.reduce(
                lambda a, b: a + b, [e * og_scr[g, pl.ds(r0, rows), :] for g, e in enumerate(es)])
            den = functools.reduce(lambda a, b: a + b, es)
            o_ref[pl.ds(r0, rows), :] = (num / den).astype(o_ref.dtype)
            return carry

        lax.fori_loop(0, seq // rows, merge, 0)


def _attention(main, rope_cos, rope_sin, *, batch, seq):
    t = batch * seq
    ng = len(ATT_GROUPS)

    def col(cb):
        return pl.BlockSpec((seq, HEAD_DIM),
                            lambda b, h, g, cb=cb: (b, cb + g * ATT_HEADS_PER_GROUP + h))

    tab = pl.BlockSpec((seq, LANES), lambda b, h, g: (0, 0))
    return pl.pallas_call(
        functools.partial(_attention_kernel, seq=seq),
        grid=(batch, ATT_HEADS_PER_GROUP, ng),
        in_specs=[col(CB_AQ), col(CB_AK), col(CB_AV), tab, tab],
        out_specs=pl.BlockSpec((seq, HEAD_DIM), lambda b, h, g: (b, h)),
        out_shape=jax.ShapeDtypeStruct((t, ATT_OUT_WIDTH), BF16),
        scratch_shapes=[
            pltpu.VMEM((seq, HEAD_DIM), F32),
            pltpu.VMEM((seq, HEAD_DIM), F32),
            pltpu.VMEM((seq, HEAD_DIM), F32),
            pltpu.VMEM((ng, seq, HEAD_DIM), F32),
            pltpu.VMEM((ng, seq, LANES), F32),
        ],
        compiler_params=_cparams(("parallel", "parallel", "arbitrary")),
        name="dilated_attention",
    )(main, main, main, rope_cos, rope_sin)


def _rope_tables(seq):
    half = ROPE_DIM // 2
    inv_freq = np.power(ROPE_THETA, -np.arange(half, dtype=np.float64) * (2.0 / ROPE_DIM))
    ang = np.arange(seq, dtype=np.float64)[:, None] * inv_freq[None, :]
    cos, sin = np.cos(ang), np.sin(ang)
    rest = HEAD_DIM - ROPE_DIM
    tc = np.concatenate([cos, cos, np.ones((seq, rest))], axis=1)
    ts = np.concatenate([-sin, sin, np.zeros((seq, rest))], axis=1)
    return jnp.asarray(tc, F32), jnp.asarray(ts, F32)


def _merge_ffn_kernel(x_ref, oa_ref, ob_ref, ga_ref, gb_ref, wa_ref, wb_ref, wo_ref,
                      nw_ref, wg_ref, wu_ref, wd_ref, fw_ref, o_ref, *, final_norm, tf):
    ya = _dot(oa_ref[...], wa_ref[...])
    yb = _dot(ob_ref[...], wb_ref[...])
    merged = (_sigmoid(ga_ref[...].astype(F32)) * ya + _sigmoid(gb_ref[...].astype(F32)) * yb)
    x = x_ref[...] + _dot(merged.astype(BF16), wo_ref[...])
    h = (x * lax.rsqrt(jnp.mean(x * x, axis=-1, keepdims=True) + EPS) * nw_ref[...]).astype(BF16)
    d_ff = wd_ref.shape[0]
    tiles = [slice(f0, min(f0 + tf, d_ff)) for f0 in range(0, d_ff, tf)]
    y = x
    pending = None
    for cols in tiles + [None]:
        issued = None if cols is None else (_dot(h, wg_ref[:, cols]), _dot(h, wu_ref[:, cols]))
        if pending is not None:
            pcols, (gate, up) = pending
            y = y + _dot((_silu(gate) * up).astype(BF16), wd_ref[pcols, :])
        pending = (cols, issued)
    if final_norm:
        y = y * lax.rsqrt(jnp.mean(y * y, axis=-1, keepdims=True) + EPS) * fw_ref[...]
    o_ref[...] = y


def _merge_ffn(x2, oa, ob, main, w_a, w_b, w_o, norm_w, w_gate_up, w_down, final_w, *,
               tm, tf, final_norm):
    t, d = x2.shape
    d_ff = w_down.shape[0]
    gate_cb = CB_GA * LANES // d
    once = pl.Buffered(1)

    def weight(shape, col=0):
        return pl.BlockSpec(shape, lambda i, col=col: (0, col), pipeline_mode=once)

    row = pl.BlockSpec((1, d), lambda i: (0, 0))
    return pl.pallas_call(
        functools.partial(_merge_ffn_kernel, final_norm=final_norm, tf=tf),
        grid=(t // tm,),
        in_specs=[
            pl.BlockSpec((tm, d), lambda i: (i, 0)),
            pl.BlockSpec((tm, DN_WIDTH), lambda i: (i, 0)),
            pl.BlockSpec((tm, ATT_OUT_WIDTH), lambda i: (i, 0)),
            pl.BlockSpec((tm, d), lambda i: (i, gate_cb)),
            pl.BlockSpec((tm, d), lambda i: (i, gate_cb + 1)),
            weight(w_a.shape), weight(w_b.shape), weight(w_o.shape),
            row, weight((d, d_ff)), weight((d, d_ff), 1), weight((d_ff, d)), row,
        ],
        out_specs=pl.BlockSpec((tm, d), lambda i: (i, 0)),
        out_shape=jax.ShapeDtypeStruct((t, d), F32),
        compiler_params=_cparams(("parallel",)),
        name="merge_ffn",
    )(x2, oa, ob, main, main, w_a, w_b, w_o, norm_w, w_gate_up, w_gate_up, w_down, final_w)


def _pad_lanes(v, offset):
    out = jnp.zeros((1, LANES), F32)
    return lax.dynamic_update_slice(out, v.reshape(1, -1).astype(F32), (0, offset))


def kernel(x, norm1_w, w_in, conv_w, a_log, dt_bias, dn_norm_w, w_proj_a, w_proj_b, w_out,
           norm2_w, w_gate_up, w_down, final_norm_w):
    batch, seq, d_model = x.shape
    depth = w_in.shape[0]
    t = batch * seq
    assert d_model == 8 * LANES and seq % (ATT_GROUPS[-1][1] * ATT_BLOCK) == 0
    assert w_in.shape[2] == SRC_GATES + 2 * d_model

    rope_cos, rope_sin = _rope_tables(seq)
    x2 = x.reshape(t, d_model)
    for i in range(depth):
        w_in_t = jnp.swapaxes(w_in[i], 0, 1).astype(BF16)
        main, small = _in_projection(x2, norm1_w[i].reshape(1, d_model), w_in_t,
                                     tm=512, tn=6 * MXU_WIDTH)
        oa = _deltanet(main, small, conv_w[i], _pad_lanes(a_log[i], DN_HEADS),
                       _pad_lanes(dt_bias[i], DN_HEADS), dn_norm_w[i].reshape(1, HEAD_DIM),
                       batch=batch, seq=seq, lb=1024)
        ob = _attention(main, rope_cos, rope_sin, batch=batch, seq=seq)
        x2 = _merge_ffn(x2, oa, ob, main, w_proj_a[i].astype(BF16), w_proj_b[i].astype(BF16),
                        w_out[i].astype(BF16), norm2_w[i].reshape(1, d_model),
                        w_gate_up[i].astype(BF16), w_down[i].astype(BF16),
                        final_norm_w.reshape(1, d_model), tm=512, tf=3 * MXU_WIDTH,
                        final_norm=(i == depth - 1))
    return x2.reshape(batch, seq, d_model)
```

```python
import functools

import jax
import numpy as np
import jax.numpy as jnp
from jax import lax
from jax.experimental import pallas as pl
from jax.experimental.pallas import tpu as pltpu

F32 = jnp.float32
BF16 = jnp.bfloat16

EPS = 1e-6
LANES = 128
MXU_WIDTH = 256
HEAD_DIM = 128
DN_HEADS = 8
DN_WIDTH = DN_HEADS * HEAD_DIM
CONV_WIDTH = 4
CHUNK = 64
DUO = 2
ATT_GROUPS = ((128, 1), (512, 4), (2048, 16))
ATT_HEADS_PER_GROUP = 4
ATT_N_HEADS = len(ATT_GROUPS) * ATT_HEADS_PER_GROUP
ATT_WIDTH = ATT_N_HEADS * HEAD_DIM
ATT_OUT_WIDTH = ATT_HEADS_PER_GROUP * HEAD_DIM
ATT_BLOCK = 128
ROPE_THETA = 500000.0
ROPE_DIM = HEAD_DIM // 4
NEG_BIG = -1e30

CB_DQ, CB_DK, CB_DV, CB_DZ = 0, 8, 16, 24
CB_GA, CB_GB = 32, 40
CB_AQ, CB_AK, CB_AV = 48, 60, 72
MAIN_WIDTH = 84 * LANES

VMEM_LIMIT = 56 * 1024 * 1024

PROJ_ROWS = 512
PROJ_COLS = 6 * MXU_WIDTH
DN_BLOCK = 1024
TAIL_ROWS = 512
FFN_COLS = 3 * MXU_WIDTH


def _cparams(sem):
    return pltpu.CompilerParams(dimension_semantics=sem, vmem_limit_bytes=VMEM_LIMIT)


def _dot(a, b):
    return jnp.dot(a, b, preferred_element_type=F32)


def _dot_nt(a, b):
    return lax.dot_general(a, b, (((1,), (1,)), ((), ())), preferred_element_type=F32)


def _sigmoid(x):
    return 0.5 * jnp.tanh(0.5 * x) + 0.5


def _silu(x):
    half = 0.5 * x
    return half * (jnp.tanh(half) + 1.0)


def _split_bf16(x):
    hi = x.astype(BF16)
    lo = (x - hi.astype(F32)).astype(BF16)
    return jnp.concatenate([hi, lo], axis=1)


SRC_SMALL = 4 * DN_WIDTH
SRC_ATT = SRC_SMALL + 2 * DN_HEADS
SRC_GATES = SRC_ATT + 3 * ATT_WIDTH


def _projection_tiles(tn):
    groups = ((0, 0, SRC_SMALL),
              (CB_GA * LANES, SRC_GATES, 2 * DN_WIDTH),
              (CB_AQ * LANES, SRC_ATT, 3 * ATT_WIDTH))
    return [(dst + off, src + off, min(tn, width - off))
            for dst, src, width in groups for off in range(0, width, tn)]


def _inproj_kernel(x_ref, nw_ref, wt_ref, o_ref, os_ref, *, tiles):
    x = x_ref[...]
    h = (x * lax.rsqrt(jnp.mean(x * x, axis=-1, keepdims=True) + EPS) * nw_ref[...]).astype(BF16)
    os_ref[...] = _dot_nt(h, wt_ref[SRC_SMALL:SRC_SMALL + LANES, :])
    for dst, src, width in tiles:
        o_ref[:, dst:dst + width] = _dot_nt(h, wt_ref[src:src + width, :]).astype(o_ref.dtype)


def _in_projection(x2, norm_w, w_t, *, tm, tn):
    t, d = x2.shape
    once = pl.Buffered(1)
    return pl.pallas_call(
        functools.partial(_inproj_kernel, tiles=_projection_tiles(tn)),
        grid=(t // tm,),
        in_specs=[
            pl.BlockSpec((tm, d), lambda i: (i, 0)),
            pl.BlockSpec((1, d), lambda i: (0, 0)),
            pl.BlockSpec(w_t.shape, lambda i: (0, 0), pipeline_mode=once),
        ],
        out_specs=[
            pl.BlockSpec((tm, MAIN_WIDTH), lambda i: (i, 0)),
            pl.BlockSpec((tm, LANES), lambda i: (i, 0)),
        ],
        out_shape=[
            jax.ShapeDtypeStruct((t, MAIN_WIDTH), BF16),
            jax.ShapeDtypeStruct((t, LANES), F32),
        ],
        compiler_params=_cparams(("parallel",)),
        name="in_projection",
    )(x2, norm_w, w_t)


def _deltanet_kernel(q_ref, k_ref, v_ref, z_ref, sm_ref, cw_ref, alog_ref, dtb_ref, nw_ref, o_ref,
                     s_scr, tail_scr, u_scr, wq_scr, qk_scr, kdt_scr, egl_scr,
                     pre_scr, gate_scr, gct_scr, sq_scr, *, lb):
    lstep = pl.program_id(1)
    nc = lb // CHUNK
    width = DN_WIDTH
    look = 16

    @pl.when(lstep == 0)
    def _():
        s_scr[...] = jnp.zeros_like(s_scr)
        tail_scr[...] = jnp.zeros_like(tail_scr)
        u_scr[:, 0:DUO * CHUNK, :] = jnp.zeros((DN_HEADS, DUO * CHUNK, HEAD_DIM), F32)
        wq_scr[0:DUO] = jnp.zeros((DUO,) + wq_scr.shape[1:], BF16)
        qk_scr[0:DUO] = jnp.zeros((DUO,) + qk_scr.shape[1:], BF16)
        kdt_scr[0:DUO] = jnp.zeros((DUO,) + kdt_scr.shape[1:], BF16)
        egl_scr[0:DUO] = jnp.zeros((DUO,) + egl_scr.shape[1:], F32)

    rowi = lax.broadcasted_iota(jnp.int32, (CHUNK, LANES), 0)
    lanei = lax.broadcasted_iota(jnp.int32, (CHUNK, LANES), 1)
    in_a = lanei < CHUNK
    ii = rowi
    jj = jnp.bitwise_and(lanei, CHUNK - 1)
    eye = (ii == jj).astype(F32)

    heads = range(DN_HEADS)
    pairs = [(h, h + 1) for h in range(0, DN_HEADS, 2)]
    head_cols = [slice(h * HEAD_DIM, (h + 1) * HEAD_DIM) for h in heads]
    srcs = (q_ref, k_ref, v_ref)
    ones = jnp.ones((2 * HEAD_DIM, HEAD_DIM), BF16)

    def anchor(value):
        rows8 = value[:8, :]
        zero = jnp.where(rows8 != rows8, rows8, 0.0)
        if zero.shape[1] < LANES:
            zero = jnp.concatenate([zero] * (LANES // zero.shape[1]), axis=1)
        return zero

    def conv_piece(which, h, c, r0, slot, zero):
        x_ref, cols = srcs[which], head_cols[h]
        prev0 = pl.multiple_of(jnp.maximum(r0 - look, 0), look)
        top = lax.select(jnp.asarray(c) == 0, tail_scr[which, :, cols],
                         x_ref[pl.ds(prev0, look), cols].astype(F32))
        win = jnp.concatenate([top, x_ref[pl.ds(r0, CHUNK), cols].astype(F32)], axis=0)
        w0 = which * width + h * HEAD_DIM
        acc = None
        for j in range(CONV_WIDTH):
            back = CONV_WIDTH - 1 - j
            rows = win if back == 0 else pltpu.roll(win, back, axis=0)
            term = rows[look:, :] * (cw_ref[j:j + 1, w0:w0 + HEAD_DIM] + zero[0:1, :])
            acc = term if acc is None else acc + term
        y = _silu(acc)
        pre_scr[slot, which, :, cols] = y
        if which < 2:
            sq_scr[slot, (which * DN_HEADS + h) * CHUNK:(which * DN_HEADS + h + 1) * CHUNK, :] = y * y

    def gate_piece(r0, slot, zero):
        sm = sm_ref[pl.ds(r0, CHUNK), :]
        xs = sm + (dtb_ref[...] + zero[0:1, :])
        softplus = jnp.maximum(xs, 0.0) + jnp.log(1.0 + jnp.exp(-jnp.abs(xs)))
        gc_all = -jnp.exp(alog_ref[...]) * softplus
        shift = 1
        while shift < CHUNK:
            gc_all = gc_all + jnp.where(rowi >= shift, pltpu.roll(gc_all, shift, axis=0), 0.0)
            shift *= 2
        gate_scr[slot, 0] = _sigmoid(sm)
        gate_scr[slot, 1] = gc_all
        gct_scr[slot] = gc_all.T

    def front_pieces(c, slot):
        r0 = c * CHUNK if isinstance(c, int) else pl.multiple_of(c * CHUNK, CHUNK)
        pieces = [functools.partial(conv_piece, which, h, c, r0, slot)
                  for which in range(3) for h in heads]
        return pieces + [functools.partial(gate_piece, r0, slot)]

    def front_finish(slot):
        inv_norm = lax.rsqrt(_dot(_split_bf16(sq_scr[slot]), ones) + EPS)
        for h in heads:
            cols = head_cols[h]
            pre_scr[slot, 0, :, cols] = pre_scr[slot, 0, :, cols] * (
                inv_norm[h * CHUNK:(h + 1) * CHUNK] * (HEAD_DIM ** -0.5))
            pre_scr[slot, 1, :, cols] = pre_scr[slot, 1, :, cols] * inv_norm[
                (DN_HEADS + h) * CHUNK:(DN_HEADS + h + 1) * CHUNK]

    for slot in range(DUO):
        for piece in front_pieces(slot, slot):
            piece(jnp.zeros((8, LANES), F32))
        front_finish(slot)

    def recur_read(cs):
        ss = [s_scr[h] for h in heads]
        rs = [_dot(wq_scr[cs, h], ss[h].astype(BF16)) for h in heads]
        return ss, rs

    def recur_update(cs, ss, rs, live):
        rc0 = pl.multiple_of(cs * CHUNK, CHUNK)
        vbs = []
        for h in heads:
            v_new = u_scr[h, pl.ds(rc0, CHUNK), :] - rs[h][:CHUNK]
            egl = egl_scr[cs, h][0:1, :]
            if live is not None:
                v_new = lax.select(live, v_new, jnp.zeros_like(v_new))
                egl = lax.select(live, egl, jnp.ones_like(egl))
            vbs.append(v_new.astype(BF16))
            s_scr[h] = ss[h] * egl + _dot(kdt_scr[cs, h], vbs[h])
        return [rs[h][CHUNK:] + _dot(qk_scr[cs, h], vbs[h]) for h in heads]

    def recur_finish(cs, outs):
        rc0 = pl.multiple_of(cs * CHUNK, CHUNK)
        for h in heads:
            o = outs[h]
            on = o * lax.rsqrt(jnp.mean(o * o, axis=-1, keepdims=True) + EPS) * nw_ref[...]
            z = z_ref[pl.ds(rc0, CHUNK), head_cols[h]].astype(F32)
            o_ref[pl.ds(rc0, CHUNK), head_cols[h]] = (on * _silu(z)).astype(o_ref.dtype)

    def prep(i, carry):
        first = i * DUO
        behind = jnp.maximum(first - DUO, 0)
        live = i > 0
        units = [(slot, a, b) for slot in range(DUO) for a, b in pairs]
        ns, held = [], {}
        qs, rhs = {}, {}
        zero_k = jnp.zeros((CHUNK, HEAD_DIM), BF16)
        for slot, a, b in units:
            c = first + slot
            beta_all = gate_scr[slot, 0]
            gc_all = gate_scr[slot, 1]
            gc_t = gct_scr[slot]
            side = []
            for h in (a, b):
                cols = head_cols[h]
                q = pre_scr[slot, 0, :, cols]
                k = pre_scr[slot, 1, :, cols]
                v = pre_scr[slot, 2, :, cols]
                bb = jnp.broadcast_to(beta_all[:, h:h + 1], (CHUNK, LANES))
                gcb = jnp.broadcast_to(gc_all[:, DN_HEADS + h:DN_HEADS + h + 1], (CHUNK, LANES))
                kb = k * bb
                held[slot, h] = (q, k, v, bb, gcb, kb)
                side.append((jnp.concatenate([kb, q], axis=0), k.astype(BF16), gcb,
                             gc_t[DN_HEADS + h:DN_HEADS + h + 1, :]))
            (lhs_a, k_a, gcb_a, row_a), (lhs_b, k_b, gcb_b, row_b) = side
            both = _dot_nt(jnp.concatenate([lhs_a, lhs_b], axis=1).astype(BF16),
                           jnp.concatenate([jnp.concatenate([k_a, zero_k], axis=1),
                                            jnp.concatenate([zero_k, k_b], axis=1)], axis=0))
            g_col = jnp.where(in_a, gcb_a, gcb_b)
            g_row = jnp.concatenate([row_a, row_b], axis=1)
            decay = jnp.where(ii >= jj, jnp.exp(jnp.minimum(g_col - g_row, 0.0)), 0.0)
            ns.append(jnp.where(ii > jj, -both[:CHUNK] * decay, 0.0))
            qk = (both[CHUNK:] * decay).astype(BF16)
            qk_scr[c, a] = qk[:, :CHUNK]
            qk_scr[c, b] = qk[:, CHUNK:]
            kb_last = kb

        def block_diag(m):
            return jnp.concatenate([jnp.where(in_a, m, 0.0), jnp.where(in_a, 0.0, m)],
                                   axis=0).astype(BF16)

        def late_piece(slot, h, zero):
            c = first + slot
            q, k, v, bb, gcb, kb = held[slot, h]
            gcz = gcb + zero[0:1, :]
            g_last = gcz[CHUNK - 1:CHUNK, :]
            eg = jnp.exp(gcz)
            kdt_scr[c, h] = (k * jnp.exp(g_last - gcz)).T.astype(BF16)
            egl_scr[c, h] = jnp.broadcast_to(jnp.exp(g_last), (8, LANES))
            rhs[slot, h] = jnp.concatenate([v * bb, kb * eg], axis=1).astype(BF16)
            qs[slot, h] = q * eg

        ahead = jnp.minimum(first + DUO, nc - DUO)
        late = [functools.partial(late_piece, slot, h) for slot in range(DUO) for h in heads]
        front = []
        for slot in range(DUO):
            front += front_pieces(ahead + slot, slot)
        gaps = [(late + front)[g::7] for g in range(7)]

        def fill(gap, operand):
            zero = anchor(operand)
            for piece in gaps[gap]:
                piece(zero)

        n_units = range(len(units))
        state_in = recur_read(behind)
        fill(0, kb_last)
        ps = [_dot(ns[n].astype(BF16), block_diag(ns[n])) for n in n_units]
        fill(1, ns[-1])
        xs_ = [eye + ns[n] for n in n_units]
        for level in range(4):
            operand = xs_[-1]
            for n in n_units:
                r = _dot(jnp.concatenate([xs_[n], ps[n]], axis=0).astype(BF16), block_diag(ps[n]))
                xs_[n] = xs_[n] + r[:CHUNK]
                ps[n] = r[CHUNK:]
            if level == 0:
                outs0 = recur_update(behind, *state_in, live)
            if level == 1:
                state_in = recur_read(behind + 1)
            if level == 2:
                outs1 = recur_update(behind + 1, *state_in, live)
                recur_finish(behind, outs0)
            if level == 3:
                recur_finish(behind + 1, outs1)
            fill(2 + level, operand)
        operand = xs_[-1]
        for n in n_units:
            xs_[n] = xs_[n] + _dot(xs_[n].astype(BF16), block_diag(ps[n]))
        fill(6, operand)
        zero_rhs = jnp.zeros((CHUNK, 2 * HEAD_DIM), BF16)
        for n, (slot, a, b) in enumerate(units):
            c = first + slot
            r0 = pl.multiple_of(c * CHUNK, CHUNK)
            uw = _dot(xs_[n].astype(BF16),
                      jnp.concatenate([jnp.concatenate([rhs[slot, a], zero_rhs], axis=1),
                                       jnp.concatenate([zero_rhs, rhs[slot, b]], axis=1)], axis=0))
            for h, base in ((a, 0), (b, 2 * HEAD_DIM)):
                u_scr[h, pl.ds(r0, CHUNK), :] = uw[:, base:base + HEAD_DIM]
                wq_scr[c, h] = jnp.concatenate(
                    [uw[:, base + HEAD_DIM:base + 2 * HEAD_DIM], qs[slot, h]], axis=0).astype(BF16)
        for slot in range(DUO):
            front_finish(slot)
        return carry

    lax.fori_loop(0, nc // DUO, prep, 0)
    for cs in range(nc - DUO, nc):
        recur_finish(cs, recur_update(cs, *recur_read(cs), None))

    for which, x_ref in enumerate((q_ref, k_ref, v_ref)):
        tail_scr[which] = x_ref[lb - look:lb, :].astype(F32)


def _deltanet(main, small, conv_w, alog_row, dtb_row, dn_norm_w, *, batch, seq, lb):
    t = batch * seq
    nlb = seq // lb
    nc = lb // CHUNK
    wblk = DN_WIDTH // LANES

    def col(cb):
        return pl.BlockSpec((lb, DN_WIDTH), lambda b, l, cb=cb: (b * nlb + l, cb // wblk))

    row = pl.BlockSpec((1, LANES), lambda b, l: (0, 0))
    return pl.pallas_call(
        functools.partial(_deltanet_kernel, lb=lb),
        grid=(batch, nlb),
        in_specs=[col(CB_DQ), col(CB_DK), col(CB_DV), col(CB_DZ),
                  pl.BlockSpec((lb, LANES), lambda b, l: (b * nlb + l, 0)),
                  pl.BlockSpec(conv_w.shape, lambda b, l: (0, 0)), row, row, row],
        out_specs=pl.BlockSpec((lb, DN_WIDTH), lambda b, l: (b * nlb + l, 0)),
        out_shape=jax.ShapeDtypeStruct((t, DN_WIDTH), BF16),
        scratch_shapes=[
            pltpu.VMEM((DN_HEADS, HEAD_DIM, HEAD_DIM), F32),
            pltpu.VMEM((3, 16, DN_WIDTH), F32),
            pltpu.VMEM((DN_HEADS, lb, HEAD_DIM), F32),
            pltpu.VMEM((nc, DN_HEADS, 2 * CHUNK, HEAD_DIM), BF16),
            pltpu.VMEM((nc, DN_HEADS, CHUNK, CHUNK), BF16),
            pltpu.VMEM((nc, DN_HEADS, HEAD_DIM, CHUNK), BF16),
            pltpu.VMEM((nc, DN_HEADS, 8, LANES), F32),
            pltpu.VMEM((DUO, 3, CHUNK, DN_WIDTH), F32),
            pltpu.VMEM((DUO, 2, CHUNK, LANES), F32),
            pltpu.VMEM((DUO, LANES, CHUNK), F32),
            pltpu.VMEM((DUO, 2 * DN_HEADS * CHUNK, HEAD_DIM), F32),
        ],
        compiler_params=_cparams(("parallel", "arbitrary")),
        name="deltanet",
    )(main, main, main, main, small, conv_w, alog_row, dtb_row, dn_norm_w)


def _attention_kernel(q_ref, k_ref, v_ref, tc_ref, ts_ref, o_ref,
                      qr_scr, kr_scr, v_scr, og_scr, lse_scr, *, seq):
    group = pl.program_id(2)
    rows = 256
    half = ROPE_DIM // 2
    lane = lax.broadcasted_iota(jnp.int32, (rows, LANES), 1)
    src_lane = lax.broadcasted_iota(jnp.int32, (LANES, LANES), 0)
    dst_lane = lax.broadcasted_iota(jnp.int32, (LANES, LANES), 1)
    want = jnp.where(dst_lane < half, dst_lane + half, jnp.where(dst_lane < ROPE_DIM, dst_lane - half, -1))
    swap = jnp.where(src_lane == want, 1.0, 0.0).astype(BF16)

    def rope(i, carry):
        r0 = pl.multiple_of(i * rows, rows)
        tc = tc_ref[pl.ds(r0, rows), :]
        ts = ts_ref[pl.ds(r0, rows), :]
        xq = q_ref[pl.ds(r0, rows), :].astype(F32)
        partner_q = jnp.where(lane < half, pltpu.roll(xq, LANES - half, axis=1),
                              pltpu.roll(xq, half, axis=1))
        qr_scr[pl.ds(r0, rows), :] = (xq * tc + partner_q * ts) * (HEAD_DIM ** -0.5)
        xk = k_ref[pl.ds(r0, rows), :]
        kr_scr[pl.ds(r0, rows), :] = xk.astype(F32) * tc + _dot(xk, swap) * ts
        v_scr[pl.ds(r0, rows), :] = v_ref[pl.ds(r0, rows), :].astype(F32)
        return carry

    lax.fori_loop(0, seq // rows, rope, 0, unroll=4)

    ii = lax.broadcasted_iota(jnp.int32, (ATT_BLOCK, 2 * ATT_BLOCK), 0)
    jj = lax.broadcasted_iota(jnp.int32, (ATT_BLOCK, 2 * ATT_BLOCK), 1)
    dist = ii + ATT_BLOCK - jj
    band_mask = jnp.where(dist >= 0, jnp.where(dist <= ATT_BLOCK, 0.0, NEG_BIG), NEG_BIG)
    prev_half = jnp.where(jj < ATT_BLOCK, 1.0, 0.0)
    per_iter = 8

    def run_group(gi, dil):
        nb = seq // (dil * ATT_BLOCK)
        span = dil * ATT_BLOCK

        def rows_of(start):
            if dil == 1:
                return pl.ds(start, ATT_BLOCK)
            return pl.ds(start, ATT_BLOCK, stride=dil)

        def blocks(it, carry):
            where = []
            for u in range(per_iter):
                t = it * per_iter + u
                r = lax.shift_right_logical(t, nb.bit_length() - 1)
                n = jnp.bitwise_and(t, nb - 1)
                start = r + n * span
                where.append((n, start, jnp.maximum(start - span, r)))
            run = min(nb, per_iter)
            kcur = [kr_scr[rows_of(start), :].astype(BF16) for _, start, _ in where]
            vcur = [v_scr[rows_of(start), :].astype(BF16) for _, start, _ in where]

            def prev_of(u, cur, ref):
                if u % run:
                    return cur[u - 1]
                if nb <= per_iter:
                    return cur[u]
                return ref[rows_of(where[u][2]), :].astype(BF16)

            scores = []
            for u, (n, start, prev) in enumerate(where):
                qb = qr_scr[rows_of(start), :].astype(BF16)
                kcat = jnp.concatenate([prev_of(u, kcur, kr_scr), kcur[u]], axis=0)
                s = _dot_nt(qb, kcat)
                no_prev = jnp.where(n > 0, 0.0, NEG_BIG)
                s = s + (band_mask + prev_half * no_prev)
                scores.append(s)
            probs = []
            for s in scores:
                m = jnp.max(s, axis=1, keepdims=True)
                p = jnp.exp(s - m)
                probs.append((m, p, jnp.sum(p, axis=1, keepdims=True)))
            for u, ((n, start, prev), (m, p, den)) in enumerate(zip(where, probs)):
                vcat = jnp.concatenate([prev_of(u, vcur, v_scr), vcur[u]], axis=0)
                acc = _dot(p.astype(BF16), vcat)
                og_scr[gi, rows_of(start), :] = acc / den
                lse_scr[gi, rows_of(start), :] = jnp.broadcast_to(m + jnp.log(den),
                                                                  (ATT_BLOCK, LANES))
            return carry

        lax.fori_loop(0, seq // (ATT_BLOCK * per_iter), blocks, 0)

    for gi, (_, dil) in enumerate(ATT_GROUPS):
        @pl.when(group == gi)
        def _(gi=gi, dil=dil):
            run_group(gi, dil)

    @pl.when(group == len(ATT_GROUPS) - 1)
    def _():
        def merge(i, carry):
            r0 = pl.multiple_of(i * rows, rows)
            ls = [lse_scr[g, pl.ds(r0, rows), :] for g in range(len(ATT_GROUPS))]
            mx = functools.reduce(jnp.maximum, ls)
            es = [jnp.exp(l - mx) for l in ls]
            num = functools.reduce(
                lambda a, b: a + b, [e * og_scr[g, pl.ds(r0, rows), :] for g, e in enumerate(es)])
            den = functools.reduce(lambda a, b: a + b, es)
            o_ref[pl.ds(r0, rows), :] = (num / den).astype(o_ref.dtype)
            return carry

        lax.fori_loop(0, seq // rows, merge, 0)


def _attention(main, rope_cos, rope_sin, *, batch, seq):
    t = batch * seq
    ng = len(ATT_GROUPS)

    def col(cb):
        return pl.BlockSpec((seq, HEAD_DIM),
                            lambda b, h, g, cb=cb: (b, cb + g * ATT_HEADS_PER_GROUP + h))

    tab = pl.BlockSpec((seq, LANES), lambda b, h, g: (0, 0))
    return pl.pallas_call(
        functools.partial(_attention_kernel, seq=seq),
        grid=(batch, ATT_HEADS_PER_GROUP, ng),
        in_specs=[col(CB_AQ), col(CB_AK), col(CB_AV), tab, tab],
        out_specs=pl.BlockSpec((seq, HEAD_DIM), lambda b, h, g: (b, h)),
        out_shape=jax.ShapeDtypeStruct((t, ATT_OUT_WIDTH), BF16),
        scratch_shapes=[
            pltpu.VMEM((seq, HEAD_DIM), F32),
            pltpu.VMEM((seq, HEAD_DIM), F32),
            pltpu.VMEM((seq, HEAD_DIM), F32),
            pltpu.VMEM((ng, seq, HEAD_DIM), F32),
            pltpu.VMEM((ng, seq, LANES), F32),
        ],
        compiler_params=_cparams(("parallel", "parallel", "arbitrary")),
        name="dilated_attention",
    )(main, main, main, rope_cos, rope_sin)


def _rope_tables(seq):
    half = ROPE_DIM // 2
    inv_freq = np.power(ROPE_THETA, -np.arange(half, dtype=np.float64) * (2.0 / ROPE_DIM))
    ang = np.arange(seq, dtype=np.float64)[:, None] * inv_freq[None, :]
    cos, sin = np.cos(ang), np.sin(ang)
    rest = HEAD_DIM - ROPE_DIM
    tc = np.concatenate([cos, cos, np.ones((seq, rest))], axis=1)
    ts = np.concatenate([-sin, sin, np.zeros((seq, rest))], axis=1)
    return jnp.asarray(tc, F32), jnp.asarray(ts, F32)


def _merge_ffn_kernel(x_ref, oa_ref, ob_ref, ga_ref, gb_ref, wa_ref, wb_ref, wo_ref,
                      nw_ref, wg_ref, wu_ref, wd_ref, fw_ref, o_ref, *, final_norm, tf):
    ya = _dot(oa_ref[...], wa_ref[...])
    yb = _dot(ob_ref[...], wb_ref[...])
    merged = (_sigmoid(ga_ref[...].astype(F32)) * ya + _sigmoid(gb_ref[...].astype(F32)) * yb)
    x = x_ref[...] + _dot(merged.astype(BF16), wo_ref[...])
    h = (x * lax.rsqrt(jnp.mean(x * x, axis=-1, keepdims=True) + EPS) * nw_ref[...]).astype(BF16)
    d_ff = wd_ref.shape[0]
    tiles = [slice(f0, min(f0 + tf, d_ff)) for f0 in range(0, d_ff, tf)]
    y = x
    pending = None
    for cols in tiles + [None]:
        issued = None if cols is None else (_dot(h, wg_ref[:, cols]), _dot(h, wu_ref[:, cols]))
        if pending is not None:
            pcols, (gate, up) = pending
            y = y + _dot((_silu(gate) * up).astype(BF16), wd_ref[pcols, :])
        pending = (cols, issued)
    if final_norm:
        y = y * lax.rsqrt(jnp.mean(y * y, axis=-1, keepdims=True) + EPS) * fw_ref[...]
    o_ref[...] = y


def _merge_ffn(x2, oa, ob, main, w_a, w_b, w_o, norm_w, w_gate_up, w_down, final_w, *,
               tm, tf, final_norm):
    t, d = x2.shape
    d_ff = w_down.shape[0]
    gate_cb = CB_GA * LANES // d
    once = pl.Buffered(1)

    def weight(shape, col=0):
        return pl.BlockSpec(shape, lambda i, col=col: (0, col), pipeline_mode=once)

    row = pl.BlockSpec((1, d), lambda i: (0, 0))
    return pl.pallas_call(
        functools.partial(_merge_ffn_kernel, final_norm=final_norm, tf=tf),
        grid=(t // tm,),
        in_specs=[
            pl.BlockSpec((tm, d), lambda i: (i, 0)),
            pl.BlockSpec((tm, DN_WIDTH), lambda i: (i, 0)),
            pl.BlockSpec((tm, ATT_OUT_WIDTH), lambda i: (i, 0)),
            pl.BlockSpec((tm, d), lambda i: (i, gate_cb)),
            pl.BlockSpec((tm, d), lambda i: (i, gate_cb + 1)),
            weight(w_a.shape), weight(w_b.shape), weight(w_o.shape),
            row, weight((d, d_ff)), weight((d, d_ff), 1), weight((d_ff, d)), row,
        ],
        out_specs=pl.BlockSpec((tm, d), lambda i: (i, 0)),
        out_shape=jax.ShapeDtypeStruct((t, d), F32),
        compiler_params=_cparams(("parallel",)),
        name="merge_ffn",
    )(x2, oa, ob, main, main, w_a, w_b, w_o, norm_w, w_gate_up, w_gate_up, w_down, final_w)


def _pad_lanes(v, offset):
    out = jnp.zeros((1, LANES), F32)
    return lax.dynamic_update_slice(out, v.reshape(1, -1).astype(F32), (0, offset))


def kernel(x, norm1_w, w_in, conv_w, a_log, dt_bias, dn_norm_w, w_proj_a, w_proj_b, w_out,
           norm2_w, w_gate_up, w_down, final_norm_w):
    batch, seq, d_model = x.shape
    depth = w_in.shape[0]
    t = batch * seq
    assert d_model == 8 * LANES and seq % (ATT_GROUPS[-1][1] * ATT_BLOCK) == 0
    assert w_in.shape[2] == SRC_GATES + 2 * d_model

    rope_cos, rope_sin = _rope_tables(seq)
    x2 = x.reshape(t, d_model)
    for i in range(depth):
        w_in_t = jnp.swapaxes(w_in[i], 0, 1).astype(BF16)
        main, small = _in_projection(x2, norm1_w[i].reshape(1, d_model), w_in_t,
                                     tm=PROJ_ROWS, tn=PROJ_COLS)
        oa = _deltanet(main, small, conv_w[i], _pad_lanes(a_log[i], DN_HEADS),
                       _pad_lanes(dt_bias[i], DN_HEADS), dn_norm_w[i].reshape(1, HEAD_DIM),
                       batch=batch, seq=seq, lb=DN_BLOCK)
        ob = _attention(main, rope_cos, rope_sin, batch=batch, seq=seq)
        x2 = _merge_ffn(x2, oa, ob, main, w_proj_a[i].astype(BF16), w_proj_b[i].astype(BF16),
                        w_out[i].astype(BF16), norm2_w[i].reshape(1, d_model),
                        w_gate_up[i].astype(BF16), w_down[i].astype(BF16),
                        final_norm_w.reshape(1, d_model), tm=TAIL_ROWS, tf=FFN_COLS,
                        final_norm=(i == depth - 1))
    return x2.reshape(batch, seq, d_model)
```

```python
import functools

import jax
import numpy as np
import jax.numpy as jnp
from jax import lax
from jax.experimental import pallas as pl
from jax.experimental.pallas import tpu as pltpu

F32 = jnp.float32
BF16 = jnp.bfloat16

EPS = 1e-6
LANES = 128
MXU_WIDTH = 256
HEAD_DIM = 128
DN_HEADS = 8
DN_WIDTH = DN_HEADS * HEAD_DIM
CONV_WIDTH = 4
CHUNK = 64
DUO = 2
ATT_GROUPS = ((128, 1), (512, 4), (2048, 16))
ATT_HEADS_PER_GROUP = 4
ATT_N_HEADS = len(ATT_GROUPS) * ATT_HEADS_PER_GROUP
ATT_WIDTH = ATT_N_HEADS * HEAD_DIM
ATT_OUT_WIDTH = ATT_HEADS_PER_GROUP * HEAD_DIM
ATT_BLOCK = 128
ROPE_THETA = 500000.0
ROPE_DIM = HEAD_DIM // 4
NEG_BIG = -1e30

CB_DQ, CB_DK, CB_DV, CB_DZ = 0, 8, 16, 24
CB_GA, CB_GB = 32, 40
CB_AQ, CB_AK, CB_AV = 48, 60, 72
MAIN_WIDTH = 84 * LANES

VMEM_LIMIT = 56 * 1024 * 1024

PROJ_ROWS = 512
PROJ_COLS = 6 * MXU_WIDTH
DN_BLOCK = 1024
TAIL_ROWS = 512
FFN_COLS = 3 * MXU_WIDTH


def _cparams(sem):
    return pltpu.CompilerParams(dimension_semantics=sem, vmem_limit_bytes=VMEM_LIMIT)


def _dot(a, b):
    return jnp.dot(a, b, preferred_element_type=F32)


def _dot_nt(a, b):
    return lax.dot_general(a, b, (((1,), (1,)), ((), ())), preferred_element_type=F32)


def _sigmoid(x):
    return 0.5 * jnp.tanh(0.5 * x) + 0.5


def _silu(x):
    half = 0.5 * x
    return half * (jnp.tanh(half) + 1.0)


SRC_SMALL = 4 * DN_WIDTH
SRC_ATT = SRC_SMALL + 2 * DN_HEADS
SRC_GATES = SRC_ATT + 3 * ATT_WIDTH


def _projection_tiles(tn):
    groups = ((0, 0, SRC_SMALL),
              (CB_GA * LANES, SRC_GATES, 2 * DN_WIDTH),
              (CB_AQ * LANES, SRC_ATT, 3 * ATT_WIDTH))
    return [(dst + off, src + off, min(tn, width - off))
            for dst, src, width in groups for off in range(0, width, tn)]


def _inproj_kernel(x_ref, nw_ref, wt_ref, o_ref, os_ref, *, tiles):
    x = x_ref[...]
    h = (x * lax.rsqrt(jnp.mean(x * x, axis=-1, keepdims=True) + EPS) * nw_ref[...]).astype(BF16)
    os_ref[...] = _dot_nt(h, wt_ref[SRC_SMALL:SRC_SMALL + LANES, :])
    for dst, src, width in tiles:
        o_ref[:, dst:dst + width] = _dot_nt(h, wt_ref[src:src + width, :]).astype(o_ref.dtype)


def _in_projection(x2, norm_w, w_t, *, tm, tn):
    t, d = x2.shape
    once = pl.Buffered(1)
    return pl.pallas_call(
        functools.partial(_inproj_kernel, tiles=_projection_tiles(tn)),
        grid=(t // tm,),
        in_specs=[
            pl.BlockSpec((tm, d), lambda i: (i, 0)),
            pl.BlockSpec((1, d), lambda i: (0, 0)),
            pl.BlockSpec(w_t.shape, lambda i: (0, 0), pipeline_mode=once),
        ],
        out_specs=[
            pl.BlockSpec((tm, MAIN_WIDTH), lambda i: (i, 0)),
            pl.BlockSpec((tm, LANES), lambda i: (i, 0)),
        ],
        out_shape=[
            jax.ShapeDtypeStruct((t, MAIN_WIDTH), BF16),
            jax.ShapeDtypeStruct((t, LANES), F32),
        ],
        compiler_params=_cparams(("parallel",)),
        name="in_projection",
    )(x2, norm_w, w_t)


def _deltanet_kernel(q_ref, k_ref, v_ref, z_ref, sm_ref, cw_ref, alog_ref, dtb_ref, nw_ref, o_ref,
                     s_scr, tail_scr, u_scr, wq_scr, qk_scr, kdt_scr, egl_scr,
                     pre_scr, gate_scr, gct_scr, *, lb):
    lstep = pl.program_id(1)
    nc = lb // CHUNK
    width = DN_WIDTH
    look = 16

    @pl.when(lstep == 0)
    def _():
        s_scr[...] = jnp.zeros_like(s_scr)
        tail_scr[...] = jnp.zeros_like(tail_scr)
        u_scr[:, 0:DUO * CHUNK, :] = jnp.zeros((DN_HEADS, DUO * CHUNK, HEAD_DIM), F32)
        wq_scr[0:DUO] = jnp.zeros((DUO,) + wq_scr.shape[1:], BF16)
        qk_scr[0:DUO] = jnp.zeros((DUO,) + qk_scr.shape[1:], BF16)
        kdt_scr[0:DUO] = jnp.zeros((DUO,) + kdt_scr.shape[1:], BF16)
        egl_scr[0:DUO] = jnp.zeros((DUO,) + egl_scr.shape[1:], F32)

    rowi = lax.broadcasted_iota(jnp.int32, (CHUNK, LANES), 0)
    lanei = lax.broadcasted_iota(jnp.int32, (CHUNK, LANES), 1)
    in_a = lanei < CHUNK
    ii = rowi
    jj = jnp.bitwise_and(lanei, CHUNK - 1)
    eye = (ii == jj).astype(F32)

    heads = range(DN_HEADS)
    pairs = [(h, h + 1) for h in range(0, DN_HEADS, 2)]
    head_cols = [slice(h * HEAD_DIM, (h + 1) * HEAD_DIM) for h in heads]
    srcs = (q_ref, k_ref, v_ref)

    def anchor(value):
        rows8 = value[:8, :]
        zero = jnp.where(rows8 != rows8, rows8, 0.0)
        if zero.shape[1] < LANES:
            zero = jnp.concatenate([zero] * (LANES // zero.shape[1]), axis=1)
        return zero

    def conv_piece(which, h, c, r0, slot, zero):
        x_ref, cols = srcs[which], head_cols[h]
        prev0 = pl.multiple_of(jnp.maximum(r0 - look, 0), look)
        top = lax.select(jnp.asarray(c) == 0, tail_scr[which, :, cols],
                         x_ref[pl.ds(prev0, look), cols].astype(F32))
        win = jnp.concatenate([top, x_ref[pl.ds(r0, CHUNK), cols].astype(F32)], axis=0)
        w0 = which * width + h * HEAD_DIM
        acc = None
        for j in range(CONV_WIDTH):
            back = CONV_WIDTH - 1 - j
            rows = win if back == 0 else pltpu.roll(win, back, axis=0)
            term = rows[look:, :] * (cw_ref[j:j + 1, w0:w0 + HEAD_DIM] + zero[0:1, :])
            acc = term if acc is None else acc + term
        y = _silu(acc)
        pre_scr[slot, which, :, cols] = y

    def gate_piece(r0, slot, zero):
        sm = sm_ref[pl.ds(r0, CHUNK), :]
        xs = sm + (dtb_ref[...] + zero[0:1, :])
        softplus = jnp.maximum(xs, 0.0) + jnp.log(1.0 + jnp.exp(-jnp.abs(xs)))
        gc_all = -jnp.exp(alog_ref[...]) * softplus
        shift = 1
        while shift < CHUNK:
            gc_all = gc_all + jnp.where(rowi >= shift, pltpu.roll(gc_all, shift, axis=0), 0.0)
            shift *= 2
        gate_scr[slot, 0] = _sigmoid(sm)
        gate_scr[slot, 1] = gc_all
        gct_scr[slot] = gc_all.T

    def front_pieces(c, slot):
        r0 = c * CHUNK if isinstance(c, int) else pl.multiple_of(c * CHUNK, CHUNK)
        pieces = [functools.partial(conv_piece, which, h, c, r0, slot)
                  for which in range(3) for h in heads]
        return pieces + [functools.partial(gate_piece, r0, slot)]

    def front_finish(slot):
        for h in heads:
            cols = head_cols[h]
            q = pre_scr[slot, 0, :, cols]
            k = pre_scr[slot, 1, :, cols]
            pre_scr[slot, 0, :, cols] = q * (lax.rsqrt(jnp.sum(q * q, axis=-1, keepdims=True) + EPS)
                                             * (HEAD_DIM ** -0.5))
            pre_scr[slot, 1, :, cols] = k * lax.rsqrt(jnp.sum(k * k, axis=-1, keepdims=True) + EPS)

    for slot in range(DUO):
        for piece in front_pieces(slot, slot):
            piece(jnp.zeros((8, LANES), F32))
        front_finish(slot)

    def recur_read(cs):
        ss = [s_scr[h] for h in heads]
        rs = [_dot(wq_scr[cs, h], ss[h].astype(BF16)) for h in heads]
        return ss, rs

    def recur_update(cs, ss, rs, live):
        rc0 = pl.multiple_of(cs * CHUNK, CHUNK)
        vbs = []
        for h in heads:
            v_new = u_scr[h, pl.ds(rc0, CHUNK), :] - rs[h][:CHUNK]
            egl = egl_scr[cs, h][0:1, :]
            if live is not None:
                v_new = lax.select(live, v_new, jnp.zeros_like(v_new))
                egl = lax.select(live, egl, jnp.ones_like(egl))
            vbs.append(v_new.astype(BF16))
            s_scr[h] = ss[h] * egl + _dot(kdt_scr[cs, h], vbs[h])
        return [rs[h][CHUNK:] + _dot(qk_scr[cs, h], vbs[h]) for h in heads]

    def recur_finish(cs, outs):
        rc0 = pl.multiple_of(cs * CHUNK, CHUNK)
        for h in heads:
            o = outs[h]
            on = o * lax.rsqrt(jnp.mean(o * o, axis=-1, keepdims=True) + EPS) * nw_ref[...]
            z = z_ref[pl.ds(rc0, CHUNK), head_cols[h]].astype(F32)
            o_ref[pl.ds(rc0, CHUNK), head_cols[h]] = (on * _silu(z)).astype(o_ref.dtype)

    def prep(i, carry):
        first = i * DUO
        behind = jnp.maximum(first - DUO, 0)
        live = i > 0
        units = [(slot, a, b) for slot in range(DUO) for a, b in pairs]
        ns, held = [], {}
        qs, rhs = {}, {}
        zero_k = jnp.zeros((CHUNK, HEAD_DIM), BF16)
        for slot, a, b in units:
            c = first + slot
            beta_all = gate_scr[slot, 0]
            gc_all = gate_scr[slot, 1]
            gc_t = gct_scr[slot]
            side = []
            for h in (a, b):
                cols = head_cols[h]
                q = pre_scr[slot, 0, :, cols]
                k = pre_scr[slot, 1, :, cols]
                v = pre_scr[slot, 2, :, cols]
                bb = jnp.broadcast_to(beta_all[:, h:h + 1], (CHUNK, LANES))
                gcb = jnp.broadcast_to(gc_all[:, DN_HEADS + h:DN_HEADS + h + 1], (CHUNK, LANES))
                kb = k * bb
                held[slot, h] = (q, k, v, bb, gcb, kb)
                side.append((jnp.concatenate([kb, q], axis=0), k.astype(BF16), gcb,
                             gc_t[DN_HEADS + h:DN_HEADS + h + 1, :]))
            (lhs_a, k_a, gcb_a, row_a), (lhs_b, k_b, gcb_b, row_b) = side
            both = _dot_nt(jnp.concatenate([lhs_a, lhs_b], axis=1).astype(BF16),
                           jnp.concatenate([jnp.concatenate([k_a, zero_k], axis=1),
                                            jnp.concatenate([zero_k, k_b], axis=1)], axis=0))
            g_col = jnp.where(in_a, gcb_a, gcb_b)
            g_row = jnp.concatenate([row_a, row_b], axis=1)
            decay = jnp.where(ii >= jj, jnp.exp(jnp.minimum(g_col - g_row, 0.0)), 0.0)
            ns.append(jnp.where(ii > jj, -both[:CHUNK] * decay, 0.0))
            qk = (both[CHUNK:] * decay).astype(BF16)
            qk_scr[c, a] = qk[:, :CHUNK]
            qk_scr[c, b] = qk[:, CHUNK:]
            kb_last = kb

        def block_diag(m):
            return jnp.concatenate([jnp.where(in_a, m, 0.0), jnp.where(in_a, 0.0, m)],
                                   axis=0).astype(BF16)

        def late_piece(slot, h, zero):
            c = first + slot
            q, k, v, bb, gcb, kb = held[slot, h]
            gcz = gcb + zero[0:1, :]
            g_last = gcz[CHUNK - 1:CHUNK, :]
            eg = jnp.exp(gcz)
            kdt_scr[c, h] = (k * jnp.exp(g_last - gcz)).T.astype(BF16)
            egl_scr[c, h] = jnp.broadcast_to(jnp.exp(g_last), (8, LANES))
            rhs[slot, h] = jnp.concatenate([v * bb, kb * eg], axis=1).astype(BF16)
            qs[slot, h] = q * eg

        ahead = jnp.minimum(first + DUO, nc - DUO)
        late = [functools.partial(late_piece, slot, h) for slot in range(DUO) for h in heads]
        front = []
        for slot in range(DUO):
            front += front_pieces(ahead + slot, slot)
        gaps = [(late + front)[g::7] for g in range(7)]

        def fill(gap, operand):
            zero = anchor(operand)
            for piece in gaps[gap]:
                piece(zero)

        n_units = range(len(units))
        state_in = recur_read(behind)
        fill(0, kb_last)
        ps = [_dot(ns[n].astype(BF16), block_diag(ns[n])) for n in n_units]
        fill(1, ns[-1])
        xs_ = [eye + ns[n] for n in n_units]
        for level in range(4):
            operand = xs_[-1]
            for n in n_units:
                r = _dot(jnp.concatenate([xs_[n], ps[n]], axis=0).astype(BF16), block_diag(ps[n]))
                xs_[n] = xs_[n] + r[:CHUNK]
                ps[n] = r[CHUNK:]
            if level == 0:
                outs0 = recur_update(behind, *state_in, live)
            if level == 1:
                state_in = recur_read(behind + 1)
            if level == 2:
                outs1 = recur_update(behind + 1, *state_in, live)
                recur_finish(behind, outs0)
            if level == 3:
                recur_finish(behind + 1, outs1)
            fill(2 + level, operand)
        operand = xs_[-1]
        for n in n_units:
            xs_[n] = xs_[n] + _dot(xs_[n].astype(BF16), block_diag(ps[n]))
        fill(6, operand)
        zero_rhs = jnp.zeros((CHUNK, 2 * HEAD_DIM), BF16)
        for n, (slot, a, b) in enumerate(units):
            c = first + slot
            r0 = pl.multiple_of(c * CHUNK, CHUNK)
            uw = _dot(xs_[n].astype(BF16),
                      jnp.concatenate([jnp.concatenate([rhs[slot, a], zero_rhs], axis=1),
                                       jnp.concatenate([zero_rhs, rhs[slot, b]], axis=1)], axis=0))
            for h, base in ((a, 0), (b, 2 * HEAD_DIM)):
                u_scr[h, pl.ds(r0, CHUNK), :] = uw[:, base:base + HEAD_DIM]
                wq_scr[c, h] = jnp.concatenate(
                    [uw[:, base + HEAD_DIM:base + 2 * HEAD_DIM], qs[slot, h]], axis=0).astype(BF16)
        for slot in range(DUO):
            front_finish(slot)
        return carry

    lax.fori_loop(0, nc // DUO, prep, 0)
    for cs in range(nc - DUO, nc):
        recur_finish(cs, recur_update(cs, *recur_read(cs), None))

    for which, x_ref in enumerate((q_ref, k_ref, v_ref)):
        tail_scr[which] = x_ref[lb - look:lb, :].astype(F32)


def _deltanet(main, small, conv_w, alog_row, dtb_row, dn_norm_w, *, batch, seq, lb):
    t = batch * seq
    nlb = seq // lb
    nc = lb // CHUNK
    wblk = DN_WIDTH // LANES

    def col(cb):
        return pl.BlockSpec((lb, DN_WIDTH), lambda b, l, cb=cb: (b * nlb + l, cb // wblk))

    row = pl.BlockSpec((1, LANES), lambda b, l: (0, 0))
    return pl.pallas_call(
        functools.partial(_deltanet_kernel, lb=lb),
        grid=(batch, nlb),
        in_specs=[col(CB_DQ), col(CB_DK), col(CB_DV), col(CB_DZ),
                  pl.BlockSpec((lb, LANES), lambda b, l: (b * nlb + l, 0)),
                  pl.BlockSpec(conv_w.shape, lambda b, l: (0, 0)), row, row, row],
        out_specs=pl.BlockSpec((lb, DN_WIDTH), lambda b, l: (b * nlb + l, 0)),
        out_shape=jax.ShapeDtypeStruct((t, DN_WIDTH), BF16),
        scratch_shapes=[
            pltpu.VMEM((DN_HEADS, HEAD_DIM, HEAD_DIM), F32),
            pltpu.VMEM((3, 16, DN_WIDTH), F32),
            pltpu.VMEM((DN_HEADS, lb, HEAD_DIM), F32),
            pltpu.VMEM((nc, DN_HEADS, 2 * CHUNK, HEAD_DIM), BF16),
            pltpu.VMEM((nc, DN_HEADS, CHUNK, CHUNK), BF16),
            pltpu.VMEM((nc, DN_HEADS, HEAD_DIM, CHUNK), BF16),
            pltpu.VMEM((nc, DN_HEADS, 8, LANES), F32),
            pltpu.VMEM((DUO, 3, CHUNK, DN_WIDTH), F32),
            pltpu.VMEM((DUO, 2, CHUNK, LANES), F32),
            pltpu.VMEM((DUO, LANES, CHUNK), F32),
        ],
        compiler_params=_cparams(("parallel", "arbitrary")),
        name="deltanet",
    )(main, main, main, main, small, conv_w, alog_row, dtb_row, dn_norm_w)


def _attention_kernel(q_ref, k_ref, v_ref, tc_ref, ts_ref, o_ref,
                      qr_scr, kr_scr, v_scr, og_scr, lse_scr, *, seq):
    group = pl.program_id(2)
    rows = 256
    half = ROPE_DIM // 2
    lane = lax.broadcasted_iota(jnp.int32, (rows, LANES), 1)
    src_lane = lax.broadcasted_iota(jnp.int32, (LANES, LANES), 0)
    dst_lane = lax.broadcasted_iota(jnp.int32, (LANES, LANES), 1)
    want = jnp.where(dst_lane < half, dst_lane + half, jnp.where(dst_lane < ROPE_DIM, dst_lane - half, -1))
    swap = jnp.where(src_lane == want, 1.0, 0.0).astype(BF16)

    def rope(i, carry):
        r0 = pl.multiple_of(i * rows, rows)
        tc = tc_ref[pl.ds(r0, rows), :]
        ts = ts_ref[pl.ds(r0, rows), :]
        xq = q_ref[pl.ds(r0, rows), :].astype(F32)
        partner_q = jnp.where(lane < half, pltpu.roll(xq, LANES - half, axis=1),
                              pltpu.roll(xq, half, axis=1))
        qr_scr[pl.ds(r0, rows), :] = (xq * tc + partner_q * ts) * (HEAD_DIM ** -0.5)
        xk = k_ref[pl.ds(r0, rows), :]
        kr_scr[pl.ds(r0, rows), :] = xk.astype(F32) * tc + _dot(xk, swap) * ts
        v_scr[pl.ds(r0, rows), :] = v_ref[pl.ds(r0, rows), :].astype(F32)
        return carry

    lax.fori_loop(0, seq // rows, rope, 0, unroll=4)

    ii = lax.broadcasted_iota(jnp.int32, (ATT_BLOCK, 2 * ATT_BLOCK), 0)
    jj = lax.broadcasted_iota(jnp.int32, (ATT_BLOCK, 2 * ATT_BLOCK), 1)
    dist = ii + ATT_BLOCK - jj
    band_mask = jnp.where(dist >= 0, jnp.where(dist <= ATT_BLOCK, 0.0, NEG_BIG), NEG_BIG)
    prev_half = jnp.where(jj < ATT_BLOCK, 1.0, 0.0)
    per_iter = 8

    def run_group(gi, dil):
        nb = seq // (dil * ATT_BLOCK)
        span = dil * ATT_BLOCK

        def rows_of(start):
            if dil == 1:
                return pl.ds(start, ATT_BLOCK)
            return pl.ds(start, ATT_BLOCK, stride=dil)

        def blocks(it, carry):
            where = []
            for u in range(per_iter):
                t = it * per_iter + u
                r = lax.shift_right_logical(t, nb.bit_length() - 1)
                n = jnp.bitwise_and(t, nb - 1)
                start = r + n * span
                where.append((n, start, jnp.maximum(start - span, r)))
            run = min(nb, per_iter)
            kcur = [kr_scr[rows_of(start), :].astype(BF16) for _, start, _ in where]
            vcur = [v_scr[rows_of(start), :].astype(BF16) for _, start, _ in where]

            def prev_of(u, cur, ref):
                if u % run:
                    return cur[u - 1]
                if nb <= per_iter:
                    return cur[u]
                return ref[rows_of(where[u][2]), :].astype(BF16)

            scores = []
            for u, (n, start, prev) in enumerate(where):
                qb = qr_scr[rows_of(start), :].astype(BF16)
                kcat = jnp.concatenate([prev_of(u, kcur, kr_scr), kcur[u]], axis=0)
                s = _dot_nt(qb, kcat)
                no_prev = jnp.where(n > 0, 0.0, NEG_BIG)
                s = s + (band_mask + prev_half * no_prev)
                scores.append(s)
            probs = []
            for s in scores:
                m = jnp.max(s, axis=1, keepdims=True)
                p = jnp.exp(s - m)
                probs.append((m, p, jnp.sum(p, axis=1, keepdims=True)))
            for u, ((n, start, prev), (m, p, den)) in enumerate(zip(where, probs)):
                vcat = jnp.concatenate([prev_of(u, vcur, v_scr), vcur[u]], axis=0)
                acc = _dot(p.astype(BF16), vcat)
                og_scr[gi, rows_of(start), :] = acc / den
                lse_scr[gi, rows_of(start), :] = jnp.broadcast_to(m + jnp.log(den),
                                                                  (ATT_BLOCK, LANES))
            return carry

        lax.fori_loop(0, seq // (ATT_BLOCK * per_iter), blocks, 0)

    for gi, (_, dil) in enumerate(ATT_GROUPS):
        @pl.when(group == gi)
        def _(gi=gi, dil=dil):
            run_group(gi, dil)

    @pl.when(group == len(ATT_GROUPS) - 1)
    def _():
        def merge(i, carry):
            r0 = pl.multiple_of(i * rows, rows)
            ls = [lse_scr[g, pl.ds(r0, rows), :] for g in range(len(ATT_GROUPS))]
            mx = functools.reduce(jnp.maximum, ls)
            es = [jnp.exp(l - mx) for l in ls]
            num = functools.reduce(
                lambda a, b: a + b, [e * og_scr[g, pl.ds(r0, rows), :] for g, e in enumerate(es)])
            den = functools.reduce(lambda a, b: a + b, es)
            o_ref[pl.ds(r0, rows), :] = (num / den).astype(o_ref.dtype)
            return carry

        lax.fori_loop(0, seq // rows, merge, 0)


def _attention(main, rope_cos, rope_sin, *, batch, seq):
    t = batch * seq
    ng = len(ATT_GROUPS)

    def col(cb):
        return pl.BlockSpec((seq, HEAD_DIM),
                            lambda b, h, g, cb=cb: (b, cb + g * ATT_HEADS_PER_GROUP + h))

    tab = pl.BlockSpec((seq, LANES), lambda b, h, g: (0, 0))
    return pl.pallas_call(
        functools.partial(_attention_kernel, seq=seq),
        grid=(batch, ATT_HEADS_PER_GROUP, ng),
        in_specs=[col(CB_AQ), col(CB_AK), col(CB_AV), tab, tab],
        out_specs=pl.BlockSpec((seq, HEAD_DIM), lambda b, h, g: (b, h)),
        out_shape=jax.ShapeDtypeStruct((t, ATT_OUT_WIDTH), BF16),
        scratch_shapes=[
            pltpu.VMEM((seq, HEAD_DIM), F32),
            pltpu.VMEM((seq, HEAD_DIM), F32),
            pltpu.VMEM((seq, HEAD_DIM), F32),
            pltpu.VMEM((ng, seq, HEAD_DIM), F32),
            pltpu.VMEM((ng, seq, LANES), F32),
        ],
        compiler_params=_cparams(("parallel", "parallel", "arbitrary")),
        name="dilated_attention",
    )(main, main, main, rope_cos, rope_sin)


def _rope_tables(seq):
    half = ROPE_DIM // 2
    inv_freq = np.power(ROPE_THETA, -np.arange(half, dtype=np.float64) * (2.0 / ROPE_DIM))
    ang = np.arange(seq, dtype=np.float64)[:, None] * inv_freq[None, :]
    cos, sin = np.cos(ang), np.sin(ang)
    rest = HEAD_DIM - ROPE_DIM
    tc = np.concatenate([cos, cos, np.ones((seq, rest))], axis=1)
    ts = np.concatenate([-sin, sin, np.zeros((seq, rest))], axis=1)
    return jnp.asarray(tc, F32), jnp.asarray(ts, F32)


def _merge_ffn_kernel(x_ref, oa_ref, ob_ref, ga_ref, gb_ref, wa_ref, wb_ref, wo_ref,
                      nw_ref, wg_ref, wu_ref, wd_ref, fw_ref, o_ref, *, final_norm, tf):
    ya = _dot(oa_ref[...], wa_ref[...])
    yb = _dot(ob_ref[...], wb_ref[...])
    merged = (_sigmoid(ga_ref[...].astype(F32)) * ya + _sigmoid(gb_ref[...].astype(F32)) * yb)
    x = x_ref[...] + _dot(merged.astype(BF16), wo_ref[...])
    h = (x * lax.rsqrt(jnp.mean(x * x, axis=-1, keepdims=True) + EPS) * nw_ref[...]).astype(BF16)
    d_ff = wd_ref.shape[0]
    tiles = [slice(f0, min(f0 + tf, d_ff)) for f0 in range(0, d_ff, tf)]
    y = x
    pending = None
    for cols in tiles + [None]:
        issued = None if cols is None else (_dot(h, wg_ref[:, cols]), _dot(h, wu_ref[:, cols]))
        if pending is not None:
            pcols, (gate, up) = pending
            y = y + _dot((_silu(gate) * up).astype(BF16), wd_ref[pcols, :])
        pending = (cols, issued)
    if final_norm:
        y = y * lax.rsqrt(jnp.mean(y * y, axis=-1, keepdims=True) + EPS) * fw_ref[...]
    o_ref[...] = y


def _merge_ffn(x2, oa, ob, main, w_a, w_b, w_o, norm_w, w_gate_up, w_down, final_w, *,
               tm, tf, final_norm):
    t, d = x2.shape
    d_ff = w_down.shape[0]
    gate_cb = CB_GA * LANES // d
    once = pl.Buffered(1)

    def weight(shape, col=0):
        return pl.BlockSpec(shape, lambda i, col=col: (0, col), pipeline_mode=once)

    row = pl.BlockSpec((1, d), lambda i: (0, 0))
    return pl.pallas_call(
        functools.partial(_merge_ffn_kernel, final_norm=final_norm, tf=tf),
        grid=(t // tm,),
        in_specs=[
            pl.BlockSpec((tm, d), lambda i: (i, 0)),
            pl.BlockSpec((tm, DN_WIDTH), lambda i: (i, 0)),
            pl.BlockSpec((tm, ATT_OUT_WIDTH), lambda i: (i, 0)),
            pl.BlockSpec((tm, d), lambda i: (i, gate_cb)),
            pl.BlockSpec((tm, d), lambda i: (i, gate_cb + 1)),
            weight(w_a.shape), weight(w_b.shape), weight(w_o.shape),
            row, weight((d, d_ff)), weight((d, d_ff), 1), weight((d_ff, d)), row,
        ],
        out_specs=pl.BlockSpec((tm, d), lambda i: (i, 0)),
        out_shape=jax.ShapeDtypeStruct((t, d), F32),
        compiler_params=_cparams(("parallel",)),
        name="merge_ffn",
    )(x2, oa, ob, main, main, w_a, w_b, w_o, norm_w, w_gate_up, w_gate_up, w_down, final_w)


def _pad_lanes(v, offset):
    out = jnp.zeros((1, LANES), F32)
    return lax.dynamic_update_slice(out, v.reshape(1, -1).astype(F32), (0, offset))


def kernel(x, norm1_w, w_in, conv_w, a_log, dt_bias, dn_norm_w, w_proj_a, w_proj_b, w_out,
           norm2_w, w_gate_up, w_down, final_norm_w):
    batch, seq, d_model = x.shape
    depth = w_in.shape[0]
    t = batch * seq
    assert d_model == 8 * LANES and seq % (ATT_GROUPS[-1][1] * ATT_BLOCK) == 0
    assert w_in.shape[2] == SRC_GATES + 2 * d_model

    rope_cos, rope_sin = _rope_tables(seq)
    x2 = x.reshape(t, d_model)
    for i in range(depth):
        w_in_t = jnp.swapaxes(w_in[i], 0, 1).astype(BF16)
        main, small = _in_projection(x2, norm1_w[i].reshape(1, d_model), w_in_t,
                                     tm=PROJ_ROWS, tn=PROJ_COLS)
        oa = _deltanet(main, small, conv_w[i], _pad_lanes(a_log[i], DN_HEADS),
                       _pad_lanes(dt_bias[i], DN_HEADS), dn_norm_w[i].reshape(1, HEAD_DIM),
                       batch=batch, seq=seq, lb=DN_BLOCK)
        ob = _attention(main, rope_cos, rope_sin, batch=batch, seq=seq)
        x2 = _merge_ffn(x2, oa, ob, main, w_proj_a[i].astype(BF16), w_proj_b[i].astype(BF16),
                        w_out[i].astype(BF16), norm2_w[i].reshape(1, d_model),
                        w_gate_up[i].astype(BF16), w_down[i].astype(BF16),
                        final_norm_w.reshape(1, d_model), tm=TAIL_ROWS, tf=FFN_COLS,
                        final_norm=(i == depth - 1))
    return x2.reshape(batch, seq, d_model)
```

```python
import functools

import jax
import numpy as np
import jax.numpy as jnp
from jax import lax
from jax.experimental import pallas as pl
from jax.experimental.pallas import tpu as pltpu

F32 = jnp.float32
BF16 = jnp.bfloat16

EPS = 1e-6
LANES = 128
MXU_WIDTH = 256
HEAD_DIM = 128
DN_HEADS = 8
DN_WIDTH = DN_HEADS * HEAD_DIM
CONV_WIDTH = 4
CHUNK = 64
DUO = 2
ATT_GROUPS = ((128, 1), (512, 4), (2048, 16))
ATT_HEADS_PER_GROUP = 4
ATT_N_HEADS = len(ATT_GROUPS) * ATT_HEADS_PER_GROUP
ATT_WIDTH = ATT_N_HEADS * HEAD_DIM
ATT_OUT_WIDTH = ATT_HEADS_PER_GROUP * HEAD_DIM
ATT_BLOCK = 128
ROPE_THETA = 500000.0
ROPE_DIM = HEAD_DIM // 4
NEG_BIG = -1e30

CB_DQ, CB_DK, CB_DV, CB_DZ = 0, 8, 16, 24
CB_GA, CB_GB = 32, 40
CB_AQ, CB_AK, CB_AV = 48, 60, 72
MAIN_WIDTH = 84 * LANES

VMEM_LIMIT = 56 * 1024 * 1024

PROJ_ROWS = 512
PROJ_COLS = 6 * MXU_WIDTH
DN_BLOCK = 1024
TAIL_ROWS = 512
FFN_COLS = 3 * MXU_WIDTH


def _cparams(sem):
    return pltpu.CompilerParams(dimension_semantics=sem, vmem_limit_bytes=VMEM_LIMIT)


def _dot(a, b):
    return jnp.dot(a, b, preferred_element_type=F32)


def _dot_nt(a, b):
    return lax.dot_general(a, b, (((1,), (1,)), ((), ())), preferred_element_type=F32)


def _sigmoid(x):
    return 0.5 * jnp.tanh(0.5 * x) + 0.5


def _silu(x):
    half = 0.5 * x
    return half * (jnp.tanh(half) + 1.0)


SRC_SMALL = 4 * DN_WIDTH
SRC_ATT = SRC_SMALL + 2 * DN_HEADS
SRC_GATES = SRC_ATT + 3 * ATT_WIDTH


def _projection_tiles(tn):
    groups = ((0, 0, SRC_SMALL),
              (CB_GA * LANES, SRC_GATES, 2 * DN_WIDTH),
              (CB_AQ * LANES, SRC_ATT, 3 * ATT_WIDTH))
    return [(dst + off, src + off, min(tn, width - off))
            for dst, src, width in groups for off in range(0, width, tn)]


def _inproj_kernel(x_ref, nw_ref, wt_ref, o_ref, os_ref, *, tiles):
    x = x_ref[...]
    h = (x * lax.rsqrt(jnp.mean(x * x, axis=-1, keepdims=True) + EPS) * nw_ref[...]).astype(BF16)
    os_ref[...] = _dot_nt(h, wt_ref[SRC_SMALL:SRC_SMALL + LANES, :])
    for dst, src, width in tiles:
        o_ref[:, dst:dst + width] = _dot_nt(h, wt_ref[src:src + width, :]).astype(o_ref.dtype)


def _in_projection(x2, norm_w, w_t, *, tm, tn):
    t, d = x2.shape
    once = pl.Buffered(1)
    return pl.pallas_call(
        functools.partial(_inproj_kernel, tiles=_projection_tiles(tn)),
        grid=(t // tm,),
        in_specs=[
            pl.BlockSpec((tm, d), lambda i: (i, 0)),
            pl.BlockSpec((1, d), lambda i: (0, 0)),
            pl.BlockSpec(w_t.shape, lambda i: (0, 0), pipeline_mode=once),
        ],
        out_specs=[
            pl.BlockSpec((tm, MAIN_WIDTH), lambda i: (i, 0)),
            pl.BlockSpec((tm, LANES), lambda i: (i, 0)),
        ],
        out_shape=[
            jax.ShapeDtypeStruct((t, MAIN_WIDTH), BF16),
            jax.ShapeDtypeStruct((t, LANES), F32),
        ],
        compiler_params=_cparams(("parallel",)),
        name="in_projection",
    )(x2, norm_w, w_t)


def _deltanet_kernel(q_ref, k_ref, v_ref, z_ref, sm_ref, cw_ref, alog_ref, dtb_ref, nw_ref, o_ref,
                     s_scr, tail_scr, u_scr, wq_scr, qk_scr, kdt_scr, egl_scr,
                     pre_scr, gate_scr, gct_scr, *, lb):
    lstep = pl.program_id(1)
    nc = lb // CHUNK
    width = DN_WIDTH
    look = 16

    @pl.when(lstep == 0)
    def _():
        s_scr[...] = jnp.zeros_like(s_scr)
        tail_scr[...] = jnp.zeros_like(tail_scr)
        u_scr[:, 0:DUO * CHUNK, :] = jnp.zeros((DN_HEADS, DUO * CHUNK, HEAD_DIM), F32)
        wq_scr[0:DUO] = jnp.zeros((DUO,) + wq_scr.shape[1:], BF16)
        qk_scr[0:DUO] = jnp.zeros((DUO,) + qk_scr.shape[1:], BF16)
        kdt_scr[0:DUO] = jnp.zeros((DUO,) + kdt_scr.shape[1:], BF16)
        egl_scr[0:DUO] = jnp.zeros((DUO,) + egl_scr.shape[1:], F32)

    rowi = lax.broadcasted_iota(jnp.int32, (CHUNK, LANES), 0)
    lanei = lax.broadcasted_iota(jnp.int32, (CHUNK, LANES), 1)
    in_a = lanei < CHUNK
    ii = rowi
    jj = jnp.bitwise_and(lanei, CHUNK - 1)
    eye = (ii == jj).astype(F32)

    heads = range(DN_HEADS)
    pairs = [(h, h + 1) for h in range(0, DN_HEADS, 2)]
    head_cols = [slice(h * HEAD_DIM, (h + 1) * HEAD_DIM) for h in heads]
    srcs = (q_ref, k_ref, v_ref)

    def anchor(value):
        rows8 = value[:8, :]
        zero = jnp.where(rows8 != rows8, rows8, 0.0)
        if zero.shape[1] < LANES:
            zero = jnp.concatenate([zero] * (LANES // zero.shape[1]), axis=1)
        return zero

    def conv_piece(which, h, c, r0, slot, zero):
        x_ref, cols = srcs[which], head_cols[h]
        prev0 = pl.multiple_of(jnp.maximum(r0 - look, 0), look)
        top = lax.select(jnp.asarray(c) == 0, tail_scr[which, :, cols],
                         x_ref[pl.ds(prev0, look), cols].astype(F32))
        win = jnp.concatenate([top, x_ref[pl.ds(r0, CHUNK), cols].astype(F32)], axis=0)
        w0 = which * width + h * HEAD_DIM
        acc = None
        for j in range(CONV_WIDTH):
            back = CONV_WIDTH - 1 - j
            rows = win if back == 0 else pltpu.roll(win, back, axis=0)
            term = rows[look:, :] * (cw_ref[j:j + 1, w0:w0 + HEAD_DIM] + zero[0:1, :])
            acc = term if acc is None else acc + term
        y = _silu(acc)
        pre_scr[slot, which, :, cols] = y

    def gate_piece(r0, slot, zero):
        sm = sm_ref[pl.ds(r0, CHUNK), :]
        xs = sm + (dtb_ref[...] + zero[0:1, :])
        softplus = jnp.maximum(xs, 0.0) + jnp.log(1.0 + jnp.exp(-jnp.abs(xs)))
        gc_all = -jnp.exp(alog_ref[...]) * softplus
        shift = 1
        while shift < CHUNK:
            gc_all = gc_all + jnp.where(rowi >= shift, pltpu.roll(gc_all, shift, axis=0), 0.0)
            shift *= 2
        gate_scr[slot, 0] = _sigmoid(sm)
        gate_scr[slot, 1] = gc_all
        gct_scr[slot] = gc_all.T

    def front_pieces(c, slot):
        r0 = c * CHUNK if isinstance(c, int) else pl.multiple_of(c * CHUNK, CHUNK)
        pieces = [functools.partial(conv_piece, which, h, c, r0, slot)
                  for which in range(3) for h in heads]
        return pieces + [functools.partial(gate_piece, r0, slot)]

    def front_finish(slot):
        for h in heads:
            cols = head_cols[h]
            q = pre_scr[slot, 0, :, cols]
            k = pre_scr[slot, 1, :, cols]
            pre_scr[slot, 0, :, cols] = q * (lax.rsqrt(jnp.sum(q * q, axis=-1, keepdims=True) + EPS)
                                             * (HEAD_DIM ** -0.5))
            pre_scr[slot, 1, :, cols] = k * lax.rsqrt(jnp.sum(k * k, axis=-1, keepdims=True) + EPS)

    for slot in range(DUO):
        for piece in front_pieces(slot, slot):
            piece(jnp.zeros((8, LANES), F32))
        front_finish(slot)

    def recur_read(cs):
        ss = [s_scr[h] for h in heads]
        rs = [_dot(wq_scr[cs, h], ss[h].astype(BF16)) for h in heads]
        return ss, rs

    def recur_update(cs, ss, rs, live):
        rc0 = pl.multiple_of(cs * CHUNK, CHUNK)
        vbs = []
        for h in heads:
            v_new = u_scr[h, pl.ds(rc0, CHUNK), :] - rs[h][:CHUNK]
            egl = egl_scr[cs, h][0:1, :]
            if live is not None:
                v_new = lax.select(live, v_new, jnp.zeros_like(v_new))
                egl = lax.select(live, egl, jnp.ones_like(egl))
            vbs.append(v_new.astype(BF16))
            s_scr[h] = ss[h] * egl + _dot(kdt_scr[cs, h], vbs[h])
        return [rs[h][CHUNK:] + _dot(qk_scr[cs, h], vbs[h]) for h in heads]

    def recur_finish(cs, outs):
        rc0 = pl.multiple_of(cs * CHUNK, CHUNK)
        for h in heads:
            o = outs[h]
            on = o * lax.rsqrt(jnp.mean(o * o, axis=-1, keepdims=True) + EPS) * nw_ref[...]
            z = z_ref[pl.ds(rc0, CHUNK), head_cols[h]].astype(F32)
            o_ref[pl.ds(rc0, CHUNK), head_cols[h]] = (on * _silu(z)).astype(o_ref.dtype)

    def prep(i, carry):
        first = i * DUO
        behind = jnp.maximum(first - DUO, 0)
        live = i > 0
        units = [(slot, a, b) for slot in range(DUO) for a, b in pairs]
        ns, held = [], {}
        qs, rhs = {}, {}
        zero_k = jnp.zeros((CHUNK, HEAD_DIM), BF16)
        for slot, a, b in units:
            c = first + slot
            beta_all = gate_scr[slot, 0]
            gc_all = gate_scr[slot, 1]
            gc_t = gct_scr[slot]
            side = []
            for h in (a, b):
                cols = head_cols[h]
                q = pre_scr[slot, 0, :, cols]
                k = pre_scr[slot, 1, :, cols]
                v = pre_scr[slot, 2, :, cols]
                bb = jnp.broadcast_to(beta_all[:, h:h + 1], (CHUNK, LANES))
                gcb = jnp.broadcast_to(gc_all[:, DN_HEADS + h:DN_HEADS + h + 1], (CHUNK, LANES))
                kb = k * bb
                held[slot, h] = (q, k, v, bb, gcb, kb)
                side.append((jnp.concatenate([kb, q], axis=0), k.astype(BF16), gcb,
                             gc_t[DN_HEADS + h:DN_HEADS + h + 1, :]))
            (lhs_a, k_a, gcb_a, row_a), (lhs_b, k_b, gcb_b, row_b) = side
            both = _dot_nt(jnp.concatenate([lhs_a, lhs_b], axis=1).astype(BF16),
                           jnp.concatenate([jnp.concatenate([k_a, zero_k], axis=1),
                                            jnp.concatenate([zero_k, k_b], axis=1)], axis=0))
            g_col = jnp.where(in_a, gcb_a, gcb_b)
            g_row = jnp.concatenate([row_a, row_b], axis=1)
            decay = jnp.where(ii >= jj, jnp.exp(jnp.minimum(g_col - g_row, 0.0)), 0.0)
            ns.append(jnp.where(ii > jj, -both[:CHUNK] * decay, 0.0))
            qk = (both[CHUNK:] * decay).astype(BF16)
            qk_scr[c, a] = qk[:, :CHUNK]
            qk_scr[c, b] = qk[:, CHUNK:]
            kb_last = kb

        def block_diag(m):
            return jnp.concatenate([jnp.where(in_a, m, 0.0), jnp.where(in_a, 0.0, m)],
                                   axis=0).astype(BF16)

        def late_piece(slot, h, zero):
            c = first + slot
            q, k, v, bb, gcb, kb = held[slot, h]
            gcz = gcb + zero[0:1, :]
            g_last = gcz[CHUNK - 1:CHUNK, :]
            eg = jnp.exp(gcz)
            kdt_scr[c, h] = (k * jnp.exp(g_last - gcz)).T.astype(BF16)
            egl_scr[c, h] = jnp.broadcast_to(jnp.exp(g_last), (8, LANES))
            rhs[slot, h] = jnp.concatenate([v * bb, kb * eg], axis=1).astype(BF16)
            qs[slot, h] = q * eg

        ahead = jnp.minimum(first + DUO, nc - DUO)
        late = [functools.partial(late_piece, slot, h) for slot in range(DUO) for h in heads]
        front = []
        for slot in range(DUO):
            front += front_pieces(ahead + slot, slot)
        gaps = [(late + front)[g::7] for g in range(7)]

        def fill(gap, operand):
            zero = anchor(operand)
            for piece in gaps[gap]:
                piece(zero)

        n_units = range(len(units))
        state_in = recur_read(behind)
        fill(0, kb_last)
        ps = [_dot(ns[n].astype(BF16), block_diag(ns[n])) for n in n_units]
        fill(1, ns[-1])
        xs_ = [eye + ns[n] for n in n_units]
        for level in range(4):
            operand = xs_[-1]
            for n in n_units:
                r = _dot(jnp.concatenate([xs_[n], ps[n]], axis=0).astype(BF16), block_diag(ps[n]))
                xs_[n] = xs_[n] + r[:CHUNK]
                ps[n] = r[CHUNK:]
            if level == 0:
                outs0 = recur_update(behind, *state_in, live)
            if level == 1:
                state_in = recur_read(behind + 1)
            if level == 2:
                outs1 = recur_update(behind + 1, *state_in, live)
                recur_finish(behind, outs0)
            if level == 3:
                recur_finish(behind + 1, outs1)
            fill(2 + level, operand)
        operand = xs_[-1]
        for n in n_units:
            xs_[n] = xs_[n] + _dot(xs_[n].astype(BF16), block_diag(ps[n]))
        fill(6, operand)
        zero_rhs = jnp.zeros((CHUNK, 2 * HEAD_DIM), BF16)
        for n, (slot, a, b) in enumerate(units):
            c = first + slot
            r0 = pl.multiple_of(c * CHUNK, CHUNK)
            uw = _dot(xs_[n].astype(BF16),
                      jnp.concatenate([jnp.concatenate([rhs[slot, a], zero_rhs], axis=1),
                                       jnp.concatenate([zero_rhs, rhs[slot, b]], axis=1)], axis=0))
            for h, base in ((a, 0), (b, 2 * HEAD_DIM)):
                u_scr[h, pl.ds(r0, CHUNK), :] = uw[:, base:base + HEAD_DIM]
                wq_scr[c, h] = jnp.concatenate(
                    [uw[:, base + HEAD_DIM:base + 2 * HEAD_DIM], qs[slot, h]], axis=0).astype(BF16)
        for slot in range(DUO):
            front_finish(slot)
        return carry

    lax.fori_loop(0, nc // DUO, prep, 0)
    for cs in range(nc - DUO, nc):
        recur_finish(cs, recur_update(cs, *recur_read(cs), None))

    for which, x_ref in enumerate((q_ref, k_ref, v_ref)):
        tail_scr[which] = x_ref[lb - look:lb, :].astype(F32)


def _deltanet(main, small, conv_w, alog_row, dtb_row, dn_norm_w, *, batch, seq, lb):
    t = batch * seq
    nlb = seq // lb
    nc = lb // CHUNK
    wblk = DN_WIDTH // LANES

    def col(cb):
        return pl.BlockSpec((lb, DN_WIDTH), lambda b, l, cb=cb: (b * nlb + l, cb // wblk))

    row = pl.BlockSpec((1, LANES), lambda b, l: (0, 0))
    return pl.pallas_call(
        functools.partial(_deltanet_kernel, lb=lb),
        grid=(batch, nlb),
        in_specs=[col(CB_DQ), col(CB_DK), col(CB_DV), col(CB_DZ),
                  pl.BlockSpec((lb, LANES), lambda b, l: (b * nlb + l, 0)),
                  pl.BlockSpec(conv_w.shape, lambda b, l: (0, 0)), row, row, row],
        out_specs=pl.BlockSpec((lb, DN_WIDTH), lambda b, l: (b * nlb + l, 0)),
        out_shape=jax.ShapeDtypeStruct((t, DN_WIDTH), BF16),
        scratch_shapes=[
            pltpu.VMEM((DN_HEADS, HEAD_DIM, HEAD_DIM), F32),
            pltpu.VMEM((3, 16, DN_WIDTH), F32),
            pltpu.VMEM((DN_HEADS, lb, HEAD_DIM), F32),
            pltpu.VMEM((nc, DN_HEADS, 2 * CHUNK, HEAD_DIM), BF16),
            pltpu.VMEM((nc, DN_HEADS, CHUNK, CHUNK), BF16),
            pltpu.VMEM((nc, DN_HEADS, HEAD_DIM, CHUNK), BF16),
            pltpu.VMEM((nc, DN_HEADS, 8, LANES), F32),
            pltpu.VMEM((DUO, 3, CHUNK, DN_WIDTH), F32),
            pltpu.VMEM((DUO, 2, CHUNK, LANES), F32),
            pltpu.VMEM((DUO, LANES, CHUNK), F32),
        ],
        compiler_params=_cparams(("parallel", "arbitrary")),
        name="deltanet",
    )(main, main, main, main, small, conv_w, alog_row, dtb_row, dn_norm_w)


def _attention_kernel(q_ref, k_ref, v_ref, tc_ref, ts_ref, o_ref,
                      qr_scr, kr_scr, v_scr, og_scr, lse_scr, *, seq):
    group = pl.program_id(2)
    rows = 256
    half = ROPE_DIM // 2
    lane = lax.broadcasted_iota(jnp.int32, (rows, LANES), 1)
    src_lane = lax.broadcasted_iota(jnp.int32, (LANES, LANES), 0)
    dst_lane = lax.broadcasted_iota(jnp.int32, (LANES, LANES), 1)
    want = jnp.where(dst_lane < half, dst_lane + half, jnp.where(dst_lane < ROPE_DIM, dst_lane - half, -1))
    swap = jnp.where(src_lane == want, 1.0, 0.0).astype(BF16)

    def rope(i, carry):
        r0 = pl.multiple_of(i * rows, rows)
        tc = tc_ref[pl.ds(r0, rows), :]
        ts = ts_ref[pl.ds(r0, rows), :]
        xq = q_ref[pl.ds(r0, rows), :].astype(F32)
        partner_q = jnp.where(lane < half, pltpu.roll(xq, LANES - half, axis=1),
                              pltpu.roll(xq, half, axis=1))
        qr_scr[pl.ds(r0, rows), :] = (xq * tc + partner_q * ts) * (HEAD_DIM ** -0.5)
        xk = k_ref[pl.ds(r0, rows), :]
        kr_scr[pl.ds(r0, rows), :] = xk.astype(F32) * tc + _dot(xk, swap) * ts
        v_scr[pl.ds(r0, rows), :] = v_ref[pl.ds(r0, rows), :].astype(F32)
        return carry

    lax.fori_loop(0, seq // rows, rope, 0, unroll=4)

    ii = lax.broadcasted_iota(jnp.int32, (ATT_BLOCK, 2 * ATT_BLOCK), 0)
    jj = lax.broadcasted_iota(jnp.int32, (ATT_BLOCK, 2 * ATT_BLOCK), 1)
    dist = ii + ATT_BLOCK - jj
    band_mask = jnp.where(dist >= 0, jnp.where(dist <= ATT_BLOCK, 0.0, NEG_BIG), NEG_BIG)
    prev_half = jnp.where(jj < ATT_BLOCK, 1.0, 0.0)
    per_iter = 8

    order = _attention_order()
    stored = order[:-1]

    def run_group(gi, dil, last):
        nb = seq // (dil * ATT_BLOCK)
        span = dil * ATT_BLOCK

        def rows_of(start):
            if dil == 1:
                return pl.ds(start, ATT_BLOCK)
            return pl.ds(start, ATT_BLOCK, stride=dil)

        def blocks(it, carry):
            where = []
            for u in range(per_iter):
                t = it * per_iter + u
                r = lax.shift_right_logical(t, nb.bit_length() - 1)
                n = jnp.bitwise_and(t, nb - 1)
                start = r + n * span
                if dil == 1:
                    start = pl.multiple_of(n * span, ATT_BLOCK)
                where.append((n, start, jnp.maximum(start - span, r)))
            run = min(nb, per_iter)
            kcur = [kr_scr[rows_of(start), :].astype(BF16) for _, start, _ in where]
            vcur = [v_scr[rows_of(start), :].astype(BF16) for _, start, _ in where]

            def prev_of(u, cur, ref):
                if u % run:
                    return cur[u - 1]
                if nb <= per_iter:
                    return cur[u]
                return ref[rows_of(where[u][2]), :].astype(BF16)

            scores = []
            for u, (n, start, prev) in enumerate(where):
                qb = qr_scr[rows_of(start), :].astype(BF16)
                kcat = jnp.concatenate([prev_of(u, kcur, kr_scr), kcur[u]], axis=0)
                s = _dot_nt(qb, kcat)
                no_prev = jnp.where(n > 0, 0.0, NEG_BIG)
                s = s + (band_mask + prev_half * no_prev)
                scores.append(s)
            probs = []
            for s in scores:
                m = jnp.max(s, axis=1, keepdims=True)
                p = jnp.exp(s - m)
                probs.append((m, p, jnp.sum(p, axis=1, keepdims=True)))
            for u, ((n, start, prev), (m, p, den)) in enumerate(zip(where, probs)):
                vcat = jnp.concatenate([prev_of(u, vcur, v_scr), vcur[u]], axis=0)
                acc = _dot(p.astype(BF16), vcat)
                out = acc / den
                lse = jnp.broadcast_to(m + jnp.log(den), (ATT_BLOCK, LANES))
                if not last:
                    og_scr[gi, rows_of(start), :] = out
                    lse_scr[gi, rows_of(start), :] = lse
                    continue
                outs = [og_scr[g, rows_of(start), :] for g in stored] + [out]
                lses = [lse_scr[g, rows_of(start), :] for g in stored] + [lse]
                mx = functools.reduce(jnp.maximum, lses)
                es = [jnp.exp(l - mx) for l in lses]
                num = functools.reduce(lambda a, b: a + b, [e * o for e, o in zip(es, outs)])
                o_ref[rows_of(start), :] = (
                    num / functools.reduce(lambda a, b: a + b, es)).astype(o_ref.dtype)
            return carry

        lax.fori_loop(0, seq // (ATT_BLOCK * per_iter), blocks, 0)

    for step, gi in enumerate(order):
        @pl.when(group == step)
        def _(gi=gi, step=step):
            run_group(gi, ATT_GROUPS[gi][1], last=(step == len(order) - 1))


def _attention_order():
    order = sorted(range(len(ATT_GROUPS)), key=lambda g: -ATT_GROUPS[g][1])
    assert ATT_GROUPS[order[-1]][1] == 1
    return order


def _attention(main, rope_cos, rope_sin, *, batch, seq):
    t = batch * seq
    ng = len(ATT_GROUPS)
    order = _attention_order()
    assert order == list(range(ng - 1, -1, -1))

    def col(cb):
        return pl.BlockSpec(
            (seq, HEAD_DIM),
            lambda b, h, g, cb=cb: (b, cb + (ng - 1 - g) * ATT_HEADS_PER_GROUP + h))

    tab = pl.BlockSpec((seq, LANES), lambda b, h, g: (0, 0))
    return pl.pallas_call(
        functools.partial(_attention_kernel, seq=seq),
        grid=(batch, ATT_HEADS_PER_GROUP, ng),
        in_specs=[col(CB_AQ), col(CB_AK), col(CB_AV), tab, tab],
        out_specs=pl.BlockSpec((seq, HEAD_DIM), lambda b, h, g: (b, h)),
        out_shape=jax.ShapeDtypeStruct((t, ATT_OUT_WIDTH), BF16),
        scratch_shapes=[
            pltpu.VMEM((seq, HEAD_DIM), F32),
            pltpu.VMEM((seq, HEAD_DIM), F32),
            pltpu.VMEM((seq, HEAD_DIM), F32),
            pltpu.VMEM((ng, seq, HEAD_DIM), F32),
            pltpu.VMEM((ng, seq, LANES), F32),
        ],
        compiler_params=_cparams(("parallel", "parallel", "arbitrary")),
        name="dilated_attention",
    )(main, main, main, rope_cos, rope_sin)


def _rope_tables(seq):
    half = ROPE_DIM // 2
    inv_freq = np.power(ROPE_THETA, -np.arange(half, dtype=np.float64) * (2.0 / ROPE_DIM))
    ang = np.arange(seq, dtype=np.float64)[:, None] * inv_freq[None, :]
    cos, sin = np.cos(ang), np.sin(ang)
    rest = HEAD_DIM - ROPE_DIM
    tc = np.concatenate([cos, cos, np.ones((seq, rest))], axis=1)
    ts = np.concatenate([-sin, sin, np.zeros((seq, rest))], axis=1)
    return jnp.asarray(tc, F32), jnp.asarray(ts, F32)


def _merge_ffn_kernel(x_ref, oa_ref, ob_ref, ga_ref, gb_ref, wa_ref, wb_ref, wo_ref,
                      nw_ref, wg_ref, wu_ref, wd_ref, fw_ref, o_ref, *, final_norm, tf):
    ya = _dot(oa_ref[...], wa_ref[...])
    yb = _dot(ob_ref[...], wb_ref[...])
    merged = (_sigmoid(ga_ref[...].astype(F32)) * ya + _sigmoid(gb_ref[...].astype(F32)) * yb)
    x = x_ref[...] + _dot(merged.astype(BF16), wo_ref[...])
    h = (x * lax.rsqrt(jnp.mean(x * x, axis=-1, keepdims=True) + EPS) * nw_ref[...]).astype(BF16)
    d_ff = wd_ref.shape[0]
    tiles = [slice(f0, min(f0 + tf, d_ff)) for f0 in range(0, d_ff, tf)]
    y = x
    pending = None
    for cols in tiles + [None]:
        issued = None if cols is None else (_dot(h, wg_ref[:, cols]), _dot(h, wu_ref[:, cols]))
        if pending is not None:
            pcols, (gate, up) = pending
            y = y + _dot((_silu(gate) * up).astype(BF16), wd_ref[pcols, :])
        pending = (cols, issued)
    if final_norm:
        y = y * lax.rsqrt(jnp.mean(y * y, axis=-1, keepdims=True) + EPS) * fw_ref[...]
    o_ref[...] = y


def _merge_ffn(x2, oa, ob, main, w_a, w_b, w_o, norm_w, w_gate_up, w_down, final_w, *,
               tm, tf, final_norm):
    t, d = x2.shape
    d_ff = w_down.shape[0]
    gate_cb = CB_GA * LANES // d
    once = pl.Buffered(1)

    def weight(shape, col=0):
        return pl.BlockSpec(shape, lambda i, col=col: (0, col), pipeline_mode=once)

    row = pl.BlockSpec((1, d), lambda i: (0, 0))
    return pl.pallas_call(
        functools.partial(_merge_ffn_kernel, final_norm=final_norm, tf=tf),
        grid=(t // tm,),
        in_specs=[
            pl.BlockSpec((tm, d), lambda i: (i, 0)),
            pl.BlockSpec((tm, DN_WIDTH), lambda i: (i, 0)),
            pl.BlockSpec((tm, ATT_OUT_WIDTH), lambda i: (i, 0)),
            pl.BlockSpec((tm, d), lambda i: (i, gate_cb)),
            pl.BlockSpec((tm, d), lambda i: (i, gate_cb + 1)),
            weight(w_a.shape), weight(w_b.shape), weight(w_o.shape),
            row, weight((d, d_ff)), weight((d, d_ff), 1), weight((d_ff, d)), row,
        ],
        out_specs=pl.BlockSpec((tm, d), lambda i: (i, 0)),
        out_shape=jax.ShapeDtypeStruct((t, d), F32),
        compiler_params=_cparams(("parallel",)),
        name="merge_ffn",
    )(x2, oa, ob, main, main, w_a, w_b, w_o, norm_w, w_gate_up, w_gate_up, w_down, final_w)


def _pad_lanes(v, offset):
    out = jnp.zeros((1, LANES), F32)
    return lax.dynamic_update_slice(out, v.reshape(1, -1).astype(F32), (0, offset))


def kernel(x, norm1_w, w_in, conv_w, a_log, dt_bias, dn_norm_w, w_proj_a, w_proj_b, w_out,
           norm2_w, w_gate_up, w_down, final_norm_w):
    batch, seq, d_model = x.shape
    depth = w_in.shape[0]
    t = batch * seq
    assert d_model == 8 * LANES and seq % (ATT_GROUPS[-1][1] * ATT_BLOCK) == 0
    assert w_in.shape[2] == SRC_GATES + 2 * d_model

    rope_cos, rope_sin = _rope_tables(seq)
    x2 = x.reshape(t, d_model)
    for i in range(depth):
        w_in_t = jnp.swapaxes(w_in[i], 0, 1).astype(BF16)
        main, small = _in_projection(x2, norm1_w[i].reshape(1, d_model), w_in_t,
                                     tm=PROJ_ROWS, tn=PROJ_COLS)
        oa = _deltanet(main, small, conv_w[i], _pad_lanes(a_log[i], DN_HEADS),
                       _pad_lanes(dt_bias[i], DN_HEADS), dn_norm_w[i].reshape(1, HEAD_DIM),
                       batch=batch, seq=seq, lb=DN_BLOCK)
        ob = _attention(main, rope_cos, rope_sin, batch=batch, seq=seq)
        x2 = _merge_ffn(x2, oa, ob, main, w_proj_a[i].astype(BF16), w_proj_b[i].astype(BF16),
                        w_out[i].astype(BF16), norm2_w[i].reshape(1, d_model),
                        w_gate_up[i].astype(BF16), w_down[i].astype(BF16),
                        final_norm_w.reshape(1, d_model), tm=TAIL_ROWS, tf=FFN_COLS,
                        final_norm=(i == depth - 1))
    return x2.reshape(batch, seq, d_model)
```

```python
import functools

import jax
import numpy as np
import jax.numpy as jnp
from jax import lax
from jax.experimental import pallas as pl
from jax.experimental.pallas import tpu as pltpu

F32 = jnp.float32
BF16 = jnp.bfloat16

EPS = 1e-6
LANES = 128
MXU_WIDTH = 256
HEAD_DIM = 128
DN_HEADS = 8
DN_WIDTH = DN_HEADS * HEAD_DIM
CONV_WIDTH = 4
CHUNK = 64
DUO = 2
ATT_GROUPS = ((128, 1), (512, 4), (2048, 16))
ATT_HEADS_PER_GROUP = 4
ATT_N_HEADS = len(ATT_GROUPS) * ATT_HEADS_PER_GROUP
ATT_WIDTH = ATT_N_HEADS * HEAD_DIM
ATT_OUT_WIDTH = ATT_HEADS_PER_GROUP * HEAD_DIM
ATT_BLOCK = 128
ROPE_THETA = 500000.0
ROPE_DIM = HEAD_DIM // 4
NEG_BIG = -1e30

CB_DQ, CB_DK, CB_DV, CB_DZ = 0, 8, 16, 24
CB_GA, CB_GB = 32, 40
CB_AQ, CB_AK, CB_AV = 48, 60, 72
MAIN_WIDTH = 84 * LANES

VMEM_LIMIT = 56 * 1024 * 1024

PROJ_ROWS = 512
PROJ_COLS = 6 * MXU_WIDTH
DN_BLOCK = 1024
TAIL_ROWS = 512
FFN_COLS = 3 * MXU_WIDTH


def _cparams(sem):
    return pltpu.CompilerParams(dimension_semantics=sem, vmem_limit_bytes=VMEM_LIMIT)


def _dot(a, b):
    return jnp.dot(a, b, preferred_element_type=F32)


def _dot_nt(a, b):
    return lax.dot_general(a, b, (((1,), (1,)), ((), ())), preferred_element_type=F32)


def _sigmoid(x):
    return 0.5 * jnp.tanh(0.5 * x) + 0.5


def _silu(x):
    half = 0.5 * x
    return half * (jnp.tanh(half) + 1.0)


def _split_bf16(x):
    hi = x.astype(BF16)
    lo = (x - hi.astype(F32)).astype(BF16)
    return jnp.concatenate([hi, lo], axis=1)


SRC_SMALL = 4 * DN_WIDTH
SRC_ATT = SRC_SMALL + 2 * DN_HEADS
SRC_GATES = SRC_ATT + 3 * ATT_WIDTH


def _projection_tiles(tn):
    groups = ((0, 0, SRC_SMALL),
              (CB_GA * LANES, SRC_GATES, 2 * DN_WIDTH),
              (CB_AQ * LANES, SRC_ATT, 3 * ATT_WIDTH))
    return [(dst + off, src + off, min(tn, width - off))
            for dst, src, width in groups for off in range(0, width, tn)]


def _inproj_kernel(x_ref, nw_ref, wt_ref, o_ref, os_ref, *, tiles):
    x = x_ref[...]
    h = (x * lax.rsqrt(jnp.mean(x * x, axis=-1, keepdims=True) + EPS) * nw_ref[...]).astype(BF16)
    os_ref[...] = _dot_nt(h, wt_ref[SRC_SMALL:SRC_SMALL + LANES, :])
    for dst, src, width in tiles:
        o_ref[:, dst:dst + width] = _dot_nt(h, wt_ref[src:src + width, :]).astype(o_ref.dtype)


def _in_projection(x2, norm_w, w_t, *, tm, tn):
    t, d = x2.shape
    once = pl.Buffered(1)
    return pl.pallas_call(
        functools.partial(_inproj_kernel, tiles=_projection_tiles(tn)),
        grid=(t // tm,),
        in_specs=[
            pl.BlockSpec((tm, d), lambda i: (i, 0)),
            pl.BlockSpec((1, d), lambda i: (0, 0)),
            pl.BlockSpec(w_t.shape, lambda i: (0, 0), pipeline_mode=once),
        ],
        out_specs=[
            pl.BlockSpec((tm, MAIN_WIDTH), lambda i: (i, 0)),
            pl.BlockSpec((tm, LANES), lambda i: (i, 0)),
        ],
        out_shape=[
            jax.ShapeDtypeStruct((t, MAIN_WIDTH), BF16),
            jax.ShapeDtypeStruct((t, LANES), F32),
        ],
        compiler_params=_cparams(("parallel",)),
        name="in_projection",
    )(x2, norm_w, w_t)


def _deltanet_kernel(q_ref, k_ref, v_ref, z_ref, sm_ref, cw_ref, alog_ref, dtb_ref, nw_ref, o_ref,
                     s_scr, tail_scr, u_scr, wq_scr, qk_scr, kdt_scr, egl_scr,
                     pre_scr, gate_scr, gct_scr, sq_scr, *, lb):
    lstep = pl.program_id(1)
    nc = lb // CHUNK
    width = DN_WIDTH
    look = 16

    @pl.when(lstep == 0)
    def _():
        s_scr[...] = jnp.zeros_like(s_scr)
        tail_scr[...] = jnp.zeros_like(tail_scr)
        u_scr[:, 0:DUO * CHUNK, :] = jnp.zeros((DN_HEADS, DUO * CHUNK, HEAD_DIM), F32)
        wq_scr[0:DUO] = jnp.zeros((DUO,) + wq_scr.shape[1:], BF16)
        qk_scr[0:DUO] = jnp.zeros((DUO,) + qk_scr.shape[1:], BF16)
        kdt_scr[0:DUO] = jnp.zeros((DUO,) + kdt_scr.shape[1:], BF16)
        egl_scr[0:DUO] = jnp.zeros((DUO,) + egl_scr.shape[1:], F32)

    rowi = lax.broadcasted_iota(jnp.int32, (CHUNK, LANES), 0)
    lanei = lax.broadcasted_iota(jnp.int32, (CHUNK, LANES), 1)
    in_a = lanei < CHUNK
    ii = rowi
    jj = jnp.bitwise_and(lanei, CHUNK - 1)
    eye = (ii == jj).astype(F32)

    heads = range(DN_HEADS)
    pairs = [(h, h + 1) for h in range(0, DN_HEADS, 2)]
    head_cols = [slice(h * HEAD_DIM, (h + 1) * HEAD_DIM) for h in heads]
    srcs = (q_ref, k_ref, v_ref)
    ones = jnp.ones((2 * HEAD_DIM, HEAD_DIM), BF16)

    def anchor(value):
        rows8 = value[:8, :]
        zero = jnp.where(rows8 != rows8, rows8, 0.0)
        if zero.shape[1] < LANES:
            zero = jnp.concatenate([zero] * (LANES // zero.shape[1]), axis=1)
        return zero

    def conv_piece(which, h, c, r0, slot, zero):
        x_ref, cols = srcs[which], head_cols[h]
        prev0 = pl.multiple_of(jnp.maximum(r0 - look, 0), look)
        top = lax.select(jnp.asarray(c) == 0, tail_scr[which, :, cols],
                         x_ref[pl.ds(prev0, look), cols].astype(F32))
        win = jnp.concatenate([top, x_ref[pl.ds(r0, CHUNK), cols].astype(F32)], axis=0)
        w0 = which * width + h * HEAD_DIM
        acc = None
        for j in range(CONV_WIDTH):
            back = CONV_WIDTH - 1 - j
            rows = win if back == 0 else pltpu.roll(win, back, axis=0)
            term = rows[look:, :] * (cw_ref[j:j + 1, w0:w0 + HEAD_DIM] + zero[0:1, :])
            acc = term if acc is None else acc + term
        y = _silu(acc)
        pre_scr[slot, which, :, cols] = y
        if which < 2:
            sq_scr[slot, (which * DN_HEADS + h) * CHUNK:(which * DN_HEADS + h + 1) * CHUNK, :] = y * y

    def gate_piece(r0, slot, zero):
        sm = sm_ref[pl.ds(r0, CHUNK), :]
        xs = sm + (dtb_ref[...] + zero[0:1, :])
        softplus = jnp.maximum(xs, 0.0) + jnp.log(1.0 + jnp.exp(-jnp.abs(xs)))
        gc_all = -jnp.exp(alog_ref[...]) * softplus
        shift = 1
        while shift < CHUNK:
            gc_all = gc_all + jnp.where(rowi >= shift, pltpu.roll(gc_all, shift, axis=0), 0.0)
            shift *= 2
        gate_scr[slot, 0] = _sigmoid(sm)
        gate_scr[slot, 1] = gc_all
        gct_scr[slot] = gc_all.T

    def front_pieces(c, slot):
        r0 = c * CHUNK if isinstance(c, int) else pl.multiple_of(c * CHUNK, CHUNK)
        pieces = [functools.partial(conv_piece, which, h, c, r0, slot)
                  for which in range(3) for h in heads]
        return pieces + [functools.partial(gate_piece, r0, slot)]

    def front_finish(slot):
        inv_norm = lax.rsqrt(_dot(_split_bf16(sq_scr[slot]), ones) + EPS)
        for h in heads:
            cols = head_cols[h]
            pre_scr[slot, 0, :, cols] = pre_scr[slot, 0, :, cols] * (
                inv_norm[h * CHUNK:(h + 1) * CHUNK] * (HEAD_DIM ** -0.5))
            pre_scr[slot, 1, :, cols] = pre_scr[slot, 1, :, cols] * inv_norm[
                (DN_HEADS + h) * CHUNK:(DN_HEADS + h + 1) * CHUNK]

    for slot in range(DUO):
        for piece in front_pieces(slot, slot):
            piece(jnp.zeros((8, LANES), F32))
        front_finish(slot)

    def recur_read(cs):
        ss = [s_scr[h] for h in heads]
        rs = [_dot(wq_scr[cs, h], ss[h].astype(BF16)) for h in heads]
        return ss, rs

    def recur_update(cs, ss, rs, live):
        rc0 = pl.multiple_of(cs * CHUNK, CHUNK)
        vbs = []
        for h in heads:
            v_new = u_scr[h, pl.ds(rc0, CHUNK), :] - rs[h][:CHUNK]
            egl = egl_scr[cs, h][0:1, :]
            if live is not None:
                v_new = lax.select(live, v_new, jnp.zeros_like(v_new))
                egl = lax.select(live, egl, jnp.ones_like(egl))
            vbs.append(v_new.astype(BF16))
            s_scr[h] = ss[h] * egl + _dot(kdt_scr[cs, h], vbs[h])
        return [rs[h][CHUNK:] + _dot(qk_scr[cs, h], vbs[h]) for h in heads]

    def recur_finish(cs, outs):
        rc0 = pl.multiple_of(cs * CHUNK, CHUNK)
        for h in heads:
            o = outs[h]
            on = o * lax.rsqrt(jnp.mean(o * o, axis=-1, keepdims=True) + EPS) * nw_ref[...]
            z = z_ref[pl.ds(rc0, CHUNK), head_cols[h]].astype(F32)
            o_ref[pl.ds(rc0, CHUNK), head_cols[h]] = (on * _silu(z)).astype(o_ref.dtype)

    def prep(i, carry):
        first = i * DUO
        behind = jnp.maximum(first - DUO, 0)
        live = i > 0
        units = [(slot, a, b) for slot in range(DUO) for a, b in pairs]
        ns, held = [], {}
        qs, rhs = {}, {}
        zero_k = jnp.zeros((CHUNK, HEAD_DIM), BF16)
        for slot, a, b in units:
            c = first + slot
            beta_all = gate_scr[slot, 0]
            gc_all = gate_scr[slot, 1]
            gc_t = gct_scr[slot]
            side = []
            for h in (a, b):
                cols = head_cols[h]
                q = pre_scr[slot, 0, :, cols]
                k = pre_scr[slot, 1, :, cols]
                v = pre_scr[slot, 2, :, cols]
                bb = jnp.broadcast_to(beta_all[:, h:h + 1], (CHUNK, LANES))
                gcb = jnp.broadcast_to(gc_all[:, DN_HEADS + h:DN_HEADS + h + 1], (CHUNK, LANES))
                kb = k * bb
                held[slot, h] = (q, k, v, bb, gcb, kb)
                side.append((jnp.concatenate([kb, q], axis=0), k.astype(BF16), gcb,
                             gc_t[DN_HEADS + h:DN_HEADS + h + 1, :]))
            (lhs_a, k_a, gcb_a, row_a), (lhs_b, k_b, gcb_b, row_b) = side
            both = _dot_nt(jnp.concatenate([lhs_a, lhs_b], axis=1).astype(BF16),
                           jnp.concatenate([jnp.concatenate([k_a, zero_k], axis=1),
                                            jnp.concatenate([zero_k, k_b], axis=1)], axis=0))
            g_col = jnp.where(in_a, gcb_a, gcb_b)
            g_row = jnp.concatenate([row_a, row_b], axis=1)
            decay = jnp.where(ii >= jj, jnp.exp(jnp.minimum(g_col - g_row, 0.0)), 0.0)
            ns.append(jnp.where(ii > jj, -both[:CHUNK] * decay, 0.0))
            qk = (both[CHUNK:] * decay).astype(BF16)
            qk_scr[c, a] = qk[:, :CHUNK]
            qk_scr[c, b] = qk[:, CHUNK:]
            kb_last = kb

        def block_diag(m):
            return jnp.concatenate([jnp.where(in_a, m, 0.0), jnp.where(in_a, 0.0, m)],
                                   axis=0).astype(BF16)

        def late_piece(slot, h, zero):
            c = first + slot
            q, k, v, bb, gcb, kb = held[slot, h]
            gcz = gcb + zero[0:1, :]
            g_last = gcz[CHUNK - 1:CHUNK, :]
            eg = jnp.exp(gcz)
            kdt_scr[c, h] = (k * jnp.exp(g_last - gcz)).T.astype(BF16)
            egl_scr[c, h] = jnp.broadcast_to(jnp.exp(g_last), (8, LANES))
            rhs[slot, h] = jnp.concatenate([v * bb, kb * eg], axis=1).astype(BF16)
            qs[slot, h] = q * eg

        ahead = jnp.minimum(first + DUO, nc - DUO)
        late = [functools.partial(late_piece, slot, h) for slot in range(DUO) for h in heads]
        front = []
        for slot in range(DUO):
            front += front_pieces(ahead + slot, slot)
        gaps = [(late + front)[g::7] for g in range(7)]

        def fill(gap, operand):
            zero = anchor(operand)
            for piece in gaps[gap]:
                piece(zero)

        n_units = range(len(units))
        state_in = recur_read(behind)
        fill(0, kb_last)
        ps = [_dot(ns[n].astype(BF16), block_diag(ns[n])) for n in n_units]
        fill(1, ns[-1])
        xs_ = [eye + ns[n] for n in n_units]
        for level in range(4):
            operand = xs_[-1]
            for n in n_units:
                r = _dot(jnp.concatenate([xs_[n], ps[n]], axis=0).astype(BF16), block_diag(ps[n]))
                xs_[n] = xs_[n] + r[:CHUNK]
                ps[n] = r[CHUNK:]
            if level == 0:
                outs0 = recur_update(behind, *state_in, live)
            if level == 1:
                state_in = recur_read(behind + 1)
            if level == 2:
                outs1 = recur_update(behind + 1, *state_in, live)
                recur_finish(behind, outs0)
            if level == 3:
                recur_finish(behind + 1, outs1)
            fill(2 + level, operand)
        operand = xs_[-1]
        for n in n_units:
            xs_[n] = xs_[n] + _dot(xs_[n].astype(BF16), block_diag(ps[n]))
        fill(6, operand)
        zero_rhs = jnp.zeros((CHUNK, 2 * HEAD_DIM), BF16)
        for n, (slot, a, b) in enumerate(units):
            c = first + slot
            r0 = pl.multiple_of(c * CHUNK, CHUNK)
            uw = _dot(xs_[n].astype(BF16),
                      jnp.concatenate([jnp.concatenate([rhs[slot, a], zero_rhs], axis=1),
                                       jnp.concatenate([zero_rhs, rhs[slot, b]], axis=1)], axis=0))
            for h, base in ((a, 0), (b, 2 * HEAD_DIM)):
                u_scr[h, pl.ds(r0, CHUNK), :] = uw[:, base:base + HEAD_DIM]
                wq_scr[c, h] = jnp.concatenate(
                    [uw[:, base + HEAD_DIM:base + 2 * HEAD_DIM], qs[slot, h]], axis=0).astype(BF16)
        for slot in range(DUO):
            front_finish(slot)
        return carry

    lax.fori_loop(0, nc // DUO, prep, 0)
    for cs in range(nc - DUO, nc):
        recur_finish(cs, recur_update(cs, *recur_read(cs), None))

    for which, x_ref in enumerate((q_ref, k_ref, v_ref)):
        tail_scr[which] = x_ref[lb - look:lb, :].astype(F32)


def _deltanet(main, small, conv_w, alog_row, dtb_row, dn_norm_w, *, batch, seq, lb):
    t = batch * seq
    nlb = seq // lb
    nc = lb // CHUNK
    wblk = DN_WIDTH // LANES

    def col(cb):
        return pl.BlockSpec((lb, DN_WIDTH), lambda b, l, cb=cb: (b * nlb + l, cb // wblk))

    row = pl.BlockSpec((1, LANES), lambda b, l: (0, 0))
    return pl.pallas_call(
        functools.partial(_deltanet_kernel, lb=lb),
        grid=(batch, nlb),
        in_specs=[col(CB_DQ), col(CB_DK), col(CB_DV), col(CB_DZ),
                  pl.BlockSpec((lb, LANES), lambda b, l: (b * nlb + l, 0)),
                  pl.BlockSpec(conv_w.shape, lambda b, l: (0, 0)), row, row, row],
        out_specs=pl.BlockSpec((lb, DN_WIDTH), lambda b, l: (b * nlb + l, 0)),
        out_shape=jax.ShapeDtypeStruct((t, DN_WIDTH), BF16),
        scratch_shapes=[
            pltpu.VMEM((DN_HEADS, HEAD_DIM, HEAD_DIM), F32),
            pltpu.VMEM((3, 16, DN_WIDTH), F32),
            pltpu.VMEM((DN_HEADS, lb, HEAD_DIM), F32),
            pltpu.VMEM((nc, DN_HEADS, 2 * CHUNK, HEAD_DIM), BF16),
            pltpu.VMEM((nc, DN_HEADS, CHUNK, CHUNK), BF16),
            pltpu.VMEM((nc, DN_HEADS, HEAD_DIM, CHUNK), BF16),
            pltpu.VMEM((nc, DN_HEADS, 8, LANES), F32),
            pltpu.VMEM((DUO, 3, CHUNK, DN_WIDTH), F32),
            pltpu.VMEM((DUO, 2, CHUNK, LANES), F32),
            pltpu.VMEM((DUO, LANES, CHUNK), F32),
            pltpu.VMEM((DUO, 2 * DN_HEADS * CHUNK, HEAD_DIM), F32),
        ],
        compiler_params=_cparams(("parallel", "arbitrary")),
        name="deltanet",
    )(main, main, main, main, small, conv_w, alog_row, dtb_row, dn_norm_w)


def _attention_kernel(q_ref, k_ref, v_ref, tc_ref, ts_ref, o_ref,
                      qr_scr, kr_scr, v_scr, og_scr, lse_scr, *, seq):
    group = pl.program_id(2)
    rows = 256
    half = ROPE_DIM // 2
    lane = lax.broadcasted_iota(jnp.int32, (rows, LANES), 1)
    src_lane = lax.broadcasted_iota(jnp.int32, (LANES, LANES), 0)
    dst_lane = lax.broadcasted_iota(jnp.int32, (LANES, LANES), 1)
    want = jnp.where(dst_lane < half, dst_lane + half, jnp.where(dst_lane < ROPE_DIM, dst_lane - half, -1))
    swap = jnp.where(src_lane == want, 1.0, 0.0).astype(BF16)

    def rope(i, carry):
        r0 = pl.multiple_of(i * rows, rows)
        tc = tc_ref[pl.ds(r0, rows), :]
        ts = ts_ref[pl.ds(r0, rows), :]
        xq = q_ref[pl.ds(r0, rows), :].astype(F32)
        partner_q = jnp.where(lane < half, pltpu.roll(xq, LANES - half, axis=1),
                              pltpu.roll(xq, half, axis=1))
        qr_scr[pl.ds(r0, rows), :] = (xq * tc + partner_q * ts) * (HEAD_DIM ** -0.5)
        xk = k_ref[pl.ds(r0, rows), :]
        kr_scr[pl.ds(r0, rows), :] = xk.astype(F32) * tc + _dot(xk, swap) * ts
        v_scr[pl.ds(r0, rows), :] = v_ref[pl.ds(r0, rows), :].astype(F32)
        return carry

    lax.fori_loop(0, seq // rows, rope, 0, unroll=4)

    ii = lax.broadcasted_iota(jnp.int32, (ATT_BLOCK, 2 * ATT_BLOCK), 0)
    jj = lax.broadcasted_iota(jnp.int32, (ATT_BLOCK, 2 * ATT_BLOCK), 1)
    dist = ii + ATT_BLOCK - jj
    band_mask = jnp.where(dist >= 0, jnp.where(dist <= ATT_BLOCK, 0.0, NEG_BIG), NEG_BIG)
    prev_half = jnp.where(jj < ATT_BLOCK, 1.0, 0.0)
    per_iter = 8

    order = _attention_order()
    stored = order[:-1]

    def run_group(gi, dil, last):
        nb = seq // (dil * ATT_BLOCK)
        span = dil * ATT_BLOCK

        def rows_of(start):
            if dil == 1:
                return pl.ds(start, ATT_BLOCK)
            return pl.ds(start, ATT_BLOCK, stride=dil)

        def blocks(it, carry):
            where = []
            for u in range(per_iter):
                t = it * per_iter + u
                r = lax.shift_right_logical(t, nb.bit_length() - 1)
                n = jnp.bitwise_and(t, nb - 1)
                start = r + n * span
                if dil == 1:
                    start = pl.multiple_of(n * span, ATT_BLOCK)
                where.append((n, start, jnp.maximum(start - span, r)))
            run = min(nb, per_iter)
            kcur = [kr_scr[rows_of(start), :].astype(BF16) for _, start, _ in where]
            vcur = [v_scr[rows_of(start), :].astype(BF16) for _, start, _ in where]

            def prev_of(u, cur, ref):
                if u % run:
                    return cur[u - 1]
                if nb <= per_iter:
                    return cur[u]
                return ref[rows_of(where[u][2]), :].astype(BF16)

            scores = []
            for u, (n, start, prev) in enumerate(where):
                qb = qr_scr[rows_of(start), :].astype(BF16)
                kcat = jnp.concatenate([prev_of(u, kcur, kr_scr), kcur[u]], axis=0)
                s = _dot_nt(qb, kcat)
                no_prev = jnp.where(n > 0, 0.0, NEG_BIG)
                s = s + (band_mask + prev_half * no_prev)
                scores.append(s)
            probs = []
            for s in scores:
                m = jnp.max(s, axis=1, keepdims=True)
                p = jnp.exp(s - m)
                probs.append((m, p, jnp.sum(p, axis=1, keepdims=True)))
            for u, ((n, start, prev), (m, p, den)) in enumerate(zip(where, probs)):
                vcat = jnp.concatenate([prev_of(u, vcur, v_scr), vcur[u]], axis=0)
                acc = _dot(p.astype(BF16), vcat)
                out = acc / den
                lse = jnp.broadcast_to(m + jnp.log(den), (ATT_BLOCK, LANES))
                if not last:
                    og_scr[gi, rows_of(start), :] = out
                    lse_scr[gi, rows_of(start), :] = lse
                    continue
                outs = [og_scr[g, rows_of(start), :] for g in stored] + [out]
                lses = [lse_scr[g, rows_of(start), :] for g in stored] + [lse]
                mx = functools.reduce(jnp.maximum, lses)
                es = [jnp.exp(l - mx) for l in lses]
                num = functools.reduce(lambda a, b: a + b, [e * o for e, o in zip(es, outs)])
                o_ref[rows_of(start), :] = (
                    num / functools.reduce(lambda a, b: a + b, es)).astype(o_ref.dtype)
            return carry

        lax.fori_loop(0, seq // (ATT_BLOCK * per_iter), blocks, 0)

    for step, gi in enumerate(order):
        @pl.when(group == step)
        def _(gi=gi, step=step):
            run_group(gi, ATT_GROUPS[gi][1], last=(step == len(order) - 1))


def _attention_order():
    order = sorted(range(len(ATT_GROUPS)), key=lambda g: -ATT_GROUPS[g][1])
    assert ATT_GROUPS[order[-1]][1] == 1
    return order


def _attention(main, rope_cos, rope_sin, *, batch, seq):
    t = batch * seq
    ng = len(ATT_GROUPS)
    order = _attention_order()
    assert order == list(range(ng - 1, -1, -1))

    def col(cb):
        return pl.BlockSpec(
            (seq, HEAD_DIM),
            lambda b, h, g, cb=cb: (b, cb + (ng - 1 - g) * ATT_HEADS_PER_GROUP + h))

    tab = pl.BlockSpec((seq, LANES), lambda b, h, g: (0, 0))
    return pl.pallas_call(
        functools.partial(_attention_kernel, seq=seq),
        grid=(batch, ATT_HEADS_PER_GROUP, ng),
        in_specs=[col(CB_AQ), col(CB_AK), col(CB_AV), tab, tab],
        out_specs=pl.BlockSpec((seq, HEAD_DIM), lambda b, h, g: (b, h)),
        out_shape=jax.ShapeDtypeStruct((t, ATT_OUT_WIDTH), BF16),
        scratch_shapes=[
            pltpu.VMEM((seq, HEAD_DIM), F32),
            pltpu.VMEM((seq, HEAD_DIM), F32),
            pltpu.VMEM((seq, HEAD_DIM), F32),
            pltpu.VMEM((ng, seq, HEAD_DIM), F32),
            pltpu.VMEM((ng, seq, LANES), F32),
        ],
        compiler_params=_cparams(("parallel", "parallel", "arbitrary")),
        name="dilated_attention",
    )(main, main, main, rope_cos, rope_sin)


def _rope_tables(seq):
    half = ROPE_DIM // 2
    inv_freq = np.power(ROPE_THETA, -np.arange(half, dtype=np.float64) * (2.0 / ROPE_DIM))
    ang = np.arange(seq, dtype=np.float64)[:, None] * inv_freq[None, :]
    cos, sin = np.cos(ang), np.sin(ang)
    rest = HEAD_DIM - ROPE_DIM
    tc = np.concatenate([cos, cos, np.ones((seq, rest))], axis=1)
    ts = np.concatenate([-sin, sin, np.zeros((seq, rest))], axis=1)
    return jnp.asarray(tc, F32), jnp.asarray(ts, F32)


def _merge_ffn_kernel(x_ref, oa_ref, ob_ref, ga_ref, gb_ref, wa_ref, wb_ref, wo_ref,
                      nw_ref, wg_ref, wu_ref, wd_ref, fw_ref, o_ref, *, final_norm, tf):
    ya = _dot(oa_ref[...], wa_ref[...])
    yb = _dot(ob_ref[...], wb_ref[...])
    merged = (_sigmoid(ga_ref[...].astype(F32)) * ya + _sigmoid(gb_ref[...].astype(F32)) * yb)
    x = x_ref[...] + _dot(merged.astype(BF16), wo_ref[...])
    h = (x * lax.rsqrt(jnp.mean(x * x, axis=-1, keepdims=True) + EPS) * nw_ref[...]).astype(BF16)
    d_ff = wd_ref.shape[0]
    tiles = [slice(f0, min(f0 + tf, d_ff)) for f0 in range(0, d_ff, tf)]
    y = x
    pending = None
    for cols in tiles + [None]:
        issued = None if cols is None else (_dot(h, wg_ref[:, cols]), _dot(h, wu_ref[:, cols]))
        if pending is not None:
            pcols, (gate, up) = pending
            y = y + _dot((_silu(gate) * up).astype(BF16), wd_ref[pcols, :])
        pending = (cols, issued)
    if final_norm:
        y = y * lax.rsqrt(jnp.mean(y * y, axis=-1, keepdims=True) + EPS) * fw_ref[...]
    o_ref[...] = y


def _merge_ffn(x2, oa, ob, main, w_a, w_b, w_o, norm_w, w_gate_up, w_down, final_w, *,
               tm, tf, final_norm):
    t, d = x2.shape
    d_ff = w_down.shape[0]
    gate_cb = CB_GA * LANES // d
    once = pl.Buffered(1)

    def weight(shape, col=0):
        return pl.BlockSpec(shape, lambda i, col=col: (0, col), pipeline_mode=once)

    row = pl.BlockSpec((1, d), lambda i: (0, 0))
    return pl.pallas_call(
        functools.partial(_merge_ffn_kernel, final_norm=final_norm, tf=tf),
        grid=(t // tm,),
        in_specs=[
            pl.BlockSpec((tm, d), lambda i: (i, 0)),
            pl.BlockSpec((tm, DN_WIDTH), lambda i: (i, 0)),
            pl.BlockSpec((tm, ATT_OUT_WIDTH), lambda i: (i, 0)),
            pl.BlockSpec((tm, d), lambda i: (i, gate_cb)),
            pl.BlockSpec((tm, d), lambda i: (i, gate_cb + 1)),
            weight(w_a.shape), weight(w_b.shape), weight(w_o.shape),
            row, weight((d, d_ff)), weight((d, d_ff), 1), weight((d_ff, d)), row,
        ],
        out_specs=pl.BlockSpec((tm, d), lambda i: (i, 0)),
        out_shape=jax.ShapeDtypeStruct((t, d), F32),
        compiler_params=_cparams(("parallel",)),
        name="merge_ffn",
    )(x2, oa, ob, main, main, w_a, w_b, w_o, norm_w, w_gate_up, w_gate_up, w_down, final_w)


def _pad_lanes(v, offset):
    out = jnp.zeros((1, LANES), F32)
    return lax.dynamic_update_slice(out, v.reshape(1, -1).astype(F32), (0, offset))


def kernel(x, norm1_w, w_in, conv_w, a_log, dt_bias, dn_norm_w, w_proj_a, w_proj_b, w_out,
           norm2_w, w_gate_up, w_down, final_norm_w):
    batch, seq, d_model = x.shape
    depth = w_in.shape[0]
    t = batch * seq
    assert d_model == 8 * LANES and seq % (ATT_GROUPS[-1][1] * ATT_BLOCK) == 0
    assert w_in.shape[2] == SRC_GATES + 2 * d_model

    rope_cos, rope_sin = _rope_tables(seq)
    x2 = x.reshape(t, d_model)
    for i in range(depth):
        w_in_t = jnp.swapaxes(w_in[i], 0, 1).astype(BF16)
        main, small = _in_projection(x2, norm1_w[i].reshape(1, d_model), w_in_t,
                                     tm=PROJ_ROWS, tn=PROJ_COLS)
        oa = _deltanet(main, small, conv_w[i], _pad_lanes(a_log[i], DN_HEADS),
                       _pad_lanes(dt_bias[i], DN_HEADS), dn_norm_w[i].reshape(1, HEAD_DIM),
                       batch=batch, seq=seq, lb=DN_BLOCK)
        ob = _attention(main, rope_cos, rope_sin, batch=batch, seq=seq)
        x2 = _merge_ffn(x2, oa, ob, main, w_proj_a[i].astype(BF16), w_proj_b[i].astype(BF16),
                        w_out[i].astype(BF16), norm2_w[i].reshape(1, d_model),
                        w_gate_up[i].astype(BF16), w_down[i].astype(BF16),
                        final_norm_w.reshape(1, d_model), tm=TAIL_ROWS, tf=FFN_COLS,
                        final_norm=(i == depth - 1))
    return x2.reshape(batch, seq, d_model)
```

```python
import functools

import jax
import numpy as np
import jax.numpy as jnp
from jax import lax
from jax.experimental import pallas as pl
from jax.experimental.pallas import tpu as pltpu

F32 = jnp.float32
BF16 = jnp.bfloat16

EPS = 1e-6
LANES = 128
MXU_WIDTH = 256
HEAD_DIM = 128
DN_HEADS = 8
DN_WIDTH = DN_HEADS * HEAD_DIM
CONV_WIDTH = 4
CHUNK = 64
DUO = 2
ATT_GROUPS = ((128, 1), (512, 4), (2048, 16))
ATT_HEADS_PER_GROUP = 4
ATT_N_HEADS = len(ATT_GROUPS) * ATT_HEADS_PER_GROUP
ATT_WIDTH = ATT_N_HEADS * HEAD_DIM
ATT_OUT_WIDTH = ATT_HEADS_PER_GROUP * HEAD_DIM
ATT_BLOCK = 128
ROPE_THETA = 500000.0
ROPE_DIM = HEAD_DIM // 4
NEG_BIG = -1e30
LOG2_E = 1.4426950408889634

CB_DQ, CB_DK, CB_DV, CB_DZ = 0, 8, 16, 24
CB_GA, CB_GB = 32, 40
CB_AQ, CB_AK, CB_AV = 48, 60, 72
MAIN_WIDTH = 84 * LANES

VMEM_LIMIT = 56 * 1024 * 1024

PROJ_ROWS = 512
PROJ_COLS = 6 * MXU_WIDTH
DN_BLOCK = 1024
TAIL_ROWS = 512
FFN_COLS = 3 * MXU_WIDTH


def _cparams(sem):
    return pltpu.CompilerParams(dimension_semantics=sem, vmem_limit_bytes=VMEM_LIMIT)


def _dot(a, b):
    return jnp.dot(a, b, preferred_element_type=F32)


def _dot_nt(a, b):
    return lax.dot_general(a, b, (((1,), (1,)), ((), ())), preferred_element_type=F32)


def _sigmoid(x):
    return 0.5 * jnp.tanh(0.5 * x) + 0.5


def _silu(x):
    half = 0.5 * x
    return half * (jnp.tanh(half) + 1.0)


def _split_bf16(x):
    hi = x.astype(BF16)
    lo = (x - hi.astype(F32)).astype(BF16)
    return jnp.concatenate([hi, lo], axis=1)


SRC_SMALL = 4 * DN_WIDTH
SRC_ATT = SRC_SMALL + 2 * DN_HEADS
SRC_GATES = SRC_ATT + 3 * ATT_WIDTH


def _projection_tiles(tn):
    groups = ((0, 0, SRC_SMALL),
              (CB_GA * LANES, SRC_GATES, 2 * DN_WIDTH),
              (CB_AQ * LANES, SRC_ATT, 3 * ATT_WIDTH))
    return [(dst + off, src + off, min(tn, width - off))
            for dst, src, width in groups for off in range(0, width, tn)]


def _inproj_kernel(x_ref, nw_ref, wt_ref, o_ref, os_ref, *, tiles):
    x = x_ref[...]
    h = (x * lax.rsqrt(jnp.mean(x * x, axis=-1, keepdims=True) + EPS) * nw_ref[...]).astype(BF16)
    os_ref[...] = _dot_nt(h, wt_ref[SRC_SMALL:SRC_SMALL + LANES, :])
    for dst, src, width in tiles:
        o_ref[:, dst:dst + width] = _dot_nt(h, wt_ref[src:src + width, :]).astype(o_ref.dtype)


def _in_projection(x2, norm_w, w_t, *, tm, tn):
    t, d = x2.shape
    once = pl.Buffered(1)
    return pl.pallas_call(
        functools.partial(_inproj_kernel, tiles=_projection_tiles(tn)),
        grid=(t // tm,),
        in_specs=[
            pl.BlockSpec((tm, d), lambda i: (i, 0)),
            pl.BlockSpec((1, d), lambda i: (0, 0)),
            pl.BlockSpec(w_t.shape, lambda i: (0, 0), pipeline_mode=once),
        ],
        out_specs=[
            pl.BlockSpec((tm, MAIN_WIDTH), lambda i: (i, 0)),
            pl.BlockSpec((tm, LANES), lambda i: (i, 0)),
        ],
        out_shape=[
            jax.ShapeDtypeStruct((t, MAIN_WIDTH), BF16),
            jax.ShapeDtypeStruct((t, LANES), F32),
        ],
        compiler_params=_cparams(("parallel",)),
        name="in_projection",
    )(x2, norm_w, w_t)


def _deltanet_kernel(q_ref, k_ref, v_ref, z_ref, sm_ref, cw_ref, alog_ref, dtb_ref, nw_ref, o_ref,
                     s_scr, tail_scr, u_scr, wq_scr, qk_scr, kdt_scr, egl_scr,
                     pre_scr, gate_scr, gct_scr, sq_scr, *, lb):
    lstep = pl.program_id(1)
    nc = lb // CHUNK
    width = DN_WIDTH
    look = 16

    @pl.when(lstep == 0)
    def _():
        s_scr[...] = jnp.zeros_like(s_scr)
        tail_scr[...] = jnp.zeros_like(tail_scr)
        u_scr[:, 0:DUO * CHUNK, :] = jnp.zeros((DN_HEADS, DUO * CHUNK, HEAD_DIM), F32)
        wq_scr[0:DUO] = jnp.zeros((DUO,) + wq_scr.shape[1:], BF16)
        qk_scr[0:DUO] = jnp.zeros((DUO,) + qk_scr.shape[1:], BF16)
        kdt_scr[0:DUO] = jnp.zeros((DUO,) + kdt_scr.shape[1:], BF16)
        egl_scr[0:DUO] = jnp.zeros((DUO,) + egl_scr.shape[1:], F32)

    rowi = lax.broadcasted_iota(jnp.int32, (CHUNK, LANES), 0)
    lanei = lax.broadcasted_iota(jnp.int32, (CHUNK, LANES), 1)
    in_a = lanei < CHUNK
    ii = rowi
    jj = jnp.bitwise_and(lanei, CHUNK - 1)
    eye = (ii == jj).astype(F32)

    heads = range(DN_HEADS)
    pairs = [(h, h + 1) for h in range(0, DN_HEADS, 2)]
    head_cols = [slice(h * HEAD_DIM, (h + 1) * HEAD_DIM) for h in heads]
    srcs = (q_ref, k_ref, v_ref)
    ones = jnp.ones((2 * HEAD_DIM, HEAD_DIM), BF16)

    def anchor(value):
        rows8 = value[:8, :]
        zero = jnp.where(rows8 != rows8, rows8, 0.0)
        if zero.shape[1] < LANES:
            zero = jnp.concatenate([zero] * (LANES // zero.shape[1]), axis=1)
        return zero

    def conv_piece(which, h, c, r0, slot, zero):
        x_ref, cols = srcs[which], head_cols[h]
        prev0 = pl.multiple_of(jnp.maximum(r0 - look, 0), look)
        top = lax.select(jnp.asarray(c) == 0, tail_scr[which, :, cols],
                         x_ref[pl.ds(prev0, look), cols].astype(F32))
        win = jnp.concatenate([top, x_ref[pl.ds(r0, CHUNK), cols].astype(F32)], axis=0)
        w0 = which * width + h * HEAD_DIM
        acc = None
        for j in range(CONV_WIDTH):
            back = CONV_WIDTH - 1 - j
            rows = win if back == 0 else pltpu.roll(win, back, axis=0)
            term = rows[look:, :] * (cw_ref[j:j + 1, w0:w0 + HEAD_DIM] + zero[0:1, :])
            acc = term if acc is None else acc + term
        y = _silu(acc)
        pre_scr[slot, which, :, cols] = y
        if which < 2:
            sq_scr[slot, (which * DN_HEADS + h) * CHUNK:(which * DN_HEADS + h + 1) * CHUNK, :] = y * y

    def gate_piece(r0, slot, zero):
        sm = sm_ref[pl.ds(r0, CHUNK), :]
        xs = sm + (dtb_ref[...] + zero[0:1, :])
        softplus = jnp.maximum(xs, 0.0) + jnp.log(1.0 + jnp.exp(-jnp.abs(xs)))
        gc_all = -jnp.exp(alog_ref[...]) * softplus
        shift = 1
        while shift < CHUNK:
            gc_all = gc_all + jnp.where(rowi >= shift, pltpu.roll(gc_all, shift, axis=0), 0.0)
            shift *= 2
        gate_scr[slot, 0] = _sigmoid(sm)
        gate_scr[slot, 1] = gc_all
        gct_scr[slot] = gc_all.T

    def front_pieces(c, slot):
        r0 = c * CHUNK if isinstance(c, int) else pl.multiple_of(c * CHUNK, CHUNK)
        pieces = [functools.partial(conv_piece, which, h, c, r0, slot)
                  for which in range(3) for h in heads]
        return pieces + [functools.partial(gate_piece, r0, slot)]

    def front_finish(slot):
        inv_norm = lax.rsqrt(_dot(_split_bf16(sq_scr[slot]), ones) + EPS)
        for h in heads:
            cols = head_cols[h]
            pre_scr[slot, 0, :, cols] = pre_scr[slot, 0, :, cols] * (
                inv_norm[h * CHUNK:(h + 1) * CHUNK] * (HEAD_DIM ** -0.5))
            pre_scr[slot, 1, :, cols] = pre_scr[slot, 1, :, cols] * inv_norm[
                (DN_HEADS + h) * CHUNK:(DN_HEADS + h + 1) * CHUNK]

    for slot in range(DUO):
        for piece in front_pieces(slot, slot):
            piece(jnp.zeros((8, LANES), F32))
        front_finish(slot)

    def recur_read(cs):
        ss = [s_scr[h] for h in heads]
        rs = [_dot(wq_scr[cs, h], ss[h].astype(BF16)) for h in heads]
        return ss, rs

    def recur_update(cs, ss, rs, live):
        rc0 = pl.multiple_of(cs * CHUNK, CHUNK)
        vbs = []
        for h in heads:
            v_new = u_scr[h, pl.ds(rc0, CHUNK), :] - rs[h][:CHUNK]
            egl = egl_scr[cs, h][0:1, :]
            if live is not None:
                v_new = lax.select(live, v_new, jnp.zeros_like(v_new))
                egl = lax.select(live, egl, jnp.ones_like(egl))
            vbs.append(v_new.astype(BF16))
            s_scr[h] = ss[h] * egl + _dot(kdt_scr[cs, h], vbs[h])
        return [rs[h][CHUNK:] + _dot(qk_scr[cs, h], vbs[h]) for h in heads]

    def recur_finish(cs, outs):
        rc0 = pl.multiple_of(cs * CHUNK, CHUNK)
        for h in heads:
            o = outs[h]
            on = o * lax.rsqrt(jnp.mean(o * o, axis=-1, keepdims=True) + EPS) * nw_ref[...]
            z = z_ref[pl.ds(rc0, CHUNK), head_cols[h]].astype(F32)
            o_ref[pl.ds(rc0, CHUNK), head_cols[h]] = (on * _silu(z)).astype(o_ref.dtype)

    def prep(i, carry):
        first = i * DUO
        behind = jnp.maximum(first - DUO, 0)
        live = i > 0
        units = [(slot, a, b) for slot in range(DUO) for a, b in pairs]
        ns, held = [], {}
        qs, rhs = {}, {}
        zero_k = jnp.zeros((CHUNK, HEAD_DIM), BF16)
        for slot, a, b in units:
            c = first + slot
            beta_all = gate_scr[slot, 0]
            gc_all = gate_scr[slot, 1]
            gc_t = gct_scr[slot]
            side = []
            for h in (a, b):
                cols = head_cols[h]
                q = pre_scr[slot, 0, :, cols]
                k = pre_scr[slot, 1, :, cols]
                v = pre_scr[slot, 2, :, cols]
                bb = jnp.broadcast_to(beta_all[:, h:h + 1], (CHUNK, LANES))
                gcb = jnp.broadcast_to(gc_all[:, DN_HEADS + h:DN_HEADS + h + 1], (CHUNK, LANES))
                kb = k * bb
                held[slot, h] = (q, k, v, bb, gcb, kb)
                side.append((jnp.concatenate([kb, q], axis=0), k.astype(BF16), gcb,
                             gc_t[DN_HEADS + h:DN_HEADS + h + 1, :]))
            (lhs_a, k_a, gcb_a, row_a), (lhs_b, k_b, gcb_b, row_b) = side
            both = _dot_nt(jnp.concatenate([lhs_a, lhs_b], axis=1).astype(BF16),
                           jnp.concatenate([jnp.concatenate([k_a, zero_k], axis=1),
                                            jnp.concatenate([zero_k, k_b], axis=1)], axis=0))
            g_col = jnp.where(in_a, gcb_a, gcb_b)
            g_row = jnp.concatenate([row_a, row_b], axis=1)
            decay = jnp.where(ii >= jj, jnp.exp(jnp.minimum(g_col - g_row, 0.0)), 0.0)
            ns.append(jnp.where(ii > jj, -both[:CHUNK] * decay, 0.0))
            qk = (both[CHUNK:] * decay).astype(BF16)
            qk_scr[c, a] = qk[:, :CHUNK]
            qk_scr[c, b] = qk[:, CHUNK:]
            kb_last = kb

        def block_diag(m):
            return jnp.concatenate([jnp.where(in_a, m, 0.0), jnp.where(in_a, 0.0, m)],
                                   axis=0).astype(BF16)

        def late_piece(slot, h, zero):
            c = first + slot
            q, k, v, bb, gcb, kb = held[slot, h]
            gcz = gcb + zero[0:1, :]
            g_last = gcz[CHUNK - 1:CHUNK, :]
            eg = jnp.exp(gcz)
            kdt_scr[c, h] = (k * jnp.exp(g_last - gcz)).T.astype(BF16)
            egl_scr[c, h] = jnp.broadcast_to(jnp.exp(g_last), (8, LANES))
            rhs[slot, h] = jnp.concatenate([v * bb, kb * eg], axis=1).astype(BF16)
            qs[slot, h] = q * eg

        ahead = jnp.minimum(first + DUO, nc - DUO)
        late = [functools.partial(late_piece, slot, h) for slot in range(DUO) for h in heads]
        front = []
        for slot in range(DUO):
            front += front_pieces(ahead + slot, slot)
        gaps = [(late + front)[g::7] for g in range(7)]

        def fill(gap, operand):
            zero = anchor(operand)
            for piece in gaps[gap]:
                piece(zero)

        n_units = range(len(units))
        state_in = recur_read(behind)
        fill(0, kb_last)
        ps = [_dot(ns[n].astype(BF16), block_diag(ns[n])) for n in n_units]
        fill(1, ns[-1])
        xs_ = [eye + ns[n] for n in n_units]
        for level in range(4):
            operand = xs_[-1]
            for n in n_units:
                r = _dot(jnp.concatenate([xs_[n], ps[n]], axis=0).astype(BF16), block_diag(ps[n]))
                xs_[n] = xs_[n] + r[:CHUNK]
                ps[n] = r[CHUNK:]
            if level == 0:
                outs0 = recur_update(behind, *state_in, live)
            if level == 1:
                state_in = recur_read(behind + 1)
            if level == 2:
                outs1 = recur_update(behind + 1, *state_in, live)
                recur_finish(behind, outs0)
            if level == 3:
                recur_finish(behind + 1, outs1)
            fill(2 + level, operand)
        operand = xs_[-1]
        for n in n_units:
            xs_[n] = xs_[n] + _dot(xs_[n].astype(BF16), block_diag(ps[n]))
        fill(6, operand)
        zero_rhs = jnp.zeros((CHUNK, 2 * HEAD_DIM), BF16)
        for n, (slot, a, b) in enumerate(units):
            c = first + slot
            r0 = pl.multiple_of(c * CHUNK, CHUNK)
            uw = _dot(xs_[n].astype(BF16),
                      jnp.concatenate([jnp.concatenate([rhs[slot, a], zero_rhs], axis=1),
                                       jnp.concatenate([zero_rhs, rhs[slot, b]], axis=1)], axis=0))
            for h, base in ((a, 0), (b, 2 * HEAD_DIM)):
                u_scr[h, pl.ds(r0, CHUNK), :] = uw[:, base:base + HEAD_DIM]
                wq_scr[c, h] = jnp.concatenate(
                    [uw[:, base + HEAD_DIM:base + 2 * HEAD_DIM], qs[slot, h]], axis=0).astype(BF16)
        for slot in range(DUO):
            front_finish(slot)
        return carry

    lax.fori_loop(0, nc // DUO, prep, 0)
    for cs in range(nc - DUO, nc):
        recur_finish(cs, recur_update(cs, *recur_read(cs), None))

    for which, x_ref in enumerate((q_ref, k_ref, v_ref)):
        tail_scr[which] = x_ref[lb - look:lb, :].astype(F32)


def _deltanet(main, small, conv_w, alog_row, dtb_row, dn_norm_w, *, batch, seq, lb):
    t = batch * seq
    nlb = seq // lb
    nc = lb // CHUNK
    wblk = DN_WIDTH // LANES

    def col(cb):
        return pl.BlockSpec((lb, DN_WIDTH), lambda b, l, cb=cb: (b * nlb + l, cb // wblk))

    row = pl.BlockSpec((1, LANES), lambda b, l: (0, 0))
    return pl.pallas_call(
        functools.partial(_deltanet_kernel, lb=lb),
        grid=(batch, nlb),
        in_specs=[col(CB_DQ), col(CB_DK), col(CB_DV), col(CB_DZ),
                  pl.BlockSpec((lb, LANES), lambda b, l: (b * nlb + l, 0)),
                  pl.BlockSpec(conv_w.shape, lambda b, l: (0, 0)), row, row, row],
        out_specs=pl.BlockSpec((lb, DN_WIDTH), lambda b, l: (b * nlb + l, 0)),
        out_shape=jax.ShapeDtypeStruct((t, DN_WIDTH), BF16),
        scratch_shapes=[
            pltpu.VMEM((DN_HEADS, HEAD_DIM, HEAD_DIM), F32),
            pltpu.VMEM((3, 16, DN_WIDTH), F32),
            pltpu.VMEM((DN_HEADS, lb, HEAD_DIM), F32),
            pltpu.VMEM((nc, DN_HEADS, 2 * CHUNK, HEAD_DIM), BF16),
            pltpu.VMEM((nc, DN_HEADS, CHUNK, CHUNK), BF16),
            pltpu.VMEM((nc, DN_HEADS, HEAD_DIM, CHUNK), BF16),
            pltpu.VMEM((nc, DN_HEADS, 8, LANES), F32),
            pltpu.VMEM((DUO, 3, CHUNK, DN_WIDTH), F32),
            pltpu.VMEM((DUO, 2, CHUNK, LANES), F32),
            pltpu.VMEM((DUO, LANES, CHUNK), F32),
            pltpu.VMEM((DUO, 2 * DN_HEADS * CHUNK, HEAD_DIM), F32),
        ],
        compiler_params=_cparams(("parallel", "arbitrary")),
        name="deltanet",
    )(main, main, main, main, small, conv_w, alog_row, dtb_row, dn_norm_w)


def _attention_kernel(q_ref, k_ref, v_ref, tc_ref, ts_ref, o_ref,
                      qr_scr, kr_scr, v_scr, og_scr, lse_scr, *, seq):
    group = pl.program_id(2)
    rows = 256
    half = ROPE_DIM // 2
    lane = lax.broadcasted_iota(jnp.int32, (rows, LANES), 1)
    src_lane = lax.broadcasted_iota(jnp.int32, (LANES, LANES), 0)
    dst_lane = lax.broadcasted_iota(jnp.int32, (LANES, LANES), 1)
    want = jnp.where(dst_lane < half, dst_lane + half, jnp.where(dst_lane < ROPE_DIM, dst_lane - half, -1))
    swap = jnp.where(src_lane == want, 1.0, 0.0).astype(BF16)

    def rope(i, carry):
        r0 = pl.multiple_of(i * rows, rows)
        tc = tc_ref[pl.ds(r0, rows), :]
        ts = ts_ref[pl.ds(r0, rows), :]
        xq = q_ref[pl.ds(r0, rows), :].astype(F32)
        partner_q = jnp.where(lane < half, pltpu.roll(xq, LANES - half, axis=1),
                              pltpu.roll(xq, half, axis=1))
        qr_scr[pl.ds(r0, rows), :] = (xq * tc + partner_q * ts) * (HEAD_DIM ** -0.5 * LOG2_E)
        xk = k_ref[pl.ds(r0, rows), :]
        kr_scr[pl.ds(r0, rows), :] = xk.astype(F32) * tc + _dot(xk, swap) * ts
        v_scr[pl.ds(r0, rows), :] = v_ref[pl.ds(r0, rows), :].astype(F32)
        return carry

    lax.fori_loop(0, seq // rows, rope, 0, unroll=4)

    ii = lax.broadcasted_iota(jnp.int32, (ATT_BLOCK, 2 * ATT_BLOCK), 0)
    jj = lax.broadcasted_iota(jnp.int32, (ATT_BLOCK, 2 * ATT_BLOCK), 1)
    dist = ii + ATT_BLOCK - jj
    band_mask = jnp.where(dist >= 0, jnp.where(dist <= ATT_BLOCK, 0.0, NEG_BIG), NEG_BIG)
    prev_half = jnp.where(jj < ATT_BLOCK, 1.0, 0.0)
    per_iter = 8

    order = _attention_order()
    stored = order[:-1]

    def run_group(gi, dil, last):
        nb = seq // (dil * ATT_BLOCK)
        span = dil * ATT_BLOCK

        def rows_of(start):
            if dil == 1:
                return pl.ds(start, ATT_BLOCK)
            return pl.ds(start, ATT_BLOCK, stride=dil)

        def blocks(it, carry):
            where = []
            for u in range(per_iter):
                t = it * per_iter + u
                r = lax.shift_right_logical(t, nb.bit_length() - 1)
                n = jnp.bitwise_and(t, nb - 1)
                start = r + n * span
                if dil == 1:
                    start = pl.multiple_of(n * span, ATT_BLOCK)
                where.append((n, start, jnp.maximum(start - span, r)))
            run = min(nb, per_iter)
            kcur = [kr_scr[rows_of(start), :].astype(BF16) for _, start, _ in where]
            vcur = [v_scr[rows_of(start), :].astype(BF16) for _, start, _ in where]

            def prev_of(u, cur, ref):
                if u % run:
                    return cur[u - 1]
                if nb <= per_iter:
                    return cur[u]
                return ref[rows_of(where[u][2]), :].astype(BF16)

            scores = []
            for u, (n, start, prev) in enumerate(where):
                qb = qr_scr[rows_of(start), :].astype(BF16)
                kcat = jnp.concatenate([prev_of(u, kcur, kr_scr), kcur[u]], axis=0)
                s = _dot_nt(qb, kcat)
                no_prev = jnp.where(n > 0, 0.0, NEG_BIG)
                s = s + (band_mask + prev_half * no_prev)
                scores.append(s)
            probs = []
            for s in scores:
                m = jnp.max(s, axis=1, keepdims=True)
                p = jnp.exp2(s - m)
                probs.append((m, p, jnp.sum(p, axis=1, keepdims=True)))
            for u, ((n, start, prev), (m, p, den)) in enumerate(zip(where, probs)):
                vcat = jnp.concatenate([prev_of(u, vcur, v_scr), vcur[u]], axis=0)
                acc = _dot(p.astype(BF16), vcat)
                out = acc / den
                lse = jnp.broadcast_to(m + jnp.log2(den), (ATT_BLOCK, LANES))
                if not last:
                    og_scr[gi, rows_of(start), :] = out
                    lse_scr[gi, rows_of(start), :] = lse
                    continue
                outs = [og_scr[g, rows_of(start), :] for g in stored] + [out]
                lses = [lse_scr[g, rows_of(start), :] for g in stored] + [lse]
                mx = functools.reduce(jnp.maximum, lses)
                es = [jnp.exp2(l - mx) for l in lses]
                num = functools.reduce(lambda a, b: a + b, [e * o for e, o in zip(es, outs)])
                o_ref[rows_of(start), :] = (
                    num / functools.reduce(lambda a, b: a + b, es)).astype(o_ref.dtype)
            return carry

        lax.fori_loop(0, seq // (ATT_BLOCK * per_iter), blocks, 0)

    for step, gi in enumerate(order):
        @pl.when(group == step)
        def _(gi=gi, step=step):
            run_group(gi, ATT_GROUPS[gi][1], last=(step == len(order) - 1))


def _attention_order():
    order = sorted(range(len(ATT_GROUPS)), key=lambda g: -ATT_GROUPS[g][1])
    assert ATT_GROUPS[order[-1]][1] == 1
    return order


def _attention(main, rope_cos, rope_sin, *, batch, seq):
    t = batch * seq
    ng = len(ATT_GROUPS)
    order = _attention_order()
    assert order == list(range(ng - 1, -1, -1))

    def col(cb):
        return pl.BlockSpec(
            (seq, HEAD_DIM),
            lambda b, h, g, cb=cb: (b, cb + (ng - 1 - g) * ATT_HEADS_PER_GROUP + h))

    tab = pl.BlockSpec((seq, LANES), lambda b, h, g: (0, 0))
    return pl.pallas_call(
        functools.partial(_attention_kernel, seq=seq),
        grid=(batch, ATT_HEADS_PER_GROUP, ng),
        in_specs=[col(CB_AQ), col(CB_AK), col(CB_AV), tab, tab],
        out_specs=pl.BlockSpec((seq, HEAD_DIM), lambda b, h, g: (b, h)),
        out_shape=jax.ShapeDtypeStruct((t, ATT_OUT_WIDTH), BF16),
        scratch_shapes=[
            pltpu.VMEM((seq, HEAD_DIM), F32),
            pltpu.VMEM((seq, HEAD_DIM), F32),
            pltpu.VMEM((seq, HEAD_DIM), F32),
            pltpu.VMEM((ng, seq, HEAD_DIM), F32),
            pltpu.VMEM((ng, seq, LANES), F32),
        ],
        compiler_params=_cparams(("parallel", "parallel", "arbitrary")),
        name="dilated_attention",
    )(main, main, main, rope_cos, rope_sin)


def _rope_tables(seq):
    half = ROPE_DIM // 2
    inv_freq = np.power(ROPE_THETA, -np.arange(half, dtype=np.float64) * (2.0 / ROPE_DIM))
    ang = np.arange(seq, dtype=np.float64)[:, None] * inv_freq[None, :]
    cos, sin = np.cos(ang), np.sin(ang)
    rest = HEAD_DIM - ROPE_DIM
    tc = np.concatenate([cos, cos, np.ones((seq, rest))], axis=1)
    ts = np.concatenate([-sin, sin, np.zeros((seq, rest))], axis=1)
    return jnp.asarray(tc, F32), jnp.asarray(ts, F32)


def _merge_ffn_kernel(x_ref, oa_ref, ob_ref, ga_ref, gb_ref, wa_ref, wb_ref, wo_ref,
                      nw_ref, wg_ref, wu_ref, wd_ref, fw_ref, o_ref, *, final_norm, tf):
    ya = _dot(oa_ref[...], wa_ref[...])
    yb = _dot(ob_ref[...], wb_ref[...])
    merged = (_sigmoid(ga_ref[...].astype(F32)) * ya + _sigmoid(gb_ref[...].astype(F32)) * yb)
    x = x_ref[...] + _dot(merged.astype(BF16), wo_ref[...])
    h = (x * lax.rsqrt(jnp.mean(x * x, axis=-1, keepdims=True) + EPS) * nw_ref[...]).astype(BF16)
    d_ff = wd_ref.shape[0]
    tiles = [slice(f0, min(f0 + tf, d_ff)) for f0 in range(0, d_ff, tf)]
    y = x
    pending = None
    for cols in tiles + [None]:
        issued = None if cols is None else (_dot(h, wg_ref[:, cols]), _dot(h, wu_ref[:, cols]))
        if pending is not None:
            pcols, (gate, up) = pending
            y = y + _dot((_silu(gate) * up).astype(BF16), wd_ref[pcols, :])
        pending = (cols, issued)
    if final_norm:
        y = y * lax.rsqrt(jnp.mean(y * y, axis=-1, keepdims=True) + EPS) * fw_ref[...]
    o_ref[...] = y


def _merge_ffn(x2, oa, ob, main, w_a, w_b, w_o, norm_w, w_gate_up, w_down, final_w, *,
               tm, tf, final_norm):
    t, d = x2.shape
    d_ff = w_down.shape[0]
    gate_cb = CB_GA * LANES // d
    once = pl.Buffered(1)

    def weight(shape, col=0):
        return pl.BlockSpec(shape, lambda i, col=col: (0, col), pipeline_mode=once)

    row = pl.BlockSpec((1, d), lambda i: (0, 0))
    return pl.pallas_call(
        functools.partial(_merge_ffn_kernel, final_norm=final_norm, tf=tf),
        grid=(t // tm,),
        in_specs=[
            pl.BlockSpec((tm, d), lambda i: (i, 0)),
            pl.BlockSpec((tm, DN_WIDTH), lambda i: (i, 0)),
            pl.BlockSpec((tm, ATT_OUT_WIDTH), lambda i: (i, 0)),
            pl.BlockSpec((tm, d), lambda i: (i, gate_cb)),
            pl.BlockSpec((tm, d), lambda i: (i, gate_cb + 1)),
            weight(w_a.shape), weight(w_b.shape), weight(w_o.shape),
            row, weight((d, d_ff)), weight((d, d_ff), 1), weight((d_ff, d)), row,
        ],
        out_specs=pl.BlockSpec((tm, d), lambda i: (i, 0)),
        out_shape=jax.ShapeDtypeStruct((t, d), F32),
        compiler_params=_cparams(("parallel",)),
        name="merge_ffn",
    )(x2, oa, ob, main, main, w_a, w_b, w_o, norm_w, w_gate_up, w_gate_up, w_down, final_w)


def _pad_lanes(v, offset):
    out = jnp.zeros((1, LANES), F32)
    return lax.dynamic_update_slice(out, v.reshape(1, -1).astype(F32), (0, offset))


def kernel(x, norm1_w, w_in, conv_w, a_log, dt_bias, dn_norm_w, w_proj_a, w_proj_b, w_out,
           norm2_w, w_gate_up, w_down, final_norm_w):
    batch, seq, d_model = x.shape
    depth = w_in.shape[0]
    t = batch * seq
    assert d_model == 8 * LANES and seq % (ATT_GROUPS[-1][1] * ATT_BLOCK) == 0
    assert w_in.shape[2] == SRC_GATES + 2 * d_model

    rope_cos, rope_sin = _rope_tables(seq)
    x2 = x.reshape(t, d_model)
    for i in range(depth):
        w_in_t = jnp.swapaxes(w_in[i], 0, 1).astype(BF16)
        main, small = _in_projection(x2, norm1_w[i].reshape(1, d_model), w_in_t,
                                     tm=PROJ_ROWS, tn=PROJ_COLS)
        oa = _deltanet(main, small, conv_w[i], _pad_lanes(a_log[i], DN_HEADS),
                       _pad_lanes(dt_bias[i], DN_HEADS), dn_norm_w[i].reshape(1, HEAD_DIM),
                       batch=batch, seq=seq, lb=DN_BLOCK)
        ob = _attention(main, rope_cos, rope_sin, batch=batch, seq=seq)
        x2 = _merge_ffn(x2, oa, ob, main, w_proj_a[i].astype(BF16), w_proj_b[i].astype(BF16),
                        w_out[i].astype(BF16), norm2_w[i].reshape(1, d_model),
                        w_gate_up[i].astype(BF16), w_down[i].astype(BF16),
                        final_norm_w.reshape(1, d_model), tm=TAIL_ROWS, tf=FFN_COLS,
                        final_norm=(i == depth - 1))
    return x2.reshape(batch, seq, d_model)
```

```python
import functools

import jax
import numpy as np
import jax.numpy as jnp
from jax import lax
from jax.experimental import pallas as pl
from jax.experimental.pallas import tpu as pltpu

F32 = jnp.float32
BF16 = jnp.bfloat16

EPS = 1e-6
LANES = 128
MXU_WIDTH = 256
HEAD_DIM = 128
DN_HEADS = 8
DN_WIDTH = DN_HEADS * HEAD_DIM
CONV_WIDTH = 4
CHUNK = 64
DUO = 2
ATT_GROUPS = ((128, 1), (512, 4), (2048, 16))
ATT_HEADS_PER_GROUP = 4
ATT_N_HEADS = len(ATT_GROUPS) * ATT_HEADS_PER_GROUP
ATT_WIDTH = ATT_N_HEADS * HEAD_DIM
ATT_OUT_WIDTH = ATT_HEADS_PER_GROUP * HEAD_DIM
ATT_BLOCK = 128
ROPE_THETA = 500000.0
ROPE_DIM = HEAD_DIM // 4
NEG_BIG = -1e30
LOG2_E = 1.4426950408889634

CB_DQ, CB_DK, CB_DV, CB_DZ = 0, 8, 16, 24
CB_GA, CB_GB = 32, 40
CB_AQ, CB_AK, CB_AV = 48, 60, 72
MAIN_WIDTH = 84 * LANES

VMEM_LIMIT = 56 * 1024 * 1024

PROJ_ROWS = 512
PROJ_COLS = 6 * MXU_WIDTH
DN_BLOCK = 1024
TAIL_ROWS = 512
FFN_COLS = 3 * MXU_WIDTH


def _cparams(sem):
    return pltpu.CompilerParams(dimension_semantics=sem, vmem_limit_bytes=VMEM_LIMIT)


def _dot(a, b):
    return jnp.dot(a, b, preferred_element_type=F32)


def _dot_nt(a, b):
    return lax.dot_general(a, b, (((1,), (1,)), ((), ())), preferred_element_type=F32)


def _sigmoid(x):
    return 0.5 * jnp.tanh(0.5 * x) + 0.5


def _silu(x):
    half = 0.5 * x
    return half * (jnp.tanh(half) + 1.0)


def _split_bf16(x):
    hi = x.astype(BF16)
    lo = (x - hi.astype(F32)).astype(BF16)
    return jnp.concatenate([hi, lo], axis=1)


SRC_SMALL = 4 * DN_WIDTH
SRC_ATT = SRC_SMALL + 2 * DN_HEADS
SRC_GATES = SRC_ATT + 3 * ATT_WIDTH


def _projection_tiles(tn):
    groups = ((0, 0, SRC_SMALL),
              (CB_GA * LANES, SRC_GATES, 2 * DN_WIDTH),
              (CB_AQ * LANES, SRC_ATT, 3 * ATT_WIDTH))
    return [(dst + off, src + off, min(tn, width - off))
            for dst, src, width in groups for off in range(0, width, tn)]


def _inproj_kernel(x_ref, nw_ref, wt_ref, o_ref, os_ref, *, tiles):
    x = x_ref[...]
    h = (x * lax.rsqrt(jnp.mean(x * x, axis=-1, keepdims=True) + EPS) * nw_ref[...]).astype(BF16)
    os_ref[...] = _dot_nt(h, wt_ref[SRC_SMALL:SRC_SMALL + LANES, :])
    for dst, src, width in tiles:
        o_ref[:, dst:dst + width] = _dot_nt(h, wt_ref[src:src + width, :]).astype(o_ref.dtype)


def _in_projection(x2, norm_w, w_t, *, tm, tn):
    t, d = x2.shape
    once = pl.Buffered(1)
    return pl.pallas_call(
        functools.partial(_inproj_kernel, tiles=_projection_tiles(tn)),
        grid=(t // tm,),
        in_specs=[
            pl.BlockSpec((tm, d), lambda i: (i, 0)),
            pl.BlockSpec((1, d), lambda i: (0, 0)),
            pl.BlockSpec(w_t.shape, lambda i: (0, 0), pipeline_mode=once),
        ],
        out_specs=[
            pl.BlockSpec((tm, MAIN_WIDTH), lambda i: (i, 0)),
            pl.BlockSpec((tm, LANES), lambda i: (i, 0)),
        ],
        out_shape=[
            jax.ShapeDtypeStruct((t, MAIN_WIDTH), BF16),
            jax.ShapeDtypeStruct((t, LANES), F32),
        ],
        compiler_params=_cparams(("parallel",)),
        name="in_projection",
    )(x2, norm_w, w_t)


def _deltanet_kernel(q_ref, k_ref, v_ref, z_ref, sm_ref, cw_ref, alog_ref, dtb_ref, nw_ref, o_ref,
                     s_scr, tail_scr, u_scr, wq_scr, qk_scr, kdt_scr, egl_scr,
                     pre_scr, gate_scr, gct_scr, sq_scr, *, lb):
    lstep = pl.program_id(1)
    nc = lb // CHUNK
    width = DN_WIDTH
    look = 16

    @pl.when(lstep == 0)
    def _():
        s_scr[...] = jnp.zeros_like(s_scr)
        tail_scr[...] = jnp.zeros_like(tail_scr)
        u_scr[:, 0:DUO * CHUNK, :] = jnp.zeros((DN_HEADS, DUO * CHUNK, HEAD_DIM), F32)
        wq_scr[0:DUO] = jnp.zeros((DUO,) + wq_scr.shape[1:], BF16)
        qk_scr[0:DUO] = jnp.zeros((DUO,) + qk_scr.shape[1:], BF16)
        kdt_scr[0:DUO] = jnp.zeros((DUO,) + kdt_scr.shape[1:], BF16)
        egl_scr[0:DUO] = jnp.zeros((DUO,) + egl_scr.shape[1:], F32)

    rowi = lax.broadcasted_iota(jnp.int32, (CHUNK, LANES), 0)
    lanei = lax.broadcasted_iota(jnp.int32, (CHUNK, LANES), 1)
    in_a = lanei < CHUNK
    ii = rowi
    jj = jnp.bitwise_and(lanei, CHUNK - 1)
    eye = (ii == jj).astype(F32)

    heads = range(DN_HEADS)
    pairs = [(h, h + 1) for h in range(0, DN_HEADS, 2)]
    head_cols = [slice(h * HEAD_DIM, (h + 1) * HEAD_DIM) for h in heads]
    srcs = (q_ref, k_ref, v_ref)
    ones = jnp.ones((2 * HEAD_DIM, HEAD_DIM), BF16)

    def anchor(value):
        rows8 = value[:8, :]
        zero = jnp.where(rows8 != rows8, rows8, 0.0)
        if zero.shape[1] < LANES:
            zero = jnp.concatenate([zero] * (LANES // zero.shape[1]), axis=1)
        return zero

    def conv_piece(which, h, c, r0, slot, zero):
        x_ref, cols = srcs[which], head_cols[h]
        prev0 = pl.multiple_of(jnp.maximum(r0 - look, 0), look)
        top = lax.select(jnp.asarray(c) == 0, tail_scr[which, :, cols],
                         x_ref[pl.ds(prev0, look), cols].astype(F32))
        win = jnp.concatenate([top, x_ref[pl.ds(r0, CHUNK), cols].astype(F32)], axis=0)
        w0 = which * width + h * HEAD_DIM
        acc = None
        for j in range(CONV_WIDTH):
            back = CONV_WIDTH - 1 - j
            rows = win if back == 0 else pltpu.roll(win, back, axis=0)
            term = rows[look:, :] * (cw_ref[j:j + 1, w0:w0 + HEAD_DIM] + zero[0:1, :])
            acc = term if acc is None else acc + term
        y = _silu(acc)
        pre_scr[slot, which, :, cols] = y
        if which < 2:
            sq_scr[slot, (which * DN_HEADS + h) * CHUNK:(which * DN_HEADS + h + 1) * CHUNK, :] = y * y

    def gate_piece(r0, slot, zero):
        sm = sm_ref[pl.ds(r0, CHUNK), :]
        xs = sm + (dtb_ref[...] + zero[0:1, :])
        softplus = jnp.maximum(xs, 0.0) + jnp.log(1.0 + jnp.exp(-jnp.abs(xs)))
        gc_all = (-jnp.exp(alog_ref[...]) * LOG2_E) * softplus
        shift = 1
        while shift < CHUNK:
            gc_all = gc_all + jnp.where(rowi >= shift, pltpu.roll(gc_all, shift, axis=0), 0.0)
            shift *= 2
        gate_scr[slot, 0] = _sigmoid(sm)
        gate_scr[slot, 1] = gc_all
        gct_scr[slot] = gc_all.T

    def front_pieces(c, slot):
        r0 = c * CHUNK if isinstance(c, int) else pl.multiple_of(c * CHUNK, CHUNK)
        pieces = [functools.partial(conv_piece, which, h, c, r0, slot)
                  for which in range(3) for h in heads]
        return pieces + [functools.partial(gate_piece, r0, slot)]

    def front_finish(slot):
        inv_norm = lax.rsqrt(_dot(_split_bf16(sq_scr[slot]), ones) + EPS)
        for h in heads:
            cols = head_cols[h]
            pre_scr[slot, 0, :, cols] = pre_scr[slot, 0, :, cols] * (
                inv_norm[h * CHUNK:(h + 1) * CHUNK] * (HEAD_DIM ** -0.5))
            pre_scr[slot, 1, :, cols] = pre_scr[slot, 1, :, cols] * inv_norm[
                (DN_HEADS + h) * CHUNK:(DN_HEADS + h + 1) * CHUNK]

    for slot in range(DUO):
        for piece in front_pieces(slot, slot):
            piece(jnp.zeros((8, LANES), F32))
        front_finish(slot)

    def recur_read(cs):
        ss = [s_scr[h] for h in heads]
        rs = [_dot(wq_scr[cs, h], ss[h].astype(BF16)) for h in heads]
        return ss, rs

    def recur_update(cs, ss, rs, live):
        rc0 = pl.multiple_of(cs * CHUNK, CHUNK)
        vbs = []
        for h in heads:
            v_new = u_scr[h, pl.ds(rc0, CHUNK), :] - rs[h][:CHUNK]
            egl = egl_scr[cs, h][0:1, :]
            if live is not None:
                v_new = lax.select(live, v_new, jnp.zeros_like(v_new))
                egl = lax.select(live, egl, jnp.ones_like(egl))
            vbs.append(v_new.astype(BF16))
            s_scr[h] = ss[h] * egl + _dot(kdt_scr[cs, h], vbs[h])
        return [rs[h][CHUNK:] + _dot(qk_scr[cs, h], vbs[h]) for h in heads]

    def recur_finish(cs, outs):
        rc0 = pl.multiple_of(cs * CHUNK, CHUNK)
        for h in heads:
            o = outs[h]
            on = o * lax.rsqrt(jnp.mean(o * o, axis=-1, keepdims=True) + EPS) * nw_ref[...]
            z = z_ref[pl.ds(rc0, CHUNK), head_cols[h]].astype(F32)
            o_ref[pl.ds(rc0, CHUNK), head_cols[h]] = (on * _silu(z)).astype(o_ref.dtype)

    def prep(i, carry):
        first = i * DUO
        behind = jnp.maximum(first - DUO, 0)
        live = i > 0
        units = [(slot, a, b) for slot in range(DUO) for a, b in pairs]
        ns, held = [], {}
        qs, rhs = {}, {}
        zero_k = jnp.zeros((CHUNK, HEAD_DIM), BF16)
        for slot, a, b in units:
            c = first + slot
            beta_all = gate_scr[slot, 0]
            gc_all = gate_scr[slot, 1]
            gc_t = gct_scr[slot]
            side = []
            for h in (a, b):
                cols = head_cols[h]
                q = pre_scr[slot, 0, :, cols]
                k = pre_scr[slot, 1, :, cols]
                v = pre_scr[slot, 2, :, cols]
                bb = jnp.broadcast_to(beta_all[:, h:h + 1], (CHUNK, LANES))
                gcb = jnp.broadcast_to(gc_all[:, DN_HEADS + h:DN_HEADS + h + 1], (CHUNK, LANES))
                kb = k * bb
                held[slot, h] = (q, k, v, bb, gcb, kb)
                side.append((jnp.concatenate([kb, q], axis=0), k.astype(BF16), gcb,
                             gc_t[DN_HEADS + h:DN_HEADS + h + 1, :]))
            (lhs_a, k_a, gcb_a, row_a), (lhs_b, k_b, gcb_b, row_b) = side
            both = _dot_nt(jnp.concatenate([lhs_a, lhs_b], axis=1).astype(BF16),
                           jnp.concatenate([jnp.concatenate([k_a, zero_k], axis=1),
                                            jnp.concatenate([zero_k, k_b], axis=1)], axis=0))
            g_col = jnp.where(in_a, gcb_a, gcb_b)
            g_row = jnp.concatenate([row_a, row_b], axis=1)
            decay = jnp.where(ii >= jj, jnp.exp2(jnp.minimum(g_col - g_row, 0.0)), 0.0)
            ns.append(jnp.where(ii > jj, -both[:CHUNK] * decay, 0.0))
            qk = (both[CHUNK:] * decay).astype(BF16)
            qk_scr[c, a] = qk[:, :CHUNK]
            qk_scr[c, b] = qk[:, CHUNK:]
            kb_last = kb

        def block_diag(m):
            return jnp.concatenate([jnp.where(in_a, m, 0.0), jnp.where(in_a, 0.0, m)],
                                   axis=0).astype(BF16)

        def late_piece(slot, h, zero):
            c = first + slot
            q, k, v, bb, gcb, kb = held[slot, h]
            gcz = gcb + zero[0:1, :]
            g_last = gcz[CHUNK - 1:CHUNK, :]
            eg = jnp.exp2(gcz)
            kdt_scr[c, h] = (k * jnp.exp2(g_last - gcz)).T.astype(BF16)
            egl_scr[c, h] = jnp.broadcast_to(jnp.exp2(g_last), (8, LANES))
            rhs[slot, h] = jnp.concatenate([v * bb, kb * eg], axis=1).astype(BF16)
            qs[slot, h] = q * eg

        ahead = jnp.minimum(first + DUO, nc - DUO)
        late = [functools.partial(late_piece, slot, h) for slot in range(DUO) for h in heads]
        front = []
        for slot in range(DUO):
            front += front_pieces(ahead + slot, slot)
        gaps = [(late + front)[g::7] for g in range(7)]

        def fill(gap, operand):
            zero = anchor(operand)
            for piece in gaps[gap]:
                piece(zero)

        n_units = range(len(units))
        state_in = recur_read(behind)
        fill(0, kb_last)
        ps = [_dot(ns[n].astype(BF16), block_diag(ns[n])) for n in n_units]
        fill(1, ns[-1])
        xs_ = [eye + ns[n] for n in n_units]
        for level in range(4):
            operand = xs_[-1]
            for n in n_units:
                r = _dot(jnp.concatenate([xs_[n], ps[n]], axis=0).astype(BF16), block_diag(ps[n]))
                xs_[n] = xs_[n] + r[:CHUNK]
                ps[n] = r[CHUNK:]
            if level == 0:
                outs0 = recur_update(behind, *state_in, live)
            if level == 1:
                state_in = recur_read(behind + 1)
            if level == 2:
                outs1 = recur_update(behind + 1, *state_in, live)
                recur_finish(behind, outs0)
            if level == 3:
                recur_finish(behind + 1, outs1)
            fill(2 + level, operand)
        operand = xs_[-1]
        for n in n_units:
            xs_[n] = xs_[n] + _dot(xs_[n].astype(BF16), block_diag(ps[n]))
        fill(6, operand)
        zero_rhs = jnp.zeros((CHUNK, 2 * HEAD_DIM), BF16)
        for n, (slot, a, b) in enumerate(units):
            c = first + slot
            r0 = pl.multiple_of(c * CHUNK, CHUNK)
            uw = _dot(xs_[n].astype(BF16),
                      jnp.concatenate([jnp.concatenate([rhs[slot, a], zero_rhs], axis=1),
                                       jnp.concatenate([zero_rhs, rhs[slot, b]], axis=1)], axis=0))
            for h, base in ((a, 0), (b, 2 * HEAD_DIM)):
                u_scr[h, pl.ds(r0, CHUNK), :] = uw[:, base:base + HEAD_DIM]
                wq_scr[c, h] = jnp.concatenate(
                    [uw[:, base + HEAD_DIM:base + 2 * HEAD_DIM], qs[slot, h]], axis=0).astype(BF16)
        for slot in range(DUO):
            front_finish(slot)
        return carry

    lax.fori_loop(0, nc // DUO, prep, 0)
    for cs in range(nc - DUO, nc):
        recur_finish(cs, recur_update(cs, *recur_read(cs), None))

    for which, x_ref in enumerate((q_ref, k_ref, v_ref)):
        tail_scr[which] = x_ref[lb - look:lb, :].astype(F32)


def _deltanet(main, small, conv_w, alog_row, dtb_row, dn_norm_w, *, batch, seq, lb):
    t = batch * seq
    nlb = seq // lb
    nc = lb // CHUNK
    wblk = DN_WIDTH // LANES

    def col(cb):
        return pl.BlockSpec((lb, DN_WIDTH), lambda b, l, cb=cb: (b * nlb + l, cb // wblk))

    row = pl.BlockSpec((1, LANES), lambda b, l: (0, 0))
    return pl.pallas_call(
        functools.partial(_deltanet_kernel, lb=lb),
        grid=(batch, nlb),
        in_specs=[col(CB_DQ), col(CB_DK), col(CB_DV), col(CB_DZ),
                  pl.BlockSpec((lb, LANES), lambda b, l: (b * nlb + l, 0)),
                  pl.BlockSpec(conv_w.shape, lambda b, l: (0, 0)), row, row, row],
        out_specs=pl.BlockSpec((lb, DN_WIDTH), lambda b, l: (b * nlb + l, 0)),
        out_shape=jax.ShapeDtypeStruct((t, DN_WIDTH), BF16),
        scratch_shapes=[
            pltpu.VMEM((DN_HEADS, HEAD_DIM, HEAD_DIM), F32),
            pltpu.VMEM((3, 16, DN_WIDTH), F32),
            pltpu.VMEM((DN_HEADS, lb, HEAD_DIM), F32),
            pltpu.VMEM((nc, DN_HEADS, 2 * CHUNK, HEAD_DIM), BF16),
            pltpu.VMEM((nc, DN_HEADS, CHUNK, CHUNK), BF16),
            pltpu.VMEM((nc, DN_HEADS, HEAD_DIM, CHUNK), BF16),
            pltpu.VMEM((nc, DN_HEADS, 8, LANES), F32),
            pltpu.VMEM((DUO, 3, CHUNK, DN_WIDTH), F32),
            pltpu.VMEM((DUO, 2, CHUNK, LANES), F32),
            pltpu.VMEM((DUO, LANES, CHUNK), F32),
            pltpu.VMEM((DUO, 2 * DN_HEADS * CHUNK, HEAD_DIM), F32),
        ],
        compiler_params=_cparams(("parallel", "arbitrary")),
        name="deltanet",
    )(main, main, main, main, small, conv_w, alog_row, dtb_row, dn_norm_w)


def _attention_kernel(q_ref, k_ref, v_ref, tc_ref, ts_ref, o_ref,
                      qr_scr, kr_scr, v_scr, og_scr, lse_scr, *, seq):
    group = pl.program_id(2)
    rows = 256
    half = ROPE_DIM // 2
    lane = lax.broadcasted_iota(jnp.int32, (rows, LANES), 1)
    src_lane = lax.broadcasted_iota(jnp.int32, (LANES, LANES), 0)
    dst_lane = lax.broadcasted_iota(jnp.int32, (LANES, LANES), 1)
    want = jnp.where(dst_lane < half, dst_lane + half, jnp.where(dst_lane < ROPE_DIM, dst_lane - half, -1))
    swap = jnp.where(src_lane == want, 1.0, 0.0).astype(BF16)

    def rope(i, carry):
        r0 = pl.multiple_of(i * rows, rows)
        tc = tc_ref[pl.ds(r0, rows), :]
        ts = ts_ref[pl.ds(r0, rows), :]
        xq = q_ref[pl.ds(r0, rows), :].astype(F32)
        partner_q = jnp.where(lane < half, pltpu.roll(xq, LANES - half, axis=1),
                              pltpu.roll(xq, half, axis=1))
        qr_scr[pl.ds(r0, rows), :] = (xq * tc + partner_q * ts) * (HEAD_DIM ** -0.5 * LOG2_E)
        xk = k_ref[pl.ds(r0, rows), :]
        kr_scr[pl.ds(r0, rows), :] = xk.astype(F32) * tc + _dot(xk, swap) * ts
        v_scr[pl.ds(r0, rows), :] = v_ref[pl.ds(r0, rows), :].astype(F32)
        return carry

    lax.fori_loop(0, seq // rows, rope, 0, unroll=4)

    ii = lax.broadcasted_iota(jnp.int32, (ATT_BLOCK, 2 * ATT_BLOCK), 0)
    jj = lax.broadcasted_iota(jnp.int32, (ATT_BLOCK, 2 * ATT_BLOCK), 1)
    dist = ii + ATT_BLOCK - jj
    band_mask = jnp.where(dist >= 0, jnp.where(dist <= ATT_BLOCK, 0.0, NEG_BIG), NEG_BIG)
    prev_half = jnp.where(jj < ATT_BLOCK, 1.0, 0.0)
    per_iter = 8

    order = _attention_order()
    stored = order[:-1]

    def run_group(gi, dil, last):
        nb = seq // (dil * ATT_BLOCK)
        span = dil * ATT_BLOCK

        def rows_of(start):
            if dil == 1:
                return pl.ds(start, ATT_BLOCK)
            return pl.ds(start, ATT_BLOCK, stride=dil)

        def blocks(it, carry):
            where = []
            for u in range(per_iter):
                t = it * per_iter + u
                r = lax.shift_right_logical(t, nb.bit_length() - 1)
                n = jnp.bitwise_and(t, nb - 1)
                start = r + n * span
                if dil == 1:
                    start = pl.multiple_of(n * span, ATT_BLOCK)
                where.append((n, start, jnp.maximum(start - span, r)))
            run = min(nb, per_iter)
            kcur = [kr_scr[rows_of(start), :].astype(BF16) for _, start, _ in where]
            vcur = [v_scr[rows_of(start), :].astype(BF16) for _, start, _ in where]

            def prev_of(u, cur, ref):
                if u % run:
                    return cur[u - 1]
                if nb <= per_iter:
                    return cur[u]
                return ref[rows_of(where[u][2]), :].astype(BF16)

            scores = []
            for u, (n, start, prev) in enumerate(where):
                qb = qr_scr[rows_of(start), :].astype(BF16)
                kcat = jnp.concatenate([prev_of(u, kcur, kr_scr), kcur[u]], axis=0)
                s = _dot_nt(qb, kcat)
                no_prev = jnp.where(n > 0, 0.0, NEG_BIG)
                s = s + (band_mask + prev_half * no_prev)
                scores.append(s)
            probs = []
            for s in scores:
                m = jnp.max(s, axis=1, keepdims=True)
                p = jnp.exp2(s - m)
                probs.append((m, p, jnp.sum(p, axis=1, keepdims=True)))
            for u, ((n, start, prev), (m, p, den)) in enumerate(zip(where, probs)):
                vcat = jnp.concatenate([prev_of(u, vcur, v_scr), vcur[u]], axis=0)
                acc = _dot(p.astype(BF16), vcat)
                out = acc / den
                lse = jnp.broadcast_to(m + jnp.log2(den), (ATT_BLOCK, LANES))
                if not last:
                    og_scr[gi, rows_of(start), :] = out
                    lse_scr[gi, rows_of(start), :] = lse
                    continue
                outs = [og_scr[g, rows_of(start), :] for g in stored] + [out]
                lses = [lse_scr[g, rows_of(start), :] for g in stored] + [lse]
                mx = functools.reduce(jnp.maximum, lses)
                es = [jnp.exp2(l - mx) for l in lses]
                num = functools.reduce(lambda a, b: a + b, [e * o for e, o in zip(es, outs)])
                o_ref[rows_of(start), :] = (
                    num / functools.reduce(lambda a, b: a + b, es)).astype(o_ref.dtype)
            return carry

        lax.fori_loop(0, seq // (ATT_BLOCK * per_iter), blocks, 0)

    for step, gi in enumerate(order):
        @pl.when(group == step)
        def _(gi=gi, step=step):
            run_group(gi, ATT_GROUPS[gi][1], last=(step == len(order) - 1))


def _attention_order():
    order = sorted(range(len(ATT_GROUPS)), key=lambda g: -ATT_GROUPS[g][1])
    assert ATT_GROUPS[order[-1]][1] == 1
    return order


def _attention(main, rope_cos, rope_sin, *, batch, seq):
    t = batch * seq
    ng = len(ATT_GROUPS)
    order = _attention_order()
    assert order == list(range(ng - 1, -1, -1))

    def col(cb):
        return pl.BlockSpec(
            (seq, HEAD_DIM),
            lambda b, h, g, cb=cb: (b, cb + (ng - 1 - g) * ATT_HEADS_PER_GROUP + h))

    tab = pl.BlockSpec((seq, LANES), lambda b, h, g: (0, 0))
    return pl.pallas_call(
        functools.partial(_attention_kernel, seq=seq),
        grid=(batch, ATT_HEADS_PER_GROUP, ng),
        in_specs=[col(CB_AQ), col(CB_AK), col(CB_AV), tab, tab],
        out_specs=pl.BlockSpec((seq, HEAD_DIM), lambda b, h, g: (b, h)),
        out_shape=jax.ShapeDtypeStruct((t, ATT_OUT_WIDTH), BF16),
        scratch_shapes=[
            pltpu.VMEM((seq, HEAD_DIM), F32),
            pltpu.VMEM((seq, HEAD_DIM), F32),
            pltpu.VMEM((seq, HEAD_DIM), F32),
            pltpu.VMEM((ng, seq, HEAD_DIM), F32),
            pltpu.VMEM((ng, seq, LANES), F32),
        ],
        compiler_params=_cparams(("parallel", "parallel", "arbitrary")),
        name="dilated_attention",
    )(main, main, main, rope_cos, rope_sin)


def _rope_tables(seq):
    half = ROPE_DIM // 2
    inv_freq = np.power(ROPE_THETA, -np.arange(half, dtype=np.float64) * (2.0 / ROPE_DIM))
    ang = np.arange(seq, dtype=np.float64)[:, None] * inv_freq[None, :]
    cos, sin = np.cos(ang), np.sin(ang)
    rest = HEAD_DIM - ROPE_DIM
    tc = np.concatenate([cos, cos, np.ones((seq, rest))], axis=1)
    ts = np.concatenate([-sin, sin, np.zeros((seq, rest))], axis=1)
    return jnp.asarray(tc, F32), jnp.asarray(ts, F32)


def _merge_ffn_kernel(x_ref, oa_ref, ob_ref, ga_ref, gb_ref, wa_ref, wb_ref, wo_ref,
                      nw_ref, wg_ref, wu_ref, wd_ref, fw_ref, o_ref, *, final_norm, tf):
    ya = _dot(oa_ref[...], wa_ref[...])
    yb = _dot(ob_ref[...], wb_ref[...])
    merged = (_sigmoid(ga_ref[...].astype(F32)) * ya + _sigmoid(gb_ref[...].astype(F32)) * yb)
    x = x_ref[...] + _dot(merged.astype(BF16), wo_ref[...])
    h = (x * lax.rsqrt(jnp.mean(x * x, axis=-1, keepdims=True) + EPS) * nw_ref[...]).astype(BF16)
    d_ff = wd_ref.shape[0]
    tiles = [slice(f0, min(f0 + tf, d_ff)) for f0 in range(0, d_ff, tf)]
    y = x
    pending = None
    for cols in tiles + [None]:
        issued = None if cols is None else (_dot(h, wg_ref[:, cols]), _dot(h, wu_ref[:, cols]))
        if pending is not None:
            pcols, (gate, up) = pending
            y = y + _dot((_silu(gate) * up).astype(BF16), wd_ref[pcols, :])
        pending = (cols, issued)
    if final_norm:
        y = y * lax.rsqrt(jnp.mean(y * y, axis=-1, keepdims=True) + EPS) * fw_ref[...]
    o_ref[...] = y


def _merge_ffn(x2, oa, ob, main, w_a, w_b, w_o, norm_w, w_gate_up, w_down, final_w, *,
               tm, tf, final_norm):
    t, d = x2.shape
    d_ff = w_down.shape[0]
    gate_cb = CB_GA * LANES // d
    once = pl.Buffered(1)

    def weight(shape, col=0):
        return pl.BlockSpec(shape, lambda i, col=col: (0, col), pipeline_mode=once)

    row = pl.BlockSpec((1, d), lambda i: (0, 0))
    return pl.pallas_call(
        functools.partial(_merge_ffn_kernel, final_norm=final_norm, tf=tf),
        grid=(t // tm,),
        in_specs=[
            pl.BlockSpec((tm, d), lambda i: (i, 0)),
            pl.BlockSpec((tm, DN_WIDTH), lambda i: (i, 0)),
            pl.BlockSpec((tm, ATT_OUT_WIDTH), lambda i: (i, 0)),
            pl.BlockSpec((tm, d), lambda i: (i, gate_cb)),
            pl.BlockSpec((tm, d), lambda i: (i, gate_cb + 1)),
            weight(w_a.shape), weight(w_b.shape), weight(w_o.shape),
            row, weight((d, d_ff)), weight((d, d_ff), 1), weight((d_ff, d)), row,
        ],
        out_specs=pl.BlockSpec((tm, d), lambda i: (i, 0)),
        out_shape=jax.ShapeDtypeStruct((t, d), F32),
        compiler_params=_cparams(("parallel",)),
        name="merge_ffn",
    )(x2, oa, ob, main, main, w_a, w_b, w_o, norm_w, w_gate_up, w_gate_up, w_down, final_w)


def _pad_lanes(v, offset):
    out = jnp.zeros((1, LANES), F32)
    return lax.dynamic_update_slice(out, v.reshape(1, -1).astype(F32), (0, offset))


def kernel(x, norm1_w, w_in, conv_w, a_log, dt_bias, dn_norm_w, w_proj_a, w_proj_b, w_out,
           norm2_w, w_gate_up, w_down, final_norm_w):
    batch, seq, d_model = x.shape
    depth = w_in.shape[0]
    t = batch * seq
    assert d_model == 8 * LANES and seq % (ATT_GROUPS[-1][1] * ATT_BLOCK) == 0
    assert w_in.shape[2] == SRC_GATES + 2 * d_model

    rope_cos, rope_sin = _rope_tables(seq)
    x2 = x.reshape(t, d_model)
    for i in range(depth):
        w_in_t = jnp.swapaxes(w_in[i], 0, 1).astype(BF16)
        main, small = _in_projection(x2, norm1_w[i].reshape(1, d_model), w_in_t,
                                     tm=PROJ_ROWS, tn=PROJ_COLS)
        oa = _deltanet(main, small, conv_w[i], _pad_lanes(a_log[i], DN_HEADS),
                       _pad_lanes(dt_bias[i], DN_HEADS), dn_norm_w[i].reshape(1, HEAD_DIM),
                       batch=batch, seq=seq, lb=DN_BLOCK)
        ob = _attention(main, rope_cos, rope_sin, batch=batch, seq=seq)
        x2 = _merge_ffn(x2, oa, ob, main, w_proj_a[i].astype(BF16), w_proj_b[i].astype(BF16),
                        w_out[i].astype(BF16), norm2_w[i].reshape(1, d_model),
                        w_gate_up[i].astype(BF16), w_down[i].astype(BF16),
                        final_norm_w.reshape(1, d_model), tm=TAIL_ROWS, tf=FFN_COLS,
                        final_norm=(i == depth - 1))
    return x2.reshape(batch, seq, d_model)
```
